```python
import math
import jax, jax.numpy as jnp
from jax import lax
import numpy as np

D_MODEL = 1024
BATCH = 8
SEQ = 2048
DEPTH = 2

GRID_W = 64
NA_HEADS = 8
NA_HEAD_DIM = 64
NA_WIN_ROWS = 8
NA_WIN_COLS = 16
NA_WIDTH = NA_HEADS * NA_HEAD_DIM
DIFF_HEADS = 4
DIFF_QK_DIM = 64
DIFF_V_DIM = 2 * DIFF_QK_DIM
DIFF_QK_WIDTH = DIFF_HEADS * 2 * DIFF_QK_DIM
DIFF_V_WIDTH = DIFF_HEADS * DIFF_V_DIM
Q_BLOCK = 128
T5_BUCKETS = 32
T5_MAX_DIST = 128
D_FF = 4 * D_MODEL
N_GATES = 2
W_IN_COLS = 3 * NA_WIDTH + 2 * DIFF_QK_WIDTH + DIFF_V_WIDTH + N_GATES * D_MODEL
RMS_EPS = 1e-6
NEG_INF = -1e30

kernel_name = "hybrid_na_diffattn_gated_encoder"


def rmsnorm(x, g):
    xf = x.astype(jnp.float32)
    y = xf * lax.rsqrt(jnp.mean(xf * xf, axis=-1, keepdims=True) + RMS_EPS)
    return (y * g.astype(jnp.float32)).astype(x.dtype)


def t5_bucket(rel):
    half = T5_BUCKETS // 2
    max_exact = half // 2
    ret = jnp.where(rel > 0, half, 0)
    n = jnp.abs(rel)
    nf = jnp.maximum(n, 1).astype(jnp.float32)
    large = max_exact + (jnp.log(nf / max_exact) / math.log(T5_MAX_DIST / max_exact)
                         * (half - max_exact)).astype(jnp.int32)
    large = jnp.minimum(large, half - 1)
    return ret + jnp.where(n < max_exact, n, large)


def neighbourhood_attention(q, k, v, rpb):
    B, S, H, dh = q.shape
    rows = S // GRID_W
    wr = min(NA_WIN_ROWS, rows)
    q = q.reshape(B, rows, GRID_W, H, dh)
    k = k.reshape(B, rows, GRID_W, H, dh)
    v = v.reshape(B, rows, GRID_W, H, dh)
    r = jnp.arange(rows)
    r0 = jnp.clip(r - wr // 2, 0, rows - wr)
    row_idx = r0[:, None] + jnp.arange(wr)[None, :]
    kg = k[:, row_idx].reshape(B, rows, wr * GRID_W, H, dh)
    vg = v[:, row_idx].reshape(B, rows, wr * GRID_W, H, dh)
    c = jnp.arange(GRID_W)
    c0 = jnp.clip(c - NA_WIN_COLS // 2, 0, GRID_W - NA_WIN_COLS)
    kc = c[None, :]
    valid = (kc >= c0[:, None]) & (kc < c0[:, None] + NA_WIN_COLS)
    mask = jnp.tile(valid, (1, wr))
    dr = row_idx - r[:, None]
    dc = jnp.clip(kc - c[:, None], -(NA_WIN_COLS - 1), NA_WIN_COLS - 1)
    bias = rpb[:, (dr + NA_WIN_ROWS - 1)[:, :, None, None],
               (dc + NA_WIN_COLS - 1)[None, None, :, :]]
    bias = bias.transpose(0, 1, 3, 2, 4).reshape(H, rows, GRID_W, wr * GRID_W)
    scale = 1.0 / math.sqrt(dh)
    logits = jnp.einsum('brqhd,brkhd->bhrqk', q, kg).astype(jnp.float32) * scale
    logits = logits + bias.astype(jnp.float32)[None]
    logits = jnp.where(mask, logits, NEG_INF)
    p = jax.nn.softmax(logits, axis=-1).astype(v.dtype)
    out = jnp.einsum('bhrqk,brkhd->brqhd', p, vg)
    return out.reshape(B, S, H * dh)


def differential_attention(q1, q2, k1, k2, v, t5_table, lam):
    B, S, H, dk = q1.shape
    nblk = S // Q_BLOCK
    scale = 1.0 / math.sqrt(dk)
    qs = jnp.stack([q1, q2], axis=0).reshape(2, B, nblk, Q_BLOCK, H, dk)
    qs = jnp.moveaxis(qs, 2, 0)
    starts = jnp.arange(nblk, dtype=jnp.int32) * Q_BLOCK
    key_pos = jnp.arange(S, dtype=jnp.int32)

    def block(args):
        qb, start = args
        qpos = start + jnp.arange(Q_BLOCK, dtype=jnp.int32)
        bucket = t5_bucket(key_pos[None, :] - qpos[:, None])
        bias = jnp.transpose(t5_table[bucket], (2, 0, 1)).astype(jnp.float32)
        s1 = jnp.einsum('bqhd,bkhd->bhqk', qb[0], k1).astype(jnp.float32) * scale + bias
        s2 = jnp.einsum('bqhd,bkhd->bhqk', qb[1], k2).astype(jnp.float32) * scale + bias
        p = (jax.nn.softmax(s1, axis=-1) - lam * jax.nn.softmax(s2, axis=-1)).astype(v.dtype)
        return jnp.einsum('bhqk,bkhd->bqhd', p, v)

    out = lax.map(block, (qs, starts))
    return jnp.moveaxis(out, 0, 1).reshape(B, S, H, v.shape[-1])


def setup_inputs(seed: int = 0) -> dict:
    key = jax.random.key(seed)
    ks = jax.random.split(key, 16)
    f32 = jnp.float32

    def nrm(k, shape, s):
        return jax.random.normal(k, shape, f32) * s

    return {
        "x": nrm(ks[0], (BATCH, SEQ, D_MODEL), 1.0),
        "t5_bias": nrm(ks[1], (T5_BUCKETS, DIFF_HEADS), 0.5),
        "final_norm_g": 1.0 + nrm(ks[2], (D_MODEL,), 0.02),
        "norm1_g": 1.0 + nrm(ks[3], (DEPTH, D_MODEL), 0.02),
        "w_in": nrm(ks[4], (DEPTH, D_MODEL, W_IN_COLS), D_MODEL ** -0.5),
        "na_rpb": nrm(ks[5], (DEPTH, NA_HEADS, 2 * NA_WIN_ROWS - 1, 2 * NA_WIN_COLS - 1), 0.2),
        "diff_lambda": nrm(ks[6], (DEPTH, 4, DIFF_QK_DIM), 0.1),
        "diff_subln_g": 1.0 + nrm(ks[7], (DEPTH, DIFF_V_DIM), 0.02),
        "w_na_o": nrm(ks[8], (DEPTH, NA_WIDTH, D_MODEL), NA_WIDTH ** -0.5),
        "w_diff_o": nrm(ks[9], (DEPTH, DIFF_V_WIDTH, D_MODEL), DIFF_V_WIDTH ** -0.5),
        "w_out": nrm(ks[10], (DEPTH, D_MODEL, D_MODEL), D_MODEL ** -0.5),
        "norm2_g": 1.0 + nrm(ks[11], (DEPTH, D_MODEL), 0.02),
        "w_ff1": nrm(ks[12], (DEPTH, D_MODEL, D_FF), D_MODEL ** -0.5),
        "w_ff2": nrm(ks[13], (DEPTH, D_FF, D_MODEL), D_FF ** -0.5),
    }


def reference(x, t5_bias, final_norm_g, norm1_g, w_in, na_rpb, diff_lambda, diff_subln_g,
              w_na_o, w_diff_o, w_out, norm2_g, w_ff1, w_ff2):
    B, S, D = x.shape
    splits = np.cumsum([NA_WIDTH, NA_WIDTH, NA_WIDTH, DIFF_QK_WIDTH, DIFF_QK_WIDTH,
                        DIFF_V_WIDTH, D_MODEL]).tolist()
    for layer in range(DEPTH):
        h = rmsnorm(x, norm1_g[layer])
        proj = jnp.einsum('bsd,de->bse', h, w_in[layer])
        qa, ka, va, qd, kd, vd, ga, gd = jnp.split(proj, splits, axis=-1)

        shp_a = (B, S, NA_HEADS, NA_HEAD_DIM)
        y_na = neighbourhood_attention(qa.reshape(shp_a), ka.reshape(shp_a),
                                       va.reshape(shp_a), na_rpb[layer])

        qd = qd.reshape(B, S, DIFF_HEADS, 2, DIFF_QK_DIM)
        kd = kd.reshape(B, S, DIFF_HEADS, 2, DIFF_QK_DIM)
        vd = vd.reshape(B, S, DIFF_HEADS, DIFF_V_DIM)
        lam_init = 0.8 - 0.6 * math.exp(-0.3 * layer)
        lp = diff_lambda[layer].astype(jnp.float32)
        lam = (jnp.exp(jnp.sum(lp[0] * lp[1])) - jnp.exp(jnp.sum(lp[2] * lp[3]))
               + lam_init)
        od = differential_attention(qd[..., 0, :], qd[..., 1, :], kd[..., 0, :],
                                    kd[..., 1, :], vd, t5_bias, lam)
        od = rmsnorm(od, diff_subln_g[layer]) * (1.0 - lam_init)
        y_diff = od.reshape(B, S, DIFF_V_WIDTH)

        b_na = jnp.einsum('bse,ed->bsd', y_na, w_na_o[layer])
        b_diff = jnp.einsum('bse,ed->bsd', y_diff, w_diff_o[layer])
        merged = jax.nn.sigmoid(ga) * b_na + jax.nn.sigmoid(gd) * b_diff
        x = x + jnp.einsum('bsd,de->bse', merged, w_out[layer])

        h2 = rmsnorm(x, norm2_g[layer])
        u = jnp.square(jax.nn.relu(jnp.einsum('bsd,df->bsf', h2, w_ff1[layer])))
        x = x + jnp.einsum('bsf,fd->bsd', u, w_ff2[layer])
    return rmsnorm(x, final_norm_g)
```

```python
import functools
import math

import numpy as np
import jax
import jax.numpy as jnp
from jax import lax
from jax.experimental import pallas as pl
from jax.experimental.pallas import tpu as pltpu

GRID_W = 64
NA_HEADS = 8
NA_HEAD_DIM = 64
NA_WIN_ROWS = 8
NA_WIN_COLS = 16
DIFF_HEADS = 4
DIFF_QK_DIM = 64
DIFF_V_DIM = 2 * DIFF_QK_DIM
T5_BUCKETS = 32
T5_MAX_DIST = 128
RMS_EPS = 1e-6
NEG_INF = -1e30

NA_WIDTH = NA_HEADS * NA_HEAD_DIM
DIFF_WIDTH = DIFF_HEADS * DIFF_V_DIM
QKV_WIDTH = 3 * NA_WIDTH + 3 * DIFF_WIDTH

LANES = 128
V7X_VMEM_BYTES = 64 * 2**20

TM_PROJ = 512
TM_MERGE = 512
TM_MLP = 1024
TF_MLP = 1024
NA_ROW_BLOCK = 4
NA_WIN_BLOCKS = 3
TQ_DIFF = 256
BAND_PAD = LANES

BF16 = jnp.bfloat16
F32 = jnp.float32


def _vmem_limit(nbytes):
    assert nbytes < V7X_VMEM_BYTES
    return int(nbytes)


def _rmsnorm_f32(x, g):
    return (x * lax.rsqrt(jnp.mean(x * x, axis=-1, keepdims=True) + RMS_EPS)) * g


def _in_proj_kernel(x_ref, g_ref, w_ref, qkv_ref, gate_ref, *, chunk):
    hb = _rmsnorm_f32(x_ref[...], g_ref[...]).astype(BF16)
    n_qkv = qkv_ref.shape[1]
    n_gate = gate_ref.shape[1]
    for c in range(n_qkv // chunk):
        cs = slice(c * chunk, (c + 1) * chunk)
        qkv_ref[:, cs] = jnp.dot(hb, w_ref[:, cs], preferred_element_type=F32).astype(BF16)
    for c in range(n_gate // chunk):
        cs = slice(c * chunk, (c + 1) * chunk)
        ws = slice(n_qkv + c * chunk, n_qkv + (c + 1) * chunk)
        gate_ref[:, cs] = jnp.dot(hb, w_ref[:, ws], preferred_element_type=F32)


def _in_proj(xt, g, w):
    T, D = xt.shape
    n_all = w.shape[1]
    n_gate = n_all - QKV_WIDTH
    tm = TM_PROJ
    return pl.pallas_call(
        functools.partial(_in_proj_kernel, chunk=1024),
        out_shape=(jax.ShapeDtypeStruct((T, QKV_WIDTH), BF16),
                   jax.ShapeDtypeStruct((T, n_gate), F32)),
        grid=(T // tm,),
        in_specs=[
            pl.BlockSpec((tm, D), lambda i: (i, 0)),
            pl.BlockSpec((1, D), lambda i: (0, 0)),
            pl.BlockSpec((D, n_all), lambda i: (0, 0)),
        ],
        out_specs=(pl.BlockSpec((tm, QKV_WIDTH), lambda i: (i, 0)),
                   pl.BlockSpec((tm, n_gate), lambda i: (i, 0))),
        compiler_params=pltpu.CompilerParams(
            dimension_semantics=("arbitrary",), vmem_limit_bytes=_vmem_limit(56 * 2**20)),
        name="in_proj",
    )(xt, g.reshape(1, D), w)


def _na_geometry(rows):
    rb_rows = NA_ROW_BLOCK
    n_rb = rows // rb_rows
    win_rows = NA_WIN_BLOCKS * rb_rows
    wr = min(NA_WIN_ROWS, rows)
    kb = np.clip(np.arange(n_rb) - 1, 0, n_rb - NA_WIN_BLOCKS)
    dr_idx = np.zeros((n_rb, rb_rows, win_rows), np.int32)
    valid = np.zeros((n_rb, rb_rows, win_rows), bool)
    for rb in range(n_rb):
        w0 = kb[rb] * rb_rows
        for ri in range(rb_rows):
            r = rb * rb_rows + ri
            r0 = min(max(r - wr // 2, 0), rows - wr)
            assert w0 <= r0 and r0 + wr <= w0 + win_rows
            for wj in range(win_rows):
                krow = w0 + wj
                valid[rb, ri, wj] = r0 <= krow < r0 + wr
                dr_idx[rb, ri, wj] = min(max(krow - r + NA_WIN_ROWS - 1, 0), 2 * NA_WIN_ROWS - 2)
    for rb in range(2, n_rb - 1):
        assert (valid[rb] == valid[1]).all() and (dr_idx[rb] == dr_idx[1]).all()
    classes = [0, 1, n_rb - 1]
    return n_rb, kb, dr_idx[classes], valid[classes]


def _na_bias_table(na_rpb, rows):
    depth, H = na_rpb.shape[:2]
    _, _, dr_idx, row_valid = _na_geometry(rows)
    n_cls, rb_rows, win_rows = dr_idx.shape
    c = np.arange(GRID_W)
    dc = np.clip(c[None, :] - c[:, None], -(NA_WIN_COLS - 1), NA_WIN_COLS - 1) + NA_WIN_COLS - 1
    c0 = np.clip(c - NA_WIN_COLS // 2, 0, GRID_W - NA_WIN_COLS)
    col_valid = (c[None, :] >= c0[:, None]) & (c[None, :] < c0[:, None] + NA_WIN_COLS)
    t = na_rpb[:, :, :, dc]
    t = jnp.take(t, jnp.asarray(dr_idx.reshape(-1)), axis=2)
    t = t.reshape(depth, H, n_cls, rb_rows, win_rows, GRID_W, GRID_W)
    t = t.transpose(0, 2, 1, 3, 5, 4, 6)
    mask = row_valid[:, None, :, None, :, None] & col_valid[None, None, None, :, None, :]
    t = jnp.where(jnp.asarray(mask)[None], t, NEG_INF)
    return t.reshape(depth * n_cls, H, rb_rows * GRID_W, win_rows * GRID_W).astype(F32)


def _na_kernel(q_ref, k0_ref, k1_ref, k2_ref, v0_ref, v1_ref, v2_ref, bias_ref, o_ref):
    rq = q_ref.shape[0]
    scale = 1.0 / math.sqrt(NA_HEAD_DIM)
    lo = lax.broadcasted_iota(jnp.int32, (rq, LANES), 1) < NA_HEAD_DIM
    heads_per_vreg = LANES // NA_HEAD_DIM
    for hp in range(NA_HEADS // heads_per_vreg):
        cs = slice(hp * LANES, (hp + 1) * LANES)
        q = q_ref[:, cs] * scale
        k = jnp.concatenate([k0_ref[:, cs], k1_ref[:, cs], k2_ref[:, cs]], axis=0)
        v = jnp.concatenate([v0_ref[:, cs], v1_ref[:, cs], v2_ref[:, cs]], axis=0)
        outs = []
        for e in range(heads_per_vreg):
            qm = jnp.where(lo if e == 0 else jnp.logical_not(lo), q, jnp.zeros_like(q))
            s = lax.dot_general(qm, k, (((1,), (1,)), ((), ())), preferred_element_type=F32)
            s = s + bias_ref[heads_per_vreg * hp + e]
            m = jnp.max(s, axis=-1, keepdims=True)
            p = jnp.exp(s - m)
            l = jnp.sum(p, axis=-1, keepdims=True)
            outs.append(jnp.dot(p.astype(BF16), v, preferred_element_type=F32) / l)
        o_ref[:, cs] = jnp.where(lo, outs[0], outs[1]).astype(BF16)


def _na_attention(qkv3, bias_table, layer):
    B, S, _ = qkv3.shape
    rows = S // GRID_W
    n_rb, kb, _, _ = _na_geometry(rows)
    rq = NA_ROW_BLOCK * GRID_W
    assert NA_HEADS * NA_HEAD_DIM == NA_WIDTH and LANES % NA_HEAD_DIM == 0
    k_col, v_col = 1, 2
    n_kb = n_rb - NA_WIN_BLOCKS

    def kv_spec(col, t):
        return pl.BlockSpec((None, rq, NA_WIDTH),
                            lambda rb, b: (b, jnp.clip(rb - 1, 0, n_kb) + t, col))

    def bias_map(rb, b):
        cls = (rb > 0).astype(jnp.int32) + (rb == n_rb - 1).astype(jnp.int32)
        return (layer * 3 + cls, 0, 0, 0)

    wk = NA_WIN_BLOCKS * rq
    return pl.pallas_call(
        _na_kernel,
        out_shape=jax.ShapeDtypeStruct((B, S, NA_WIDTH), BF16),
        grid=(n_rb, B),
        in_specs=[pl.BlockSpec((None, rq, NA_WIDTH), lambda rb, b: (b, rb, 0))]
        + [kv_spec(k_col, t) for t in range(NA_WIN_BLOCKS)]
        + [kv_spec(v_col, t) for t in range(NA_WIN_BLOCKS)]
        + [pl.BlockSpec((None, NA_HEADS, rq, wk), bias_map)],
        out_specs=pl.BlockSpec((None, rq, NA_WIDTH), lambda rb, b: (b, rb, 0)),
        compiler_params=pltpu.CompilerParams(
            dimension_semantics=("arbitrary", "arbitrary"),
            vmem_limit_bytes=_vmem_limit(40 * 2**20)),
        name="na_attn",
    )(qkv3, qkv3, qkv3, qkv3, qkv3, qkv3, qkv3, bias_table)


def _t5_bucket_steps(seq):
    half = T5_BUCKETS // 2
    max_exact = half // 2
    rel = np.arange(-(seq - 1), seq)
    n = np.abs(rel)
    nf = np.maximum(n, 1).astype(np.float64)
    large = max_exact + np.floor(
        np.log(nf / max_exact) / math.log(T5_MAX_DIST / max_exact) * (half - max_exact) + 1e-9
    ).astype(np.int64)
    large = np.minimum(large, half - 1)
    bucket = np.where(rel > 0, half, 0) + np.where(n < max_exact, n, large)
    steps = [(int(rel[i]), int(bucket[i])) for i in range(1, len(rel)) if bucket[i] != bucket[i - 1]]
    sat = max(abs(steps[0][0]) + 1, abs(steps[-1][0]))
    return int(bucket[0]), steps, sat


def _diff_kernel(t5_ref, q_ref, k_ref, v_ref, lam_ref, g_ref, o_ref, s_ref, corr_ref, *,
                 lam_init, bucket_steps):
    b, h, qi = pl.program_id(0), pl.program_id(1), pl.program_id(2)
    tq = q_ref.shape[0]
    seq = k_ref.shape[0]
    band = tq + 2 * BAND_PAD
    first_bucket, steps, _ = bucket_steps
    last_bucket = steps[-1][1]
    split = BAND_PAD + tq // 2

    @pl.when((b == 0) & (h == 0) & (qi == 0))
    def _init():
        s_ref[...] = jnp.zeros_like(s_ref)
        row = lax.broadcasted_iota(jnp.int32, (tq, band), 0)
        col = lax.broadcasted_iota(jnp.int32, (tq, band), 1)
        rel = col - BAND_PAD - row
        for hh in range(DIFF_HEADS):
            val = jnp.full((tq, band), t5_ref[first_bucket, hh], F32)
            for thr, bkt in steps:
                val = jnp.where(rel >= thr, t5_ref[bkt, hh], val)
            far = jnp.where(col < split, t5_ref[first_bucket, hh], t5_ref[last_bucket, hh])
            corr_ref[hh] = val - far

    scale = 1.0 / math.sqrt(DIFF_QK_DIM)
    q = q_ref[...] * scale
    lo = lax.broadcasted_iota(jnp.int32, (tq, LANES), 1) < DIFF_QK_DIM
    zero = jnp.zeros_like(q)
    qs = jnp.concatenate([jnp.where(lo, q, zero), jnp.where(lo, zero, q)], axis=0)
    s = lax.dot_general(qs, k_ref[...], (((1,), (1,)), ((), ())), preferred_element_type=F32)
    q0 = qi * tq
    kcol = lax.broadcasted_iota(jnp.int32, (1, seq), 1)
    far = jnp.where(kcol < q0 + tq // 2, t5_ref[first_bucket, h], t5_ref[last_bucket, h])
    s_ref[:, BAND_PAD:BAND_PAD + seq] = s + far
    start = pl.multiple_of(q0, LANES)
    corr = corr_ref[h]
    s_ref[0:tq, pl.ds(start, band)] += corr
    s_ref[tq:2 * tq, pl.ds(start, band)] += corr

    sv = s_ref[:, BAND_PAD:BAND_PAD + seq]
    m = jnp.max(sv, axis=-1, keepdims=True)
    e = jnp.exp(sv - m)
    r = 1.0 / jnp.sum(e, axis=-1, keepdims=True)
    lp = lam_ref[...]
    lam = (jnp.exp(jnp.sum(lp[0:1] * lp[1:2], axis=-1, keepdims=True))
           - jnp.exp(jnp.sum(lp[2:3] * lp[3:4], axis=-1, keepdims=True)) + lam_init)
    p = e[0:tq] * r[0:tq] - e[tq:2 * tq] * (lam * r[tq:2 * tq])
    o = jnp.dot(p.astype(BF16), v_ref[...], preferred_element_type=F32)
    o_ref[...] = (_rmsnorm_f32(o, g_ref[...]) * (1.0 - lam_init)).astype(BF16)


def _diff_attention(qkv3, t5_bias, lam_params, subln_g, lam_init):
    B, S, _ = qkv3.shape
    tq = TQ_DIFF
    bucket_steps = _t5_bucket_steps(S)
    assert bucket_steps[2] <= BAND_PAD and S % tq == 0 and tq % LANES == 0
    assert 2 * DIFF_QK_DIM == LANES and DIFF_V_DIM == LANES
    q_col = 3 * NA_WIDTH // LANES
    k_col = q_col + DIFF_HEADS
    v_col = k_col + DIFF_HEADS
    band = tq + 2 * BAND_PAD
    return pl.pallas_call(
        functools.partial(_diff_kernel, lam_init=lam_init, bucket_steps=bucket_steps),
        out_shape=jax.ShapeDtypeStruct((B, S, DIFF_WIDTH), BF16),
        grid=(B, DIFF_HEADS, S // tq),
        in_specs=[
            pl.BlockSpec(memory_space=pltpu.SMEM),
            pl.BlockSpec((None, tq, LANES), lambda b, h, i: (b, i, q_col + h)),
            pl.BlockSpec((None, S, LANES), lambda b, h, i: (b, 0, k_col + h)),
            pl.BlockSpec((None, S, LANES), lambda b, h, i: (b, 0, v_col + h)),
            pl.BlockSpec((4, DIFF_QK_DIM), lambda b, h, i: (0, 0)),
            pl.BlockSpec((1, DIFF_V_DIM), lambda b, h, i: (0, 0)),
        ],
        out_specs=pl.BlockSpec((None, tq, LANES), lambda b, h, i: (b, i, h)),
        scratch_shapes=[
            pltpu.VMEM((2 * tq, S + 2 * BAND_PAD), F32),
            pltpu.VMEM((DIFF_HEADS, tq, band), F32),
        ],
        compiler_params=pltpu.CompilerParams(
            dimension_semantics=("arbitrary", "arbitrary", "arbitrary"),
            vmem_limit_bytes=_vmem_limit(48 * 2**20)),
        name="diff_attn",
    )(t5_bias, qkv3, qkv3, qkv3, lam_params, subln_g.reshape(1, DIFF_V_DIM))


def _merge_kernel(x_ref, yna_ref, ydf_ref, gate_ref, wna_ref, wdf_ref, wout_ref, o_ref):
    d = x_ref.shape[1]
    b_na = jnp.dot(yna_ref[...], wna_ref[...], preferred_element_type=F32)
    b_df = jnp.dot(ydf_ref[...], wdf_ref[...], preferred_element_type=F32)
    merged = jax.nn.sigmoid(gate_ref[:, 0:d]) * b_na + jax.nn.sigmoid(gate_ref[:, d:2 * d]) * b_df
    o_ref[...] = x_ref[...] + jnp.dot(merged.astype(BF16), wout_ref[...],
                                      preferred_element_type=F32)


def _merge_out(xt, y_na, y_df, gates, w_na, w_df, w_out):
    T, D = xt.shape
    tm = TM_MERGE
    return pl.pallas_call(
        _merge_kernel,
        out_shape=jax.ShapeDtypeStruct((T, D), F32),
        grid=(T // tm,),
        in_specs=[
            pl.BlockSpec((tm, D), lambda i: (i, 0)),
            pl.BlockSpec((tm, NA_WIDTH), lambda i: (i, 0)),
            pl.BlockSpec((tm, DIFF_WIDTH), lambda i: (i, 0)),
            pl.BlockSpec((tm, 2 * D), lambda i: (i, 0)),
            pl.BlockSpec((NA_WIDTH, D), lambda i: (0, 0)),
            pl.BlockSpec((DIFF_WIDTH, D), lambda i: (0, 0)),
            pl.BlockSpec((D, D), lambda i: (0, 0)),
        ],
        out_specs=pl.BlockSpec((tm, D), lambda i: (i, 0)),
        compiler_params=pltpu.CompilerParams(
            dimension_semantics=("arbitrary",), vmem_limit_bytes=_vmem_limit(40 * 2**20)),
        name="merge_out",
    )(xt, y_na, y_df, gates, w_na, w_df, w_out)


def _mlp_kernel(x_ref, g_ref, w1_ref, w2_ref, fg_ref, o_ref, h_ref, *, final):
    j = pl.program_id(1)

    @pl.when(j == 0)
    def _start():
        x = x_ref[...]
        h_ref[...] = _rmsnorm_f32(x, g_ref[...]).astype(BF16)
        o_ref[...] = x

    u = jnp.dot(h_ref[...], w1_ref[...], preferred_element_type=F32)
    u = jnp.square(jnp.maximum(u, 0.0)).astype(BF16)
    o_ref[...] += jnp.dot(u, w2_ref[...], preferred_element_type=F32)

    if final:
        @pl.when(j == pl.num_programs(1) - 1)
        def _finish():
            o_ref[...] = _rmsnorm_f32(o_ref[...], fg_ref[...])


def _mlp(xt, g, w1, w2, final_g, final):
    T, D = xt.shape
    F = w1.shape[1]
    tm, tf = TM_MLP, TF_MLP
    return pl.pallas_call(
        functools.partial(_mlp_kernel, final=final),
        out_shape=jax.ShapeDtypeStruct((T, D), F32),
        grid=(T // tm, F // tf),
        in_specs=[
            pl.BlockSpec((tm, D), lambda i, j: (i, 0)),
            pl.BlockSpec((1, D), lambda i, j: (0, 0)),
            pl.BlockSpec((D, tf), lambda i, j: (0, j)),
            pl.BlockSpec((tf, D), lambda i, j: (j, 0)),
            pl.BlockSpec((1, D), lambda i, j: (0, 0)),
        ],
        out_specs=pl.BlockSpec((tm, D), lambda i, j: (i, 0)),
        scratch_shapes=[pltpu.VMEM((tm, D), BF16)],
        compiler_params=pltpu.CompilerParams(
            dimension_semantics=("arbitrary", "arbitrary"),
            vmem_limit_bytes=_vmem_limit(48 * 2**20)),
        name="mlp",
    )(xt, g.reshape(1, D), w1, w2, final_g.reshape(1, D))


def kernel(x, t5_bias, final_norm_g, norm1_g, w_in, na_rpb, diff_lambda, diff_subln_g, w_na_o,
           w_diff_o, w_out, norm2_g, w_ff1, w_ff2):
    B, S, D = x.shape
    depth = w_in.shape[0]
    T = B * S
    assert w_in.shape[2] == QKV_WIDTH + 2 * D and S % GRID_W == 0
    xt = x.reshape(T, D)
    na_bias = _na_bias_table(na_rpb, S // GRID_W)
    for layer in range(depth):
        qkv, gates = _in_proj(xt, norm1_g[layer], w_in[layer].astype(BF16))
        qkv3 = qkv.reshape(B, S, QKV_WIDTH)
        y_na = _na_attention(qkv3, na_bias, layer)
        lam_init = 0.8 - 0.6 * math.exp(-0.3 * layer)
        y_df = _diff_attention(qkv3, t5_bias, diff_lambda[layer], diff_subln_g[layer], lam_init)
        xt = _merge_out(xt, y_na.reshape(T, NA_WIDTH), y_df.reshape(T, DIFF_WIDTH), gates,
                        w_na_o[layer].astype(BF16), w_diff_o[layer].astype(BF16),
                        w_out[layer].astype(BF16))
        xt = _mlp(xt, norm2_g[layer], w_ff1[layer].astype(BF16), w_ff2[layer].astype(BF16),
                  final_norm_g, final=(layer == depth - 1))
    return xt.reshape(B, S, D)
```

```python
import functools
import math

import numpy as np
import jax
import jax.numpy as jnp
from jax import lax
from jax.experimental import pallas as pl
from jax.experimental.pallas import tpu as pltpu

GRID_W = 64
NA_HEADS = 8
NA_HEAD_DIM = 64
NA_WIN_ROWS = 8
NA_WIN_COLS = 16
DIFF_HEADS = 4
DIFF_QK_DIM = 64
DIFF_V_DIM = 2 * DIFF_QK_DIM
T5_BUCKETS = 32
T5_MAX_DIST = 128
RMS_EPS = 1e-6
NEG_INF = -1e30

NA_WIDTH = NA_HEADS * NA_HEAD_DIM
DIFF_WIDTH = DIFF_HEADS * DIFF_V_DIM
QKV_WIDTH = 3 * NA_WIDTH + 3 * DIFF_WIDTH

LANES = 128
V7X_VMEM_BYTES = 64 * 2**20

TM_PROJ = 512
TM_MERGE = 512
TM_MLP = 1024
TF_MLP = 1024
NA_ROW_BLOCK = 4
NA_WIN_BLOCKS = 3
TQ_DIFF = 256
BAND_PAD = LANES

BF16 = jnp.bfloat16
F32 = jnp.float32


def _vmem_limit(nbytes):
    assert nbytes < V7X_VMEM_BYTES
    return int(nbytes)


def _aligned(index, multiple):
    return index if isinstance(index, int) else pl.multiple_of(index, multiple)


def _rmsnorm_f32(x, g):
    return (x * lax.rsqrt(jnp.mean(x * x, axis=-1, keepdims=True) + RMS_EPS)) * g


def _in_proj_kernel(x_ref, g_ref, w_ref, qkv_ref, gate_ref, *, chunk):
    hb = _rmsnorm_f32(x_ref[...], g_ref[...]).astype(BF16)
    n_qkv = qkv_ref.shape[1]
    n_gate = gate_ref.shape[1]
    for c in range(n_qkv // chunk):
        cs = slice(c * chunk, (c + 1) * chunk)
        qkv_ref[:, cs] = jnp.dot(hb, w_ref[:, cs], preferred_element_type=F32).astype(BF16)
    for c in range(n_gate // chunk):
        cs = slice(c * chunk, (c + 1) * chunk)
        ws = slice(n_qkv + c * chunk, n_qkv + (c + 1) * chunk)
        gate_ref[:, cs] = jnp.dot(hb, w_ref[:, ws], preferred_element_type=F32)


def _in_proj(xt, g, w):
    T, D = xt.shape
    n_all = w.shape[1]
    n_gate = n_all - QKV_WIDTH
    tm = TM_PROJ
    return pl.pallas_call(
        functools.partial(_in_proj_kernel, chunk=1024),
        out_shape=(jax.ShapeDtypeStruct((T, QKV_WIDTH), BF16),
                   jax.ShapeDtypeStruct((T, n_gate), F32)),
        grid=(T // tm,),
        in_specs=[
            pl.BlockSpec((tm, D), lambda i: (i, 0)),
            pl.BlockSpec((1, D), lambda i: (0, 0)),
            pl.BlockSpec((D, n_all), lambda i: (0, 0)),
        ],
        out_specs=(pl.BlockSpec((tm, QKV_WIDTH), lambda i: (i, 0)),
                   pl.BlockSpec((tm, n_gate), lambda i: (i, 0))),
        compiler_params=pltpu.CompilerParams(
            dimension_semantics=("arbitrary",), vmem_limit_bytes=_vmem_limit(56 * 2**20)),
        name="in_proj",
    )(xt, g.reshape(1, D), w)


def _na_geometry(rows):
    rb_rows = NA_ROW_BLOCK
    n_rb = rows // rb_rows
    win_rows = NA_WIN_BLOCKS * rb_rows
    wr = min(NA_WIN_ROWS, rows)
    kb = np.clip(np.arange(n_rb) - 1, 0, n_rb - NA_WIN_BLOCKS)
    dr_idx = np.zeros((n_rb, rb_rows, win_rows), np.int32)
    valid = np.zeros((n_rb, rb_rows, win_rows), bool)
    for rb in range(n_rb):
        w0 = kb[rb] * rb_rows
        for ri in range(rb_rows):
            r = rb * rb_rows + ri
            r0 = min(max(r - wr // 2, 0), rows - wr)
            assert w0 <= r0 and r0 + wr <= w0 + win_rows
            for wj in range(win_rows):
                krow = w0 + wj
                valid[rb, ri, wj] = r0 <= krow < r0 + wr
                dr_idx[rb, ri, wj] = min(max(krow - r + NA_WIN_ROWS - 1, 0), 2 * NA_WIN_ROWS - 2)
    for rb in range(2, n_rb - 1):
        assert (valid[rb] == valid[1]).all() and (dr_idx[rb] == dr_idx[1]).all()
    classes = [0, 1, n_rb - 1]
    return n_rb, kb, dr_idx[classes], valid[classes]


def _na_bias_table(na_rpb, rows):
    depth, H = na_rpb.shape[:2]
    _, _, dr_idx, row_valid = _na_geometry(rows)
    n_cls, rb_rows, win_rows = dr_idx.shape
    c = np.arange(GRID_W)
    dc = np.clip(c[None, :] - c[:, None], -(NA_WIN_COLS - 1), NA_WIN_COLS - 1) + NA_WIN_COLS - 1
    c0 = np.clip(c - NA_WIN_COLS // 2, 0, GRID_W - NA_WIN_COLS)
    col_valid = (c[None, :] >= c0[:, None]) & (c[None, :] < c0[:, None] + NA_WIN_COLS)
    t = na_rpb[:, :, :, dc]
    t = jnp.take(t, jnp.asarray(dr_idx.reshape(-1)), axis=2)
    t = t.reshape(depth, H, n_cls, rb_rows, win_rows, GRID_W, GRID_W)
    t = t.transpose(0, 2, 1, 3, 5, 4, 6)
    mask = row_valid[:, None, :, None, :, None] & col_valid[None, None, None, :, None, :]
    t = jnp.where(jnp.asarray(mask)[None], t, NEG_INF)
    return t.reshape(depth * n_cls, H, rb_rows * GRID_W, win_rows * GRID_W).astype(F32)


def _na_kernel(q_ref, k0_ref, k1_ref, k2_ref, v0_ref, v1_ref, v2_ref, bias_ref, o_ref):
    rq = q_ref.shape[0]
    scale = 1.0 / math.sqrt(NA_HEAD_DIM)
    lo = lax.broadcasted_iota(jnp.int32, (rq, LANES), 1) < NA_HEAD_DIM
    heads_per_vreg = LANES // NA_HEAD_DIM
    for hp in range(NA_HEADS // heads_per_vreg):
        cs = slice(hp * LANES, (hp + 1) * LANES)
        q = q_ref[:, cs] * scale
        k = jnp.concatenate([k0_ref[:, cs], k1_ref[:, cs], k2_ref[:, cs]], axis=0)
        v = jnp.concatenate([v0_ref[:, cs], v1_ref[:, cs], v2_ref[:, cs]], axis=0)
        v_ones = jnp.concatenate([v, jnp.ones_like(v)], axis=1)
        outs = []
        for e in range(heads_per_vreg):
            qm = jnp.where(lo if e == 0 else jnp.logical_not(lo), q, jnp.zeros_like(q))
            s = lax.dot_general(qm, k, (((1,), (1,)), ((), ())), preferred_element_type=F32)
            s = s + bias_ref[heads_per_vreg * hp + e]
            m = jnp.max(s, axis=-1, keepdims=True)
            p = jnp.exp((s - m).astype(BF16))
            ol = jnp.dot(p, v_ones, preferred_element_type=F32)
            outs.append(ol[:, 0:LANES] / ol[:, LANES:2 * LANES])
        o_ref[:, cs] = jnp.where(lo, outs[0], outs[1]).astype(BF16)


def _na_attention(qkv3, bias_table, layer):
    B, S, _ = qkv3.shape
    rows = S // GRID_W
    n_rb, kb, _, _ = _na_geometry(rows)
    rq = NA_ROW_BLOCK * GRID_W
    assert NA_HEADS * NA_HEAD_DIM == NA_WIDTH and LANES % NA_HEAD_DIM == 0
    k_col, v_col = 1, 2
    n_kb = n_rb - NA_WIN_BLOCKS

    def kv_spec(col, t):
        return pl.BlockSpec((None, rq, NA_WIDTH),
                            lambda rb, b: (b, jnp.clip(rb - 1, 0, n_kb) + t, col))

    def bias_map(rb, b):
        cls = (rb > 0).astype(jnp.int32) + (rb == n_rb - 1).astype(jnp.int32)
        return (layer * 3 + cls, 0, 0, 0)

    wk = NA_WIN_BLOCKS * rq
    return pl.pallas_call(
        _na_kernel,
        out_shape=jax.ShapeDtypeStruct((B, S, NA_WIDTH), BF16),
        grid=(n_rb, B),
        in_specs=[pl.BlockSpec((None, rq, NA_WIDTH), lambda rb, b: (b, rb, 0))]
        + [kv_spec(k_col, t) for t in range(NA_WIN_BLOCKS)]
        + [kv_spec(v_col, t) for t in range(NA_WIN_BLOCKS)]
        + [pl.BlockSpec((None, NA_HEADS, rq, wk), bias_map)],
        out_specs=pl.BlockSpec((None, rq, NA_WIDTH), lambda rb, b: (b, rb, 0)),
        compiler_params=pltpu.CompilerParams(
            dimension_semantics=("arbitrary", "arbitrary"),
            vmem_limit_bytes=_vmem_limit(40 * 2**20)),
        name="na_attn",
    )(qkv3, qkv3, qkv3, qkv3, qkv3, qkv3, qkv3, bias_table)


def _t5_bucket_steps(seq):
    half = T5_BUCKETS // 2
    max_exact = half // 2
    rel = np.arange(-(seq - 1), seq)
    n = np.abs(rel)
    nf = np.maximum(n, 1).astype(np.float64)
    large = max_exact + np.floor(
        np.log(nf / max_exact) / math.log(T5_MAX_DIST / max_exact) * (half - max_exact) + 1e-9
    ).astype(np.int64)
    large = np.minimum(large, half - 1)
    bucket = np.where(rel > 0, half, 0) + np.where(n < max_exact, n, large)
    steps = [(int(rel[i]), int(bucket[i])) for i in range(1, len(rel)) if bucket[i] != bucket[i - 1]]
    sat = max(abs(steps[0][0]) + 1, abs(steps[-1][0]))
    return int(bucket[0]), steps, sat


def _diff_kernel(t5_ref, q_ref, k_ref, v_ref, lam_ref, g_ref, o_ref, s0_ref, s1_ref, e0_ref,
                 e1_ref, corr_ref, kfar_ref, *, tq, lam_init, bucket_steps):
    b, h = pl.program_id(0), pl.program_id(1)
    seq = k_ref.shape[0]
    band = tq + 2 * BAND_PAD
    n_t = seq // tq
    first_bucket, steps, _ = bucket_steps
    last_bucket = steps[-1][1]
    split = BAND_PAD + tq // 2
    one_lane = 2 * n_t
    assert one_lane + 2 <= LANES

    @pl.when((b == 0) & (h == 0))
    def _init():
        s0_ref[...] = jnp.zeros_like(s0_ref)
        s1_ref[...] = jnp.zeros_like(s1_ref)
        row = lax.broadcasted_iota(jnp.int32, (tq, band), 0)
        col = lax.broadcasted_iota(jnp.int32, (tq, band), 1)
        rel = col - BAND_PAD - row
        for hh in range(DIFF_HEADS):
            val = jnp.full((tq, band), t5_ref[first_bucket, hh], F32)
            for thr, bkt in steps:
                val = jnp.where(rel >= thr, t5_ref[bkt, hh], val)
            far = jnp.where(col < split, t5_ref[first_bucket, hh], t5_ref[last_bucket, hh])
            corr_ref[hh] = val - far
        key = lax.broadcasted_iota(jnp.int32, (seq, LANES), 0)
        lane = lax.broadcasted_iota(jnp.int32, (seq, LANES), 1)
        tile = jnp.where(lane < n_t, lane, lane - n_t)
        step = jnp.where(key >= tile * tq + tq // 2, 1.0, 0.0)
        ones = jnp.where(lane < one_lane + 2, 1.0, 0.0)
        kfar_ref[...] = jnp.where(lane < one_lane, step, ones).astype(BF16)

    scale = 1.0 / math.sqrt(DIFF_QK_DIM)
    lane = lax.broadcasted_iota(jnp.int32, (tq, LANES), 1)
    lo = lane < DIFF_QK_DIM
    c_neg = t5_ref[first_bucket, h]
    c_step = t5_ref[last_bucket, h] - c_neg
    lp = lam_ref[...]
    lam = (jnp.exp(jnp.sum(lp[0:1] * lp[1:2], axis=-1, keepdims=True))
           - jnp.exp(jnp.sum(lp[2:3] * lp[3:4], axis=-1, keepdims=True)) + lam_init)

    def scores(qi, s_ref):
        row0 = _aligned(qi * tq, tq)
        q = q_ref[pl.ds(row0, tq), :] * scale
        zero = jnp.zeros_like(q)
        far = jnp.where((lane == qi) | (lane == n_t + qi), c_step,
                        jnp.where(lane >= one_lane, c_neg, 0.0))
        far = jnp.where(lane < one_lane + 2, far, 0.0)
        far_hi = far.astype(BF16)
        far_lo = (far - far_hi.astype(F32)).astype(BF16)
        q_far = jnp.where((lane < n_t) | (lane == one_lane), far_hi, far_lo)
        lhs = jnp.concatenate(
            [jnp.concatenate([jnp.where(lo, q, zero), q_far], axis=1),
             jnp.concatenate([jnp.where(lo, zero, q), q_far], axis=1)], axis=0)
        rhs = jnp.concatenate([k_ref[...], kfar_ref[...]], axis=1)
        s_ref[:, BAND_PAD:BAND_PAD + seq] = lax.dot_general(
            lhs, rhs, (((1,), (1,)), ((), ())), preferred_element_type=F32)
        corr = corr_ref[h]
        s_ref[0:tq, pl.ds(row0, band)] += corr
        s_ref[tq:2 * tq, pl.ds(row0, band)] += corr

    def numerators(s_ref, e_ref):
        sv = s_ref[:, BAND_PAD:BAND_PAD + seq]
        m = jnp.max(sv, axis=-1, keepdims=True)
        e_ref[...] = jnp.exp((sv - m).astype(BF16))

    def attend(qi, e_ref):
        v = v_ref[...]
        ol = jnp.dot(e_ref[...], jnp.concatenate([v, jnp.ones_like(v)], axis=1),
                     preferred_element_type=F32)
        o = (ol[0:tq, 0:LANES] / ol[0:tq, LANES:2 * LANES]
             - lam * (ol[tq:2 * tq, 0:LANES] / ol[tq:2 * tq, LANES:2 * LANES]))
        row0 = _aligned(qi * tq, tq)
        o_ref[pl.ds(row0, tq), :] = (_rmsnorm_f32(o, g_ref[...]) * (1.0 - lam_init)).astype(BF16)

    assert n_t % 2 == 0 and n_t >= 4
    scores(0, s0_ref)
    scores(1, s1_ref)
    numerators(s0_ref, e0_ref)

    def pair(j, carry):
        t = 2 * j
        scores(t + 2, s0_ref)
        numerators(s1_ref, e1_ref)
        attend(t, e0_ref)
        scores(t + 3, s1_ref)
        numerators(s0_ref, e0_ref)
        attend(t + 1, e1_ref)
        return carry

    lax.fori_loop(0, n_t // 2 - 1, pair, 0)
    numerators(s1_ref, e1_ref)
    attend(n_t - 2, e0_ref)
    attend(n_t - 1, e1_ref)


def _diff_attention(qkv3, t5_bias, lam_params, subln_g, lam_init):
    B, S, _ = qkv3.shape
    tq = TQ_DIFF
    bucket_steps = _t5_bucket_steps(S)
    assert bucket_steps[2] <= BAND_PAD and S % tq == 0 and tq % LANES == 0
    assert 2 * DIFF_QK_DIM == LANES and DIFF_V_DIM == LANES
    q_col = 3 * NA_WIDTH // LANES
    k_col = q_col + DIFF_HEADS
    v_col = k_col + DIFF_HEADS
    band = tq + 2 * BAND_PAD
    return pl.pallas_call(
        functools.partial(_diff_kernel, tq=tq, lam_init=lam_init, bucket_steps=bucket_steps),
        out_shape=jax.ShapeDtypeStruct((B, S, DIFF_WIDTH), BF16),
        grid=(B, DIFF_HEADS),
        in_specs=[
            pl.BlockSpec(memory_space=pltpu.SMEM),
            pl.BlockSpec((None, S, LANES), lambda b, h: (b, 0, q_col + h)),
            pl.BlockSpec((None, S, LANES), lambda b, h: (b, 0, k_col + h)),
            pl.BlockSpec((None, S, LANES), lambda b, h: (b, 0, v_col + h)),
            pl.BlockSpec((4, DIFF_QK_DIM), lambda b, h: (0, 0)),
            pl.BlockSpec((1, DIFF_V_DIM), lambda b, h: (0, 0)),
        ],
        out_specs=pl.BlockSpec((None, S, LANES), lambda b, h: (b, 0, h)),
        scratch_shapes=[
            pltpu.VMEM((2 * tq, S + 2 * BAND_PAD), F32),
            pltpu.VMEM((2 * tq, S + 2 * BAND_PAD), F32),
            pltpu.VMEM((2 * tq, S), BF16),
            pltpu.VMEM((2 * tq, S), BF16),
            pltpu.VMEM((DIFF_HEADS, tq, band), F32),
            pltpu.VMEM((S, LANES), BF16),
        ],
        compiler_params=pltpu.CompilerParams(
            dimension_semantics=("arbitrary", "arbitrary"),
            vmem_limit_bytes=_vmem_limit(48 * 2**20)),
        name="diff_attn",
    )(t5_bias, qkv3, qkv3, qkv3, lam_params, subln_g.reshape(1, DIFF_V_DIM))


def _merge_kernel(x_ref, yna_ref, ydf_ref, gate_ref, wna_ref, wdf_ref, wout_ref, o_ref):
    d = x_ref.shape[1]
    b_na = jnp.dot(yna_ref[...], wna_ref[...], preferred_element_type=F32)
    b_df = jnp.dot(ydf_ref[...], wdf_ref[...], preferred_element_type=F32)
    merged = jax.nn.sigmoid(gate_ref[:, 0:d]) * b_na + jax.nn.sigmoid(gate_ref[:, d:2 * d]) * b_df
    o_ref[...] = x_ref[...] + jnp.dot(merged.astype(BF16), wout_ref[...],
                                      preferred_element_type=F32)


def _merge_out(xt, y_na, y_df, gates, w_na, w_df, w_out):
    T, D = xt.shape
    tm = TM_MERGE
    return pl.pallas_call(
        _merge_kernel,
        out_shape=jax.ShapeDtypeStruct((T, D), F32),
        grid=(T // tm,),
        in_specs=[
            pl.BlockSpec((tm, D), lambda i: (i, 0)),
            pl.BlockSpec((tm, NA_WIDTH), lambda i: (i, 0)),
            pl.BlockSpec((tm, DIFF_WIDTH), lambda i: (i, 0)),
            pl.BlockSpec((tm, 2 * D), lambda i: (i, 0)),
            pl.BlockSpec((NA_WIDTH, D), lambda i: (0, 0)),
            pl.BlockSpec((DIFF_WIDTH, D), lambda i: (0, 0)),
            pl.BlockSpec((D, D), lambda i: (0, 0)),
        ],
        out_specs=pl.BlockSpec((tm, D), lambda i: (i, 0)),
        compiler_params=pltpu.CompilerParams(
            dimension_semantics=("arbitrary",), vmem_limit_bytes=_vmem_limit(40 * 2**20)),
        name="merge_out",
    )(xt, y_na, y_df, gates, w_na, w_df, w_out)


def _mlp_kernel(x_ref, g_ref, w1_ref, w2_ref, fg_ref, o_ref, h_ref, *, final):
    j = pl.program_id(1)

    @pl.when(j == 0)
    def _start():
        x = x_ref[...]
        h_ref[...] = _rmsnorm_f32(x, g_ref[...]).astype(BF16)
        o_ref[...] = x

    u = jnp.dot(h_ref[...], w1_ref[...], preferred_element_type=F32)
    u = jnp.square(jnp.maximum(u, 0.0)).astype(BF16)
    o_ref[...] += jnp.dot(u, w2_ref[...], preferred_element_type=F32)

    if final:
        @pl.when(j == pl.num_programs(1) - 1)
        def _finish():
            o_ref[...] = _rmsnorm_f32(o_ref[...], fg_ref[...])


def _mlp(xt, g, w1, w2, final_g, final):
    T, D = xt.shape
    F = w1.shape[1]
    tm, tf = TM_MLP, TF_MLP
    return pl.pallas_call(
        functools.partial(_mlp_kernel, final=final),
        out_shape=jax.ShapeDtypeStruct((T, D), F32),
        grid=(T // tm, F // tf),
        in_specs=[
            pl.BlockSpec((tm, D), lambda i, j: (i, 0)),
            pl.BlockSpec((1, D), lambda i, j: (0, 0)),
            pl.BlockSpec((D, tf), lambda i, j: (0, j)),
            pl.BlockSpec((tf, D), lambda i, j: (j, 0)),
            pl.BlockSpec((1, D), lambda i, j: (0, 0)),
        ],
        out_specs=pl.BlockSpec((tm, D), lambda i, j: (i, 0)),
        scratch_shapes=[pltpu.VMEM((tm, D), BF16)],
        compiler_params=pltpu.CompilerParams(
            dimension_semantics=("arbitrary", "arbitrary"),
            vmem_limit_bytes=_vmem_limit(48 * 2**20)),
        name="mlp",
    )(xt, g.reshape(1, D), w1, w2, final_g.reshape(1, D))


def kernel(x, t5_bias, final_norm_g, norm1_g, w_in, na_rpb, diff_lambda, diff_subln_g, w_na_o,
           w_diff_o, w_out, norm2_g, w_ff1, w_ff2):
    B, S, D = x.shape
    depth = w_in.shape[0]
    T = B * S
    assert w_in.shape[2] == QKV_WIDTH + 2 * D and S % GRID_W == 0
    xt = x.reshape(T, D)
    na_bias = _na_bias_table(na_rpb, S // GRID_W)
    for layer in range(depth):
        qkv, gates = _in_proj(xt, norm1_g[layer], w_in[layer].astype(BF16))
        qkv3 = qkv.reshape(B, S, QKV_WIDTH)
        y_na = _na_attention(qkv3, na_bias, layer)
        lam_init = 0.8 - 0.6 * math.exp(-0.3 * layer)
        y_df = _diff_attention(qkv3, t5_bias, diff_lambda[layer], diff_subln_g[layer], lam_init)
        xt = _merge_out(xt, y_na.reshape(T, NA_WIDTH), y_df.reshape(T, DIFF_WIDTH), gates,
                        w_na_o[layer].astype(BF16), w_diff_o[layer].astype(BF16),
                        w_out[layer].astype(BF16))
        xt = _mlp(xt, norm2_g[layer], w_ff1[layer].astype(BF16), w_ff2[layer].astype(BF16),
                  final_norm_g, final=(layer == depth - 1))
    return xt.reshape(B, S, D)
```

```python
import functools
import math

import numpy as np
import jax
import jax.numpy as jnp
from jax import lax
from jax.experimental import pallas as pl
from jax.experimental.pallas import tpu as pltpu

GRID_W = 64
NA_HEADS = 8
NA_HEAD_DIM = 64
NA_WIN_ROWS = 8
NA_WIN_COLS = 16
DIFF_HEADS = 4
DIFF_QK_DIM = 64
DIFF_V_DIM = 2 * DIFF_QK_DIM
T5_BUCKETS = 32
T5_MAX_DIST = 128
RMS_EPS = 1e-6
NEG_INF = -1e30

NA_WIDTH = NA_HEADS * NA_HEAD_DIM
DIFF_WIDTH = DIFF_HEADS * DIFF_V_DIM
QKV_WIDTH = 3 * NA_WIDTH + 3 * DIFF_WIDTH

LANES = 128
V7X_VMEM_BYTES = 64 * 2**20

TM_PROJ = 512
TM_MERGE = 512
TM_MLP = 1024
TF_MLP = 1024
NA_ROW_BLOCK = 4
NA_WIN_BLOCKS = 3
TQ_DIFF = 256
BAND_PAD = LANES

BF16 = jnp.bfloat16
F32 = jnp.float32


def _vmem_limit(nbytes):
    assert nbytes < V7X_VMEM_BYTES
    return int(nbytes)


def _aligned(index, multiple):
    return index if isinstance(index, int) else pl.multiple_of(index, multiple)


def _rmsnorm_f32(x, g):
    return (x * lax.rsqrt(jnp.mean(x * x, axis=-1, keepdims=True) + RMS_EPS)) * g


def _in_proj_kernel(x_ref, g_ref, w_ref, qkv_ref, gate_ref, *, chunk):
    hb = _rmsnorm_f32(x_ref[...], g_ref[...]).astype(BF16)
    n_qkv = qkv_ref.shape[1]
    n_gate = gate_ref.shape[1]
    for c in range(n_qkv // chunk):
        cs = slice(c * chunk, (c + 1) * chunk)
        qkv_ref[:, cs] = jnp.dot(hb, w_ref[:, cs], preferred_element_type=F32).astype(BF16)
    for c in range(n_gate // chunk):
        cs = slice(c * chunk, (c + 1) * chunk)
        ws = slice(n_qkv + c * chunk, n_qkv + (c + 1) * chunk)
        gate_ref[:, cs] = jnp.dot(hb, w_ref[:, ws], preferred_element_type=F32)


def _in_proj(xt, g, w, layer):
    T, D = xt.shape
    n_all = w.shape[2]
    n_gate = n_all - QKV_WIDTH
    tm = TM_PROJ
    return pl.pallas_call(
        functools.partial(_in_proj_kernel, chunk=1024),
        out_shape=(jax.ShapeDtypeStruct((T, QKV_WIDTH), BF16),
                   jax.ShapeDtypeStruct((T, n_gate), F32)),
        grid=(T // tm,),
        in_specs=[
            pl.BlockSpec((tm, D), lambda i: (i, 0)),
            pl.BlockSpec((1, D), lambda i: (0, 0)),
            pl.BlockSpec((None, D, n_all), lambda i: (layer, 0, 0)),
        ],
        out_specs=(pl.BlockSpec((tm, QKV_WIDTH), lambda i: (i, 0)),
                   pl.BlockSpec((tm, n_gate), lambda i: (i, 0))),
        compiler_params=pltpu.CompilerParams(
            dimension_semantics=("arbitrary",), vmem_limit_bytes=_vmem_limit(56 * 2**20)),
        name="in_proj",
    )(xt, g.reshape(1, D), w)


def _na_geometry(rows):
    rb_rows = NA_ROW_BLOCK
    n_rb = rows // rb_rows
    win_rows = NA_WIN_BLOCKS * rb_rows
    wr = min(NA_WIN_ROWS, rows)
    kb = np.clip(np.arange(n_rb) - 1, 0, n_rb - NA_WIN_BLOCKS)
    dr_idx = np.zeros((n_rb, rb_rows, win_rows), np.int32)
    valid = np.zeros((n_rb, rb_rows, win_rows), bool)
    for rb in range(n_rb):
        w0 = kb[rb] * rb_rows
        for ri in range(rb_rows):
            r = rb * rb_rows + ri
            r0 = min(max(r - wr // 2, 0), rows - wr)
            assert w0 <= r0 and r0 + wr <= w0 + win_rows
            for wj in range(win_rows):
                krow = w0 + wj
                valid[rb, ri, wj] = r0 <= krow < r0 + wr
                dr_idx[rb, ri, wj] = min(max(krow - r + NA_WIN_ROWS - 1, 0), 2 * NA_WIN_ROWS - 2)
    for rb in range(2, n_rb - 1):
        assert (valid[rb] == valid[1]).all() and (dr_idx[rb] == dr_idx[1]).all()
    classes = [0, 1, n_rb - 1]
    return n_rb, kb, dr_idx[classes], valid[classes]


def _na_bias_kernel(rpb_ref, o_ref, *, dr_idx, row_valid):
    assert LANES == 2 * GRID_W
    n_cls, rb_rows, win_rows = dr_idx.shape
    c = lax.broadcasted_iota(jnp.int32, (GRID_W, LANES), 0)
    lane = lax.broadcasted_iota(jnp.int32, (GRID_W, LANES), 1)
    left = lane < GRID_W
    kc = jnp.where(left, lane, lane - GRID_W)
    c0 = jnp.clip(c - NA_WIN_COLS // 2, 0, GRID_W - NA_WIN_COLS)
    col_ok = (kc >= c0) & (kc < c0 + NA_WIN_COLS)
    neg = jnp.full((GRID_W, LANES), NEG_INF, F32)

    def half_tile(d, right):
        w = jnp.broadcast_to(rpb_ref[d:d + 1, :], (GRID_W, LANES))
        t = pltpu.roll(w, GRID_W if right else 0, 1, stride=1, stride_axis=0)
        return jnp.where(col_ok, t, neg)

    for cls in range(n_cls):
        for ri in range(rb_rows):
            for p in range(win_rows // 2):
                halves = []
                for side in range(2):
                    wj = 2 * p + side
                    ok = bool(row_valid[cls, ri, wj])
                    halves.append(half_tile(int(dr_idx[cls, ri, wj]), side == 1) if ok else neg)
                o_ref[cls, ri * GRID_W:(ri + 1) * GRID_W, p * LANES:(p + 1) * LANES] = (
                    jnp.where(left, halves[0], halves[1]))


def _na_bias_table(na_rpb, rows):
    depth, H, n_dr, n_dc = na_rpb.shape
    assert n_dr == 2 * NA_WIN_ROWS - 1 and n_dc == 2 * NA_WIN_COLS - 1
    _, _, dr_idx, row_valid = _na_geometry(rows)
    n_cls, rb_rows, win_rows = dr_idx.shape
    rq, wk = rb_rows * GRID_W, win_rows * GRID_W
    lanes = jnp.pad(na_rpb, ((0, 0), (0, 0), (0, 0), (0, LANES - n_dc)), constant_values=NEG_INF)
    lanes = jnp.roll(lanes, -(NA_WIN_COLS - 1), axis=-1)
    return pl.pallas_call(
        functools.partial(_na_bias_kernel, dr_idx=dr_idx, row_valid=row_valid),
        out_shape=jax.ShapeDtypeStruct((depth, n_cls, H, rq, wk), F32),
        grid=(depth, H),
        in_specs=[pl.BlockSpec((None, None, n_dr, LANES), lambda l, h: (l, h, 0, 0))],
        out_specs=pl.BlockSpec((None, n_cls, None, rq, wk), lambda l, h: (l, 0, h, 0, 0)),
        compiler_params=pltpu.CompilerParams(dimension_semantics=("arbitrary", "arbitrary")),
        name="na_bias",
    )(lanes)


def _na_kernel(q_ref, k0_ref, k1_ref, k2_ref, v0_ref, v1_ref, v2_ref, bias_ref, o_ref):
    rq = q_ref.shape[0]
    scale = 1.0 / math.sqrt(NA_HEAD_DIM)
    lo = lax.broadcasted_iota(jnp.int32, (rq, LANES), 1) < NA_HEAD_DIM
    heads_per_vreg = LANES // NA_HEAD_DIM
    for hp in range(NA_HEADS // heads_per_vreg):
        cs = slice(hp * LANES, (hp + 1) * LANES)
        q = q_ref[:, cs] * scale
        k = jnp.concatenate([k0_ref[:, cs], k1_ref[:, cs], k2_ref[:, cs]], axis=0)
        v = jnp.concatenate([v0_ref[:, cs], v1_ref[:, cs], v2_ref[:, cs]], axis=0)
        v_ones = jnp.concatenate([v, jnp.ones_like(v)], axis=1)
        outs = []
        for e in range(heads_per_vreg):
            qm = jnp.where(lo if e == 0 else jnp.logical_not(lo), q, jnp.zeros_like(q))
            s = lax.dot_general(qm, k, (((1,), (1,)), ((), ())), preferred_element_type=F32)
            s = s + bias_ref[heads_per_vreg * hp + e]
            m = jnp.max(s, axis=-1, keepdims=True)
            p = jnp.exp((s - m).astype(BF16))
            ol = jnp.dot(p, v_ones, preferred_element_type=F32)
            outs.append(ol[:, 0:LANES] / ol[:, LANES:2 * LANES])
        o_ref[:, cs] = jnp.where(lo, outs[0], outs[1]).astype(BF16)


def _na_attention(qkv3, bias_table, layer):
    B, S, _ = qkv3.shape
    rows = S // GRID_W
    n_rb, kb, _, _ = _na_geometry(rows)
    rq = NA_ROW_BLOCK * GRID_W
    assert NA_HEADS * NA_HEAD_DIM == NA_WIDTH and LANES % NA_HEAD_DIM == 0
    k_col, v_col = 1, 2
    n_kb = n_rb - NA_WIN_BLOCKS

    def kv_spec(col, t):
        return pl.BlockSpec((None, rq, NA_WIDTH),
                            lambda rb, b: (b, jnp.clip(rb - 1, 0, n_kb) + t, col))

    def bias_map(rb, b):
        cls = (rb > 0).astype(jnp.int32) + (rb == n_rb - 1).astype(jnp.int32)
        return (layer, cls, 0, 0, 0)

    wk = NA_WIN_BLOCKS * rq
    return pl.pallas_call(
        _na_kernel,
        out_shape=jax.ShapeDtypeStruct((B, S, NA_WIDTH), BF16),
        grid=(n_rb, B),
        in_specs=[pl.BlockSpec((None, rq, NA_WIDTH), lambda rb, b: (b, rb, 0))]
        + [kv_spec(k_col, t) for t in range(NA_WIN_BLOCKS)]
        + [kv_spec(v_col, t) for t in range(NA_WIN_BLOCKS)]
        + [pl.BlockSpec((None, None, NA_HEADS, rq, wk), bias_map)],
        out_specs=pl.BlockSpec((None, rq, NA_WIDTH), lambda rb, b: (b, rb, 0)),
        compiler_params=pltpu.CompilerParams(
            dimension_semantics=("arbitrary", "arbitrary"),
            vmem_limit_bytes=_vmem_limit(40 * 2**20)),
        name="na_attn",
    )(qkv3, qkv3, qkv3, qkv3, qkv3, qkv3, qkv3, bias_table)


def _t5_bucket_steps(seq):
    half = T5_BUCKETS // 2
    max_exact = half // 2
    rel = np.arange(-(seq - 1), seq)
    n = np.abs(rel)
    nf = np.maximum(n, 1).astype(np.float64)
    large = max_exact + np.floor(
        np.log(nf / max_exact) / math.log(T5_MAX_DIST / max_exact) * (half - max_exact) + 1e-9
    ).astype(np.int64)
    large = np.minimum(large, half - 1)
    bucket = np.where(rel > 0, half, 0) + np.where(n < max_exact, n, large)
    steps = [(int(rel[i]), int(bucket[i])) for i in range(1, len(rel)) if bucket[i] != bucket[i - 1]]
    sat = max(abs(steps[0][0]) + 1, abs(steps[-1][0]))
    return int(bucket[0]), steps, sat


def _diff_kernel(t5_ref, q_ref, k_ref, v_ref, lam_ref, g_ref, o_ref, s0_ref, s1_ref, e0_ref,
                 e1_ref, corr_ref, kfar_ref, *, tq, lam_init, bucket_steps):
    b, h = pl.program_id(0), pl.program_id(1)
    seq = k_ref.shape[0]
    band = tq + 2 * BAND_PAD
    n_t = seq // tq
    first_bucket, steps, _ = bucket_steps
    last_bucket = steps[-1][1]
    split = BAND_PAD + tq // 2
    one_lane = 2 * n_t
    assert one_lane + 2 <= LANES

    @pl.when((b == 0) & (h == 0))
    def _init():
        s0_ref[...] = jnp.zeros_like(s0_ref)
        s1_ref[...] = jnp.zeros_like(s1_ref)
        row = lax.broadcasted_iota(jnp.int32, (tq, band), 0)
        col = lax.broadcasted_iota(jnp.int32, (tq, band), 1)
        rel = col - BAND_PAD - row
        for hh in range(DIFF_HEADS):
            val = jnp.full((tq, band), t5_ref[first_bucket, hh], F32)
            for thr, bkt in steps:
                val = jnp.where(rel >= thr, t5_ref[bkt, hh], val)
            far = jnp.where(col < split, t5_ref[first_bucket, hh], t5_ref[last_bucket, hh])
            corr_ref[hh] = val - far
        key = lax.broadcasted_iota(jnp.int32, (seq, LANES), 0)
        lane = lax.broadcasted_iota(jnp.int32, (seq, LANES), 1)
        tile = jnp.where(lane < n_t, lane, lane - n_t)
        step = jnp.where(key >= tile * tq + tq // 2, 1.0, 0.0)
        ones = jnp.where(lane < one_lane + 2, 1.0, 0.0)
        kfar_ref[...] = jnp.where(lane < one_lane, step, ones).astype(BF16)

    scale = 1.0 / math.sqrt(DIFF_QK_DIM)
    lane = lax.broadcasted_iota(jnp.int32, (tq, LANES), 1)
    lo = lane < DIFF_QK_DIM
    c_neg = t5_ref[first_bucket, h]
    c_step = t5_ref[last_bucket, h] - c_neg
    lp = lam_ref[...]
    lam = (jnp.exp(jnp.sum(lp[0:1] * lp[1:2], axis=-1, keepdims=True))
           - jnp.exp(jnp.sum(lp[2:3] * lp[3:4], axis=-1, keepdims=True)) + lam_init)

    def scores(qi, s_ref):
        row0 = _aligned(qi * tq, tq)
        q = q_ref[pl.ds(row0, tq), :] * scale
        zero = jnp.zeros_like(q)
        far = jnp.where((lane == qi) | (lane == n_t + qi), c_step,
                        jnp.where(lane >= one_lane, c_neg, 0.0))
        far = jnp.where(lane < one_lane + 2, far, 0.0)
        far_hi = far.astype(BF16)
        far_lo = (far - far_hi.astype(F32)).astype(BF16)
        q_far = jnp.where((lane < n_t) | (lane == one_lane), far_hi, far_lo)
        lhs = jnp.concatenate(
            [jnp.concatenate([jnp.where(lo, q, zero), q_far], axis=1),
             jnp.concatenate([jnp.where(lo, zero, q), q_far], axis=1)], axis=0)
        rhs = jnp.concatenate([k_ref[...], kfar_ref[...]], axis=1)
        s_ref[:, BAND_PAD:BAND_PAD + seq] = lax.dot_general(
            lhs, rhs, (((1,), (1,)), ((), ())), preferred_element_type=F32)
        corr = corr_ref[h]
        s_ref[0:tq, pl.ds(row0, band)] += corr
        s_ref[tq:2 * tq, pl.ds(row0, band)] += corr

    def numerators(s_ref, e_ref):
        sv = s_ref[:, BAND_PAD:BAND_PAD + seq]
        m = jnp.max(sv, axis=-1, keepdims=True)
        e_ref[...] = jnp.exp((sv - m).astype(BF16))

    def attend(qi, e_ref):
        v = v_ref[...]
        ol = jnp.dot(e_ref[...], jnp.concatenate([v, jnp.ones_like(v)], axis=1),
                     preferred_element_type=F32)
        o = (ol[0:tq, 0:LANES] / ol[0:tq, LANES:2 * LANES]
             - lam * (ol[tq:2 * tq, 0:LANES] / ol[tq:2 * tq, LANES:2 * LANES]))
        row0 = _aligned(qi * tq, tq)
        o_ref[pl.ds(row0, tq), :] = (_rmsnorm_f32(o, g_ref[...]) * (1.0 - lam_init)).astype(BF16)

    assert n_t % 2 == 0 and n_t >= 4
    scores(0, s0_ref)
    scores(1, s1_ref)
    numerators(s0_ref, e0_ref)

    def pair(j, carry):
        t = 2 * j
        scores(t + 2, s0_ref)
        numerators(s1_ref, e1_ref)
        attend(t, e0_ref)
        scores(t + 3, s1_ref)
        numerators(s0_ref, e0_ref)
        attend(t + 1, e1_ref)
        return carry

    lax.fori_loop(0, n_t // 2 - 1, pair, 0)
    numerators(s1_ref, e1_ref)
    attend(n_t - 2, e0_ref)
    attend(n_t - 1, e1_ref)


def _diff_attention(qkv3, t5_bias, lam_params, subln_g, lam_init):
    B, S, _ = qkv3.shape
    tq = TQ_DIFF
    bucket_steps = _t5_bucket_steps(S)
    assert bucket_steps[2] <= BAND_PAD and S % tq == 0 and tq % LANES == 0
    assert 2 * DIFF_QK_DIM == LANES and DIFF_V_DIM == LANES
    q_col = 3 * NA_WIDTH // LANES
    k_col = q_col + DIFF_HEADS
    v_col = k_col + DIFF_HEADS
    band = tq + 2 * BAND_PAD
    return pl.pallas_call(
        functools.partial(_diff_kernel, tq=tq, lam_init=lam_init, bucket_steps=bucket_steps),
        out_shape=jax.ShapeDtypeStruct((B, S, DIFF_WIDTH), BF16),
        grid=(B, DIFF_HEADS),
        in_specs=[
            pl.BlockSpec(memory_space=pltpu.SMEM),
            pl.BlockSpec((None, S, LANES), lambda b, h: (b, 0, q_col + h)),
            pl.BlockSpec((None, S, LANES), lambda b, h: (b, 0, k_col + h)),
            pl.BlockSpec((None, S, LANES), lambda b, h: (b, 0, v_col + h)),
            pl.BlockSpec((4, DIFF_QK_DIM), lambda b, h: (0, 0)),
            pl.BlockSpec((1, DIFF_V_DIM), lambda b, h: (0, 0)),
        ],
        out_specs=pl.BlockSpec((None, S, LANES), lambda b, h: (b, 0, h)),
        scratch_shapes=[
            pltpu.VMEM((2 * tq, S + 2 * BAND_PAD), F32),
            pltpu.VMEM((2 * tq, S + 2 * BAND_PAD), F32),
            pltpu.VMEM((2 * tq, S), BF16),
            pltpu.VMEM((2 * tq, S), BF16),
            pltpu.VMEM((DIFF_HEADS, tq, band), F32),
            pltpu.VMEM((S, LANES), BF16),
        ],
        compiler_params=pltpu.CompilerParams(
            dimension_semantics=("arbitrary", "arbitrary"),
            vmem_limit_bytes=_vmem_limit(48 * 2**20)),
        name="diff_attn",
    )(t5_bias, qkv3, qkv3, qkv3, lam_params, subln_g.reshape(1, DIFF_V_DIM))


def _merge_kernel(x_ref, yna_ref, ydf_ref, gate_ref, wna_ref, wdf_ref, wout_ref, o_ref):
    d = x_ref.shape[1]
    b_na = jnp.dot(yna_ref[...], wna_ref[...], preferred_element_type=F32)
    b_df = jnp.dot(ydf_ref[...], wdf_ref[...], preferred_element_type=F32)
    merged = jax.nn.sigmoid(gate_ref[:, 0:d]) * b_na + jax.nn.sigmoid(gate_ref[:, d:2 * d]) * b_df
    o_ref[...] = x_ref[...] + jnp.dot(merged.astype(BF16), wout_ref[...],
                                      preferred_element_type=F32)


def _merge_out(xt, y_na, y_df, gates, w_na, w_df, w_out, layer):
    T, D = xt.shape
    tm = TM_MERGE
    return pl.pallas_call(
        _merge_kernel,
        out_shape=jax.ShapeDtypeStruct((T, D), F32),
        grid=(T // tm,),
        in_specs=[
            pl.BlockSpec((tm, D), lambda i: (i, 0)),
            pl.BlockSpec((tm, NA_WIDTH), lambda i: (i, 0)),
            pl.BlockSpec((tm, DIFF_WIDTH), lambda i: (i, 0)),
            pl.BlockSpec((tm, 2 * D), lambda i: (i, 0)),
            pl.BlockSpec((None, NA_WIDTH, D), lambda i: (layer, 0, 0)),
            pl.BlockSpec((None, DIFF_WIDTH, D), lambda i: (layer, 0, 0)),
            pl.BlockSpec((None, D, D), lambda i: (layer, 0, 0)),
        ],
        out_specs=pl.BlockSpec((tm, D), lambda i: (i, 0)),
        compiler_params=pltpu.CompilerParams(
            dimension_semantics=("arbitrary",), vmem_limit_bytes=_vmem_limit(40 * 2**20)),
        name="merge_out",
    )(xt, y_na, y_df, gates, w_na, w_df, w_out)


def _mlp_kernel(x_ref, g_ref, w1_ref, w2_ref, fg_ref, o_ref, h_ref, *, final):
    j = pl.program_id(1)

    @pl.when(j == 0)
    def _start():
        x = x_ref[...]
        h_ref[...] = _rmsnorm_f32(x, g_ref[...]).astype(BF16)
        o_ref[...] = x

    u = jnp.dot(h_ref[...], w1_ref[...], preferred_element_type=F32)
    u = jnp.square(jnp.maximum(u, 0.0)).astype(BF16)
    o_ref[...] += jnp.dot(u, w2_ref[...], preferred_element_type=F32)

    if final:
        @pl.when(j == pl.num_programs(1) - 1)
        def _finish():
            o_ref[...] = _rmsnorm_f32(o_ref[...], fg_ref[...])


def _mlp(xt, g, w1, w2, final_g, layer, final):
    T, D = xt.shape
    F = w1.shape[2]
    tm, tf = TM_MLP, TF_MLP
    return pl.pallas_call(
        functools.partial(_mlp_kernel, final=final),
        out_shape=jax.ShapeDtypeStruct((T, D), F32),
        grid=(T // tm, F // tf),
        in_specs=[
            pl.BlockSpec((tm, D), lambda i, j: (i, 0)),
            pl.BlockSpec((1, D), lambda i, j: (0, 0)),
            pl.BlockSpec((None, D, tf), lambda i, j: (layer, 0, j)),
            pl.BlockSpec((None, tf, D), lambda i, j: (layer, j, 0)),
            pl.BlockSpec((1, D), lambda i, j: (0, 0)),
        ],
        out_specs=pl.BlockSpec((tm, D), lambda i, j: (i, 0)),
        scratch_shapes=[pltpu.VMEM((tm, D), BF16)],
        compiler_params=pltpu.CompilerParams(
            dimension_semantics=("arbitrary", "arbitrary"),
            vmem_limit_bytes=_vmem_limit(48 * 2**20)),
        name="mlp",
    )(xt, g.reshape(1, D), w1, w2, final_g.reshape(1, D))


def kernel(x, t5_bias, final_norm_g, norm1_g, w_in, na_rpb, diff_lambda, diff_subln_g, w_na_o,
           w_diff_o, w_out, norm2_g, w_ff1, w_ff2):
    B, S, D = x.shape
    depth = w_in.shape[0]
    T = B * S
    assert w_in.shape[2] == QKV_WIDTH + 2 * D and S % GRID_W == 0
    xt = x.reshape(T, D)
    na_bias = _na_bias_table(na_rpb, S // GRID_W)
    w_in, w_na_o, w_diff_o, w_out, w_ff1, w_ff2 = (
        w.astype(BF16) for w in (w_in, w_na_o, w_diff_o, w_out, w_ff1, w_ff2))
    for layer in range(depth):
        qkv, gates = _in_proj(xt, norm1_g[layer], w_in, layer)
        qkv3 = qkv.reshape(B, S, QKV_WIDTH)
        y_na = _na_attention(qkv3, na_bias, layer)
        lam_init = 0.8 - 0.6 * math.exp(-0.3 * layer)
        y_df = _diff_attention(qkv3, t5_bias, diff_lambda[layer], diff_subln_g[layer], lam_init)
        xt = _merge_out(xt, y_na.reshape(T, NA_WIDTH), y_df.reshape(T, DIFF_WIDTH), gates,
                        w_na_o, w_diff_o, w_out, layer)
        xt = _mlp(xt, norm2_g[layer], w_ff1, w_ff2, final_norm_g, layer,
                  final=(layer == depth - 1))
    return xt.reshape(B, S, D)
```

```python
import functools
import math

import numpy as np
import jax
import jax.numpy as jnp
from jax import lax
from jax.experimental import pallas as pl
from jax.experimental.pallas import tpu as pltpu

GRID_W = 64
NA_HEADS = 8
NA_HEAD_DIM = 64
NA_WIN_ROWS = 8
NA_WIN_COLS = 16
DIFF_HEADS = 4
DIFF_QK_DIM = 64
DIFF_V_DIM = 2 * DIFF_QK_DIM
T5_BUCKETS = 32
T5_MAX_DIST = 128
RMS_EPS = 1e-6
NEG_INF = -1e30

NA_WIDTH = NA_HEADS * NA_HEAD_DIM
DIFF_WIDTH = DIFF_HEADS * DIFF_V_DIM
QKV_WIDTH = 3 * NA_WIDTH + 3 * DIFF_WIDTH

LANES = 128
V7X_VMEM_BYTES = 64 * 2**20

TM_PROJ = 512
TM_MERGE = 512
TM_MLP = 1024
TF_MLP = 1024
NA_ROW_BLOCK = 4
NA_WIN_BLOCKS = 3
TQ_DIFF = 256

BF16 = jnp.bfloat16
F32 = jnp.float32


def _vmem_limit(nbytes):
    assert nbytes < V7X_VMEM_BYTES
    return int(nbytes)


def _aligned(index, multiple):
    return index if isinstance(index, int) else pl.multiple_of(index, multiple)


def _rmsnorm_f32(x, g):
    return (x * lax.rsqrt(jnp.mean(x * x, axis=-1, keepdims=True) + RMS_EPS)) * g


def _in_proj_kernel(x_ref, g_ref, w_ref, qkv_ref, gate_ref, *, chunk):
    hb = _rmsnorm_f32(x_ref[...], g_ref[...]).astype(BF16)
    n_qkv = qkv_ref.shape[1]
    n_gate = gate_ref.shape[1]
    for c in range(n_qkv // chunk):
        cs = slice(c * chunk, (c + 1) * chunk)
        qkv_ref[:, cs] = jnp.dot(hb, w_ref[:, cs], preferred_element_type=F32).astype(BF16)
    for c in range(n_gate // chunk):
        cs = slice(c * chunk, (c + 1) * chunk)
        ws = slice(n_qkv + c * chunk, n_qkv + (c + 1) * chunk)
        gate_ref[:, cs] = jnp.dot(hb, w_ref[:, ws], preferred_element_type=F32)


def _in_proj(xt, g, w, layer):
    T, D = xt.shape
    n_all = w.shape[2]
    n_gate = n_all - QKV_WIDTH
    tm = TM_PROJ
    return pl.pallas_call(
        functools.partial(_in_proj_kernel, chunk=1024),
        out_shape=(jax.ShapeDtypeStruct((T, QKV_WIDTH), BF16),
                   jax.ShapeDtypeStruct((T, n_gate), F32)),
        grid=(T // tm,),
        in_specs=[
            pl.BlockSpec((tm, D), lambda i: (i, 0)),
            pl.BlockSpec((1, D), lambda i: (0, 0)),
            pl.BlockSpec((None, D, n_all), lambda i: (layer, 0, 0)),
        ],
        out_specs=(pl.BlockSpec((tm, QKV_WIDTH), lambda i: (i, 0)),
                   pl.BlockSpec((tm, n_gate), lambda i: (i, 0))),
        compiler_params=pltpu.CompilerParams(
            dimension_semantics=("arbitrary",), vmem_limit_bytes=_vmem_limit(56 * 2**20)),
        name="in_proj",
    )(xt, g.reshape(1, D), w)


def _na_geometry(rows):
    rb_rows = NA_ROW_BLOCK
    n_rb = rows // rb_rows
    win_rows = NA_WIN_BLOCKS * rb_rows
    wr = min(NA_WIN_ROWS, rows)
    kb = np.clip(np.arange(n_rb) - 1, 0, n_rb - NA_WIN_BLOCKS)
    dr_idx = np.zeros((n_rb, rb_rows, win_rows), np.int32)
    valid = np.zeros((n_rb, rb_rows, win_rows), bool)
    for rb in range(n_rb):
        w0 = kb[rb] * rb_rows
        for ri in range(rb_rows):
            r = rb * rb_rows + ri
            r0 = min(max(r - wr // 2, 0), rows - wr)
            assert w0 <= r0 and r0 + wr <= w0 + win_rows
            for wj in range(win_rows):
                krow = w0 + wj
                valid[rb, ri, wj] = r0 <= krow < r0 + wr
                dr_idx[rb, ri, wj] = min(max(krow - r + NA_WIN_ROWS - 1, 0), 2 * NA_WIN_ROWS - 2)
    for rb in range(2, n_rb - 1):
        assert (valid[rb] == valid[1]).all() and (dr_idx[rb] == dr_idx[1]).all()
    classes = [0, 1, n_rb - 1]
    return n_rb, kb, dr_idx[classes], valid[classes]


def _na_bias_kernel(rpb_ref, o_ref, *, dr_idx, row_valid):
    assert LANES == 2 * GRID_W
    n_cls, rb_rows, win_rows = dr_idx.shape
    c = lax.broadcasted_iota(jnp.int32, (GRID_W, LANES), 0)
    lane = lax.broadcasted_iota(jnp.int32, (GRID_W, LANES), 1)
    left = lane < GRID_W
    kc = jnp.where(left, lane, lane - GRID_W)
    c0 = jnp.clip(c - NA_WIN_COLS // 2, 0, GRID_W - NA_WIN_COLS)
    col_ok = (kc >= c0) & (kc < c0 + NA_WIN_COLS)
    neg = jnp.full((GRID_W, LANES), NEG_INF, F32)

    def half_tile(d, right):
        w = jnp.broadcast_to(rpb_ref[d:d + 1, :], (GRID_W, LANES))
        t = pltpu.roll(w, GRID_W if right else 0, 1, stride=1, stride_axis=0)
        return jnp.where(col_ok, t, neg)

    for cls in range(n_cls):
        for ri in range(rb_rows):
            for p in range(win_rows // 2):
                halves = []
                for side in range(2):
                    wj = 2 * p + side
                    ok = bool(row_valid[cls, ri, wj])
                    halves.append(half_tile(int(dr_idx[cls, ri, wj]), side == 1) if ok else neg)
                o_ref[cls, ri * GRID_W:(ri + 1) * GRID_W, p * LANES:(p + 1) * LANES] = (
                    jnp.where(left, halves[0], halves[1]))


def _na_bias_table(na_rpb, rows):
    depth, H, n_dr, n_dc = na_rpb.shape
    assert n_dr == 2 * NA_WIN_ROWS - 1 and n_dc == 2 * NA_WIN_COLS - 1
    _, _, dr_idx, row_valid = _na_geometry(rows)
    n_cls, rb_rows, win_rows = dr_idx.shape
    rq, wk = rb_rows * GRID_W, win_rows * GRID_W
    lanes = jnp.pad(na_rpb, ((0, 0), (0, 0), (0, 0), (0, LANES - n_dc)), constant_values=NEG_INF)
    lanes = jnp.roll(lanes, -(NA_WIN_COLS - 1), axis=-1)
    return pl.pallas_call(
        functools.partial(_na_bias_kernel, dr_idx=dr_idx, row_valid=row_valid),
        out_shape=jax.ShapeDtypeStruct((depth, n_cls, H, rq, wk), F32),
        grid=(depth, H),
        in_specs=[pl.BlockSpec((None, None, n_dr, LANES), lambda l, h: (l, h, 0, 0))],
        out_specs=pl.BlockSpec((None, n_cls, None, rq, wk), lambda l, h: (l, 0, h, 0, 0)),
        compiler_params=pltpu.CompilerParams(dimension_semantics=("arbitrary", "arbitrary")),
        name="na_bias",
    )(lanes)


def _na_kernel(q_ref, k0_ref, k1_ref, k2_ref, v0_ref, v1_ref, v2_ref, bias_ref, o_ref):
    rq = q_ref.shape[0]
    scale = 1.0 / math.sqrt(NA_HEAD_DIM)
    lo = lax.broadcasted_iota(jnp.int32, (rq, LANES), 1) < NA_HEAD_DIM
    heads_per_vreg = LANES // NA_HEAD_DIM
    for hp in range(NA_HEADS // heads_per_vreg):
        cs = slice(hp * LANES, (hp + 1) * LANES)
        q = q_ref[:, cs] * scale
        k = jnp.concatenate([k0_ref[:, cs], k1_ref[:, cs], k2_ref[:, cs]], axis=0)
        v = jnp.concatenate([v0_ref[:, cs], v1_ref[:, cs], v2_ref[:, cs]], axis=0)
        v_ones = jnp.concatenate([v, jnp.ones_like(v)], axis=1)
        outs = []
        for e in range(heads_per_vreg):
            qm = jnp.where(lo if e == 0 else jnp.logical_not(lo), q, jnp.zeros_like(q))
            s = lax.dot_general(qm, k, (((1,), (1,)), ((), ())), preferred_element_type=F32)
            s = s + bias_ref[heads_per_vreg * hp + e]
            m = jnp.max(s, axis=-1, keepdims=True)
            p = jnp.exp((s - m).astype(BF16))
            ol = jnp.dot(p, v_ones, preferred_element_type=F32)
            outs.append(ol[:, 0:LANES] / ol[:, LANES:2 * LANES])
        o_ref[:, cs] = jnp.where(lo, outs[0], outs[1]).astype(BF16)


def _na_attention(qkv3, bias_table, layer):
    B, S, _ = qkv3.shape
    rows = S // GRID_W
    n_rb, kb, _, _ = _na_geometry(rows)
    rq = NA_ROW_BLOCK * GRID_W
    assert NA_HEADS * NA_HEAD_DIM == NA_WIDTH and LANES % NA_HEAD_DIM == 0
    k_col, v_col = 1, 2
    n_kb = n_rb - NA_WIN_BLOCKS

    def kv_spec(col, t):
        return pl.BlockSpec((None, rq, NA_WIDTH),
                            lambda rb, b: (b, jnp.clip(rb - 1, 0, n_kb) + t, col))

    def bias_map(rb, b):
        cls = (rb > 0).astype(jnp.int32) + (rb == n_rb - 1).astype(jnp.int32)
        return (layer, cls, 0, 0, 0)

    wk = NA_WIN_BLOCKS * rq
    return pl.pallas_call(
        _na_kernel,
        out_shape=jax.ShapeDtypeStruct((B, S, NA_WIDTH), BF16),
        grid=(n_rb, B),
        in_specs=[pl.BlockSpec((None, rq, NA_WIDTH), lambda rb, b: (b, rb, 0))]
        + [kv_spec(k_col, t) for t in range(NA_WIN_BLOCKS)]
        + [kv_spec(v_col, t) for t in range(NA_WIN_BLOCKS)]
        + [pl.BlockSpec((None, None, NA_HEADS, rq, wk), bias_map)],
        out_specs=pl.BlockSpec((None, rq, NA_WIDTH), lambda rb, b: (b, rb, 0)),
        compiler_params=pltpu.CompilerParams(
            dimension_semantics=("arbitrary", "arbitrary"),
            vmem_limit_bytes=_vmem_limit(40 * 2**20)),
        name="na_attn",
    )(qkv3, qkv3, qkv3, qkv3, qkv3, qkv3, qkv3, bias_table)


def _t5_bucket_steps(seq):
    half = T5_BUCKETS // 2
    max_exact = half // 2
    rel = np.arange(-(seq - 1), seq)
    n = np.abs(rel)
    nf = np.maximum(n, 1).astype(np.float64)
    large = max_exact + np.floor(
        np.log(nf / max_exact) / math.log(T5_MAX_DIST / max_exact) * (half - max_exact) + 1e-9
    ).astype(np.int64)
    large = np.minimum(large, half - 1)
    bucket = np.where(rel > 0, half, 0) + np.where(n < max_exact, n, large)
    steps = [(int(rel[i]), int(bucket[i])) for i in range(1, len(rel)) if bucket[i] != bucket[i - 1]]
    sat = max(abs(steps[0][0]) + 1, abs(steps[-1][0]))
    return int(bucket[0]), steps, sat


def _diff_kernel(t5_ref, q_ref, k_ref, v_ref, lam_ref, g_ref, o_ref, s0_ref, s1_ref, m0_ref,
                 m1_ref, corr_ref, kfar_ref, *, tq, lam_init, bucket_steps):
    b, h = pl.program_id(0), pl.program_id(1)
    seq = k_ref.shape[0]
    n_t = seq // tq
    band = 3 * tq
    first_bucket, steps, _ = bucket_steps
    last_bucket = steps[-1][1]
    split = tq + tq // 2
    one_lane = 2 * n_t
    assert one_lane + 2 <= LANES and n_t >= 4 and n_t % 2 == 0

    @pl.when((b == 0) & (h == 0))
    def _init():
        row = lax.broadcasted_iota(jnp.int32, (tq, band), 0)
        col = lax.broadcasted_iota(jnp.int32, (tq, band), 1)
        rel = col - tq - row
        for hh in range(DIFF_HEADS):
            val = jnp.full((tq, band), t5_ref[first_bucket, hh], F32)
            for thr, bkt in steps:
                val = jnp.where(rel >= thr, t5_ref[bkt, hh], val)
            far = jnp.where(col < split, t5_ref[first_bucket, hh], t5_ref[last_bucket, hh])
            corr_ref[hh] = val - far
        corr_ref[DIFF_HEADS] = jnp.zeros((tq, band), F32)
        key = lax.broadcasted_iota(jnp.int32, (seq, LANES), 0)
        lane = lax.broadcasted_iota(jnp.int32, (seq, LANES), 1)
        tile = jnp.where(lane < n_t, lane, lane - n_t)
        step = jnp.where(key >= tile * tq + tq // 2, 1.0, 0.0)
        ones = jnp.where(lane < one_lane + 2, 1.0, 0.0)
        kfar_ref[...] = jnp.where(lane < one_lane, step, ones).astype(BF16)

    scale = 1.0 / math.sqrt(DIFF_QK_DIM)
    lane = lax.broadcasted_iota(jnp.int32, (tq, LANES), 1)
    lo = lane < DIFF_QK_DIM
    c_neg = t5_ref[first_bucket, h]
    c_step = t5_ref[last_bucket, h] - c_neg
    lp = lam_ref[...]
    lam = (jnp.exp(jnp.sum(lp[0:1] * lp[1:2], axis=-1, keepdims=True))
           - jnp.exp(jnp.sum(lp[2:3] * lp[3:4], axis=-1, keepdims=True)) + lam_init)

    chunk_order = [-1, 0, 1] + list(range(2, n_t - 1))

    def key_rows(qi, d):
        chunk = (qi + d) % n_t if isinstance(qi, int) else lax.rem(qi + (d + n_t), n_t)
        return pl.ds(_aligned(chunk * tq, tq), tq)

    def scores(qi, s_ref, m_ref):
        row0 = _aligned(qi * tq, tq)
        q = q_ref[pl.ds(row0, tq), :] * scale
        zero = jnp.zeros_like(q)
        far = jnp.where((lane == qi) | (lane == n_t + qi), c_step,
                        jnp.where(lane >= one_lane, c_neg, 0.0))
        far = jnp.where(lane < one_lane + 2, far, 0.0)
        far_hi = far.astype(BF16)
        far_lo = (far - far_hi.astype(F32)).astype(BF16)
        q_far = jnp.where((lane < n_t) | (lane == one_lane), far_hi, far_lo)
        lhs = jnp.concatenate(
            [jnp.concatenate([jnp.where(lo, q, zero), q_far], axis=1),
             jnp.concatenate([jnp.where(lo, zero, q), q_far], axis=1)], axis=0)
        m_run = None
        for pos, d in enumerate(chunk_order):
            rows = key_rows(qi, d)
            rhs = jnp.concatenate([k_ref[rows, :], kfar_ref[rows, :]], axis=1)
            s = lax.dot_general(lhs, rhs, (((1,), (1,)), ((), ())), preferred_element_type=F32)
            if d in (-1, 0, 1):
                inside = (qi + d >= 0) & (qi + d < n_t)
                head = jnp.where(inside, h, DIFF_HEADS)
                corr = corr_ref[head, :, (d + 1) * tq:(d + 2) * tq]
                s = s + jnp.concatenate([corr, corr], axis=0)
            s_ref[:, pos * tq:(pos + 1) * tq] = s
            m_run = s if m_run is None else jnp.maximum(m_run, s)
        m_ref[...] = jnp.broadcast_to(jnp.max(m_run, axis=-1, keepdims=True), (2 * tq, LANES))

    def attend(qi, s_ref, m_ref):
        m = jnp.concatenate([m_ref[...]] * (tq // LANES), axis=1)
        ol = None
        for pos, d in enumerate(chunk_order):
            e = jnp.exp((s_ref[:, pos * tq:(pos + 1) * tq] - m).astype(BF16))
            v = v_ref[key_rows(qi, d), :]
            part = jnp.dot(e, jnp.concatenate([v, jnp.ones_like(v)], axis=1),
                           preferred_element_type=F32)
            ol = part if ol is None else ol + part
        o = (ol[0:tq, 0:LANES] / ol[0:tq, LANES:2 * LANES]
             - lam * (ol[tq:2 * tq, 0:LANES] / ol[tq:2 * tq, LANES:2 * LANES]))
        row0 = _aligned(qi * tq, tq)
        o_ref[pl.ds(row0, tq), :] = (_rmsnorm_f32(o, g_ref[...]) * (1.0 - lam_init)).astype(BF16)

    scores(0, s0_ref, m0_ref)

    def pair(j, carry):
        t = 2 * j
        scores(t + 1, s1_ref, m1_ref)
        attend(t, s0_ref, m0_ref)
        scores(t + 2, s0_ref, m0_ref)
        attend(t + 1, s1_ref, m1_ref)
        return carry

    lax.fori_loop(0, n_t // 2 - 1, pair, 0)
    scores(n_t - 1, s1_ref, m1_ref)
    attend(n_t - 2, s0_ref, m0_ref)
    attend(n_t - 1, s1_ref, m1_ref)


def _diff_attention(qkv3, t5_bias, lam_params, subln_g, lam_init):
    B, S, _ = qkv3.shape
    tq = TQ_DIFF
    bucket_steps = _t5_bucket_steps(S)
    assert bucket_steps[2] <= tq and S % tq == 0 and tq % LANES == 0
    assert 2 * DIFF_QK_DIM == LANES and DIFF_V_DIM == LANES
    q_col = 3 * NA_WIDTH // LANES
    k_col = q_col + DIFF_HEADS
    v_col = k_col + DIFF_HEADS
    return pl.pallas_call(
        functools.partial(_diff_kernel, tq=tq, lam_init=lam_init, bucket_steps=bucket_steps),
        out_shape=jax.ShapeDtypeStruct((B, S, DIFF_WIDTH), BF16),
        grid=(B, DIFF_HEADS),
        in_specs=[
            pl.BlockSpec(memory_space=pltpu.SMEM),
            pl.BlockSpec((None, S, LANES), lambda b, h: (b, 0, q_col + h)),
            pl.BlockSpec((None, S, LANES), lambda b, h: (b, 0, k_col + h)),
            pl.BlockSpec((None, S, LANES), lambda b, h: (b, 0, v_col + h)),
            pl.BlockSpec((4, DIFF_QK_DIM), lambda b, h: (0, 0)),
            pl.BlockSpec((1, DIFF_V_DIM), lambda b, h: (0, 0)),
        ],
        out_specs=pl.BlockSpec((None, S, LANES), lambda b, h: (b, 0, h)),
        scratch_shapes=[
            pltpu.VMEM((2 * tq, S), F32),
            pltpu.VMEM((2 * tq, S), F32),
            pltpu.VMEM((2 * tq, LANES), F32),
            pltpu.VMEM((2 * tq, LANES), F32),
            pltpu.VMEM((DIFF_HEADS + 1, tq, 3 * tq), F32),
            pltpu.VMEM((S, LANES), BF16),
        ],
        compiler_params=pltpu.CompilerParams(
            dimension_semantics=("arbitrary", "arbitrary"),
            vmem_limit_bytes=_vmem_limit(48 * 2**20)),
        name="diff_attn",
    )(t5_bias, qkv3, qkv3, qkv3, lam_params, subln_g.reshape(1, DIFF_V_DIM))


def _merge_kernel(x_ref, yna_ref, ydf_ref, gate_ref, wna_ref, wdf_ref, wout_ref, o_ref):
    d = x_ref.shape[1]
    b_na = jnp.dot(yna_ref[...], wna_ref[...], preferred_element_type=F32)
    b_df = jnp.dot(ydf_ref[...], wdf_ref[...], preferred_element_type=F32)
    merged = jax.nn.sigmoid(gate_ref[:, 0:d]) * b_na + jax.nn.sigmoid(gate_ref[:, d:2 * d]) * b_df
    o_ref[...] = x_ref[...] + jnp.dot(merged.astype(BF16), wout_ref[...],
                                      preferred_element_type=F32)


def _merge_out(xt, y_na, y_df, gates, w_na, w_df, w_out, layer):
    T, D = xt.shape
    tm = TM_MERGE
    return pl.pallas_call(
        _merge_kernel,
        out_shape=jax.ShapeDtypeStruct((T, D), F32),
        grid=(T // tm,),
        in_specs=[
            pl.BlockSpec((tm, D), lambda i: (i, 0)),
            pl.BlockSpec((tm, NA_WIDTH), lambda i: (i, 0)),
            pl.BlockSpec((tm, DIFF_WIDTH), lambda i: (i, 0)),
            pl.BlockSpec((tm, 2 * D), lambda i: (i, 0)),
            pl.BlockSpec((None, NA_WIDTH, D), lambda i: (layer, 0, 0)),
            pl.BlockSpec((None, DIFF_WIDTH, D), lambda i: (layer, 0, 0)),
            pl.BlockSpec((None, D, D), lambda i: (layer, 0, 0)),
        ],
        out_specs=pl.BlockSpec((tm, D), lambda i: (i, 0)),
        compiler_params=pltpu.CompilerParams(
            dimension_semantics=("arbitrary",), vmem_limit_bytes=_vmem_limit(40 * 2**20)),
        name="merge_out",
    )(xt, y_na, y_df, gates, w_na, w_df, w_out)


def _mlp_kernel(x_ref, g_ref, w1_ref, w2_ref, fg_ref, o_ref, h_ref, *, final):
    j = pl.program_id(1)

    @pl.when(j == 0)
    def _start():
        x = x_ref[...]
        h_ref[...] = _rmsnorm_f32(x, g_ref[...]).astype(BF16)
        o_ref[...] = x

    u = jnp.dot(h_ref[...], w1_ref[...], preferred_element_type=F32)
    u = jnp.square(jnp.maximum(u, 0.0)).astype(BF16)
    o_ref[...] += jnp.dot(u, w2_ref[...], preferred_element_type=F32)

    if final:
        @pl.when(j == pl.num_programs(1) - 1)
        def _finish():
            o_ref[...] = _rmsnorm_f32(o_ref[...], fg_ref[...])


def _mlp(xt, g, w1, w2, final_g, layer, final):
    T, D = xt.shape
    F = w1.shape[2]
    tm, tf = TM_MLP, TF_MLP
    return pl.pallas_call(
        functools.partial(_mlp_kernel, final=final),
        out_shape=jax.ShapeDtypeStruct((T, D), F32),
        grid=(T // tm, F // tf),
        in_specs=[
            pl.BlockSpec((tm, D), lambda i, j: (i, 0)),
            pl.BlockSpec((1, D), lambda i, j: (0, 0)),
            pl.BlockSpec((None, D, tf), lambda i, j: (layer, 0, j)),
            pl.BlockSpec((None, tf, D), lambda i, j: (layer, j, 0)),
            pl.BlockSpec((1, D), lambda i, j: (0, 0)),
        ],
        out_specs=pl.BlockSpec((tm, D), lambda i, j: (i, 0)),
        scratch_shapes=[pltpu.VMEM((tm, D), BF16)],
        compiler_params=pltpu.CompilerParams(
            dimension_semantics=("arbitrary", "arbitrary"),
            vmem_limit_bytes=_vmem_limit(48 * 2**20)),
        name="mlp",
    )(xt, g.reshape(1, D), w1, w2, final_g.reshape(1, D))


def kernel(x, t5_bias, final_norm_g, norm1_g, w_in, na_rpb, diff_lambda, diff_subln_g, w_na_o,
           w_diff_o, w_out, norm2_g, w_ff1, w_ff2):
    B, S, D = x.shape
    depth = w_in.shape[0]
    T = B * S
    assert w_in.shape[2] == QKV_WIDTH + 2 * D and S % GRID_W == 0
    xt = x.reshape(T, D)
    na_bias = _na_bias_table(na_rpb, S // GRID_W)
    w_in, w_na_o, w_diff_o, w_out, w_ff1, w_ff2 = (
        w.astype(BF16) for w in (w_in, w_na_o, w_diff_o, w_out, w_ff1, w_ff2))
    for layer in range(depth):
        qkv, gates = _in_proj(xt, norm1_g[layer], w_in, layer)
        qkv3 = qkv.reshape(B, S, QKV_WIDTH)
        y_na = _na_attention(qkv3, na_bias, layer)
        lam_init = 0.8 - 0.6 * math.exp(-0.3 * layer)
        y_df = _diff_attention(qkv3, t5_bias, diff_lambda[layer], diff_subln_g[layer], lam_init)
        xt = _merge_out(xt, y_na.reshape(T, NA_WIDTH), y_df.reshape(T, DIFF_WIDTH), gates,
                        w_na_o, w_diff_o, w_out, layer)
        xt = _mlp(xt, norm2_g[layer], w_ff1, w_ff2, final_norm_g, layer,
                  final=(layer == depth - 1))
    return xt.reshape(B, S, D)
```

```python
import functools
import math

import numpy as np
import jax
import jax.numpy as jnp
from jax import lax
from jax.experimental import pallas as pl
from jax.experimental.pallas import tpu as pltpu

GRID_W = 64
NA_HEADS = 8
NA_HEAD_DIM = 64
NA_WIN_ROWS = 8
NA_WIN_COLS = 16
DIFF_HEADS = 4
DIFF_QK_DIM = 64
DIFF_V_DIM = 2 * DIFF_QK_DIM
T5_BUCKETS = 32
T5_MAX_DIST = 128
RMS_EPS = 1e-6
NEG_INF = -1e30

NA_WIDTH = NA_HEADS * NA_HEAD_DIM
DIFF_WIDTH = DIFF_HEADS * DIFF_V_DIM
QKV_WIDTH = 3 * NA_WIDTH + 3 * DIFF_WIDTH

LANES = 128
V7X_VMEM_BYTES = 64 * 2**20

TM_PROJ = 512
TM_MERGE = 512
TM_MLP = 1024
TF_MLP = 1024
NA_ROW_BLOCK = 4
NA_WIN_BLOCKS = 3
TQ_DIFF = 256

BF16 = jnp.bfloat16
F32 = jnp.float32


def _vmem_limit(nbytes):
    assert nbytes < V7X_VMEM_BYTES
    return int(nbytes)


def _aligned(index, multiple):
    return index if isinstance(index, int) else pl.multiple_of(index, multiple)


def _rmsnorm_f32(x, g):
    return (x * lax.rsqrt(jnp.mean(x * x, axis=-1, keepdims=True) + RMS_EPS)) * g


def _in_proj_kernel(x_ref, g_ref, w_ref, qkv_ref, gate_ref, *, chunk):
    hb = _rmsnorm_f32(x_ref[...], g_ref[...]).astype(BF16)
    n_qkv = qkv_ref.shape[1]
    n_gate = gate_ref.shape[1]
    for c in range(n_qkv // chunk):
        cs = slice(c * chunk, (c + 1) * chunk)
        qkv_ref[:, cs] = jnp.dot(hb, w_ref[:, cs], preferred_element_type=F32).astype(BF16)
    for c in range(n_gate // chunk):
        cs = slice(c * chunk, (c + 1) * chunk)
        ws = slice(n_qkv + c * chunk, n_qkv + (c + 1) * chunk)
        gate_ref[:, cs] = jnp.dot(hb, w_ref[:, ws], preferred_element_type=F32)


def _in_proj(xt, g, w, layer):
    T, D = xt.shape
    n_all = w.shape[2]
    n_gate = n_all - QKV_WIDTH
    tm = TM_PROJ
    return pl.pallas_call(
        functools.partial(_in_proj_kernel, chunk=1024),
        out_shape=(jax.ShapeDtypeStruct((T, QKV_WIDTH), BF16),
                   jax.ShapeDtypeStruct((T, n_gate), F32)),
        grid=(T // tm,),
        in_specs=[
            pl.BlockSpec((tm, D), lambda i: (i, 0)),
            pl.BlockSpec((1, D), lambda i: (0, 0)),
            pl.BlockSpec((None, D, n_all), lambda i: (layer, 0, 0)),
        ],
        out_specs=(pl.BlockSpec((tm, QKV_WIDTH), lambda i: (i, 0)),
                   pl.BlockSpec((tm, n_gate), lambda i: (i, 0))),
        compiler_params=pltpu.CompilerParams(
            dimension_semantics=("arbitrary",), vmem_limit_bytes=_vmem_limit(56 * 2**20)),
        name="in_proj",
    )(xt, g.reshape(1, D), w)


def _na_geometry(rows):
    rb_rows = NA_ROW_BLOCK
    n_rb = rows // rb_rows
    win_rows = NA_WIN_BLOCKS * rb_rows
    wr = min(NA_WIN_ROWS, rows)
    kb = np.clip(np.arange(n_rb) - 1, 0, n_rb - NA_WIN_BLOCKS)
    dr_idx = np.zeros((n_rb, rb_rows, win_rows), np.int32)
    valid = np.zeros((n_rb, rb_rows, win_rows), bool)
    for rb in range(n_rb):
        w0 = kb[rb] * rb_rows
        for ri in range(rb_rows):
            r = rb * rb_rows + ri
            r0 = min(max(r - wr // 2, 0), rows - wr)
            assert w0 <= r0 and r0 + wr <= w0 + win_rows
            for wj in range(win_rows):
                krow = w0 + wj
                valid[rb, ri, wj] = r0 <= krow < r0 + wr
                dr_idx[rb, ri, wj] = min(max(krow - r + NA_WIN_ROWS - 1, 0), 2 * NA_WIN_ROWS - 2)
    for rb in range(2, n_rb - 1):
        assert (valid[rb] == valid[1]).all() and (dr_idx[rb] == dr_idx[1]).all()
    classes = [0, 1, n_rb - 1]
    return n_rb, kb, dr_idx[classes], valid[classes]


def _na_bias_kernel(rpb_ref, o_ref, *, dr_idx, row_valid):
    assert LANES == 2 * GRID_W
    n_cls, rb_rows, win_rows = dr_idx.shape
    c = lax.broadcasted_iota(jnp.int32, (GRID_W, LANES), 0)
    lane = lax.broadcasted_iota(jnp.int32, (GRID_W, LANES), 1)
    left = lane < GRID_W
    kc = jnp.where(left, lane, lane - GRID_W)
    c0 = jnp.clip(c - NA_WIN_COLS // 2, 0, GRID_W - NA_WIN_COLS)
    col_ok = (kc >= c0) & (kc < c0 + NA_WIN_COLS)
    neg = jnp.full((GRID_W, LANES), NEG_INF, F32)

    def half_tile(d, right):
        w = jnp.broadcast_to(rpb_ref[d:d + 1, :], (GRID_W, LANES))
        t = pltpu.roll(w, GRID_W if right else 0, 1, stride=1, stride_axis=0)
        return jnp.where(col_ok, t, neg)

    for cls in range(n_cls):
        for ri in range(rb_rows):
            for p in range(win_rows // 2):
                halves = []
                for side in range(2):
                    wj = 2 * p + side
                    ok = bool(row_valid[cls, ri, wj])
                    halves.append(half_tile(int(dr_idx[cls, ri, wj]), side == 1) if ok else neg)
                o_ref[cls, ri * GRID_W:(ri + 1) * GRID_W, p * LANES:(p + 1) * LANES] = (
                    jnp.where(left, halves[0], halves[1]))


def _na_bias_table(na_rpb, rows):
    depth, H, n_dr, n_dc = na_rpb.shape
    assert n_dr == 2 * NA_WIN_ROWS - 1 and n_dc == 2 * NA_WIN_COLS - 1
    _, _, dr_idx, row_valid = _na_geometry(rows)
    n_cls, rb_rows, win_rows = dr_idx.shape
    rq, wk = rb_rows * GRID_W, win_rows * GRID_W
    lanes = jnp.pad(na_rpb, ((0, 0), (0, 0), (0, 0), (0, LANES - n_dc)), constant_values=NEG_INF)
    lanes = jnp.roll(lanes, -(NA_WIN_COLS - 1), axis=-1)
    return pl.pallas_call(
        functools.partial(_na_bias_kernel, dr_idx=dr_idx, row_valid=row_valid),
        out_shape=jax.ShapeDtypeStruct((depth, n_cls, H, rq, wk), F32),
        grid=(depth, H),
        in_specs=[pl.BlockSpec((None, None, n_dr, LANES), lambda l, h: (l, h, 0, 0))],
        out_specs=pl.BlockSpec((None, n_cls, None, rq, wk), lambda l, h: (l, 0, h, 0, 0)),
        compiler_params=pltpu.CompilerParams(dimension_semantics=("arbitrary", "arbitrary")),
        name="na_bias",
    )(lanes)


def _na_kernel(q_ref, k0_ref, k1_ref, k2_ref, v0_ref, v1_ref, v2_ref, bias_ref, o_ref):
    rq = q_ref.shape[0]
    scale = 1.0 / math.sqrt(NA_HEAD_DIM)
    lo = lax.broadcasted_iota(jnp.int32, (rq, LANES), 1) < NA_HEAD_DIM
    heads_per_vreg = LANES // NA_HEAD_DIM
    for hp in range(NA_HEADS // heads_per_vreg):
        cs = slice(hp * LANES, (hp + 1) * LANES)
        q = q_ref[:, cs] * scale
        k = jnp.concatenate([k0_ref[:, cs], k1_ref[:, cs], k2_ref[:, cs]], axis=0)
        v = jnp.concatenate([v0_ref[:, cs], v1_ref[:, cs], v2_ref[:, cs]], axis=0)
        v_ones = jnp.concatenate([v, jnp.ones_like(v)], axis=1)
        outs = []
        for e in range(heads_per_vreg):
            qm = jnp.where(lo if e == 0 else jnp.logical_not(lo), q, jnp.zeros_like(q))
            s = lax.dot_general(qm, k, (((1,), (1,)), ((), ())), preferred_element_type=F32)
            s = s + bias_ref[heads_per_vreg * hp + e]
            m = jnp.max(s, axis=-1, keepdims=True)
            p = jnp.exp((s - m).astype(BF16))
            ol = jnp.dot(p, v_ones, preferred_element_type=F32)
            outs.append(ol[:, 0:LANES] / ol[:, LANES:2 * LANES])
        o_ref[:, cs] = jnp.where(lo, outs[0], outs[1]).astype(BF16)


def _na_attention(qkv3, bias_table, layer):
    B, S, _ = qkv3.shape
    rows = S // GRID_W
    n_rb, kb, _, _ = _na_geometry(rows)
    rq = NA_ROW_BLOCK * GRID_W
    assert NA_HEADS * NA_HEAD_DIM == NA_WIDTH and LANES % NA_HEAD_DIM == 0
    k_col, v_col = 1, 2
    n_kb = n_rb - NA_WIN_BLOCKS

    def kv_spec(col, t):
        return pl.BlockSpec((None, rq, NA_WIDTH),
                            lambda rb, b: (b, jnp.clip(rb - 1, 0, n_kb) + t, col))

    def bias_map(rb, b):
        cls = (rb > 0).astype(jnp.int32) + (rb == n_rb - 1).astype(jnp.int32)
        return (layer, cls, 0, 0, 0)

    wk = NA_WIN_BLOCKS * rq
    return pl.pallas_call(
        _na_kernel,
        out_shape=jax.ShapeDtypeStruct((B, S, NA_WIDTH), BF16),
        grid=(n_rb, B),
        in_specs=[pl.BlockSpec((None, rq, NA_WIDTH), lambda rb, b: (b, rb, 0))]
        + [kv_spec(k_col, t) for t in range(NA_WIN_BLOCKS)]
        + [kv_spec(v_col, t) for t in range(NA_WIN_BLOCKS)]
        + [pl.BlockSpec((None, None, NA_HEADS, rq, wk), bias_map)],
        out_specs=pl.BlockSpec((None, rq, NA_WIDTH), lambda rb, b: (b, rb, 0)),
        compiler_params=pltpu.CompilerParams(
            dimension_semantics=("arbitrary", "arbitrary"),
            vmem_limit_bytes=_vmem_limit(40 * 2**20)),
        name="na_attn",
    )(qkv3, qkv3, qkv3, qkv3, qkv3, qkv3, qkv3, bias_table)


def _t5_bucket_steps(seq):
    half = T5_BUCKETS // 2
    max_exact = half // 2
    rel = np.arange(-(seq - 1), seq)
    n = np.abs(rel)
    nf = np.maximum(n, 1).astype(np.float64)
    large = max_exact + np.floor(
        np.log(nf / max_exact) / math.log(T5_MAX_DIST / max_exact) * (half - max_exact) + 1e-9
    ).astype(np.int64)
    large = np.minimum(large, half - 1)
    bucket = np.where(rel > 0, half, 0) + np.where(n < max_exact, n, large)
    steps = [(int(rel[i]), int(bucket[i])) for i in range(1, len(rel)) if bucket[i] != bucket[i - 1]]
    sat = max(abs(steps[0][0]) + 1, abs(steps[-1][0]))
    return int(bucket[0]), steps, sat


def _diff_kernel(t5_ref, q_ref, k_ref, v_ref, lam_ref, g_ref, o_ref, s0_ref, s1_ref, m0_ref,
                 m1_ref, ol0_ref, ol1_ref, corr_ref, kfar_ref, *, tq, lam_init, bucket_steps):
    b = pl.program_id(0)
    seq = k_ref.shape[0]
    n_t = seq // tq
    band = 3 * tq
    first_bucket, steps, _ = bucket_steps
    last_bucket = steps[-1][1]
    split = tq + tq // 2
    one_lane = 2 * n_t
    assert one_lane + 2 <= LANES and n_t >= 4 and n_t % 2 == 0

    @pl.when(b == 0)
    def _init():
        row = lax.broadcasted_iota(jnp.int32, (tq, band), 0)
        col = lax.broadcasted_iota(jnp.int32, (tq, band), 1)
        rel = col - tq - row
        for hh in range(DIFF_HEADS):
            val = jnp.full((tq, band), t5_ref[first_bucket, hh], F32)
            for thr, bkt in steps:
                val = jnp.where(rel >= thr, t5_ref[bkt, hh], val)
            far = jnp.where(col < split, t5_ref[first_bucket, hh], t5_ref[last_bucket, hh])
            corr_ref[hh] = val - far
        corr_ref[DIFF_HEADS] = jnp.zeros((tq, band), F32)
        key = lax.broadcasted_iota(jnp.int32, (seq, LANES), 0)
        lane = lax.broadcasted_iota(jnp.int32, (seq, LANES), 1)
        tile = jnp.where(lane < n_t, lane, lane - n_t)
        step = jnp.where(key >= tile * tq + tq // 2, 1.0, 0.0)
        ones = jnp.where(lane < one_lane + 2, 1.0, 0.0)
        kfar_ref[...] = jnp.where(lane < one_lane, step, ones).astype(BF16)

    scale = 1.0 / math.sqrt(DIFF_QK_DIM)
    lane = lax.broadcasted_iota(jnp.int32, (tq, LANES), 1)
    lo = lane < DIFF_QK_DIM
    lp = lam_ref[...]
    lam = (jnp.exp(jnp.sum(lp[0:1] * lp[1:2], axis=-1, keepdims=True))
           - jnp.exp(jnp.sum(lp[2:3] * lp[3:4], axis=-1, keepdims=True)) + lam_init)

    chunk_order = [-1, 0, 1] + list(range(2, n_t - 1))

    def tile_ids(t):
        if isinstance(t, int):
            h, qi = divmod(t, n_t)
        else:
            h, qi = lax.div(t, n_t), lax.rem(t, n_t)
        return h, qi, pl.ds(_aligned(h * LANES, LANES), LANES)

    def key_rows(qi, d):
        chunk = (qi + d) % n_t if isinstance(qi, int) else lax.rem(qi + (d + n_t), n_t)
        return pl.ds(_aligned(chunk * tq, tq), tq)

    def scores(t, s_ref, m_ref):
        h, qi, cols = tile_ids(t)
        c_neg = t5_ref[first_bucket, h]
        c_step = t5_ref[last_bucket, h] - c_neg
        row0 = _aligned(qi * tq, tq)
        q = q_ref[pl.ds(row0, tq), cols] * scale
        zero = jnp.zeros_like(q)
        far = jnp.where((lane == qi) | (lane == n_t + qi), c_step,
                        jnp.where(lane >= one_lane, c_neg, 0.0))
        far = jnp.where(lane < one_lane + 2, far, 0.0)
        far_hi = far.astype(BF16)
        far_lo = (far - far_hi.astype(F32)).astype(BF16)
        q_far = jnp.where((lane < n_t) | (lane == one_lane), far_hi, far_lo)
        lhs = jnp.concatenate(
            [jnp.concatenate([jnp.where(lo, q, zero), q_far], axis=1),
             jnp.concatenate([jnp.where(lo, zero, q), q_far], axis=1)], axis=0)
        m_run = None
        for pos, d in enumerate(chunk_order):
            rows = key_rows(qi, d)
            rhs = jnp.concatenate([k_ref[rows, cols], kfar_ref[rows, :]], axis=1)
            s = lax.dot_general(lhs, rhs, (((1,), (1,)), ((), ())), preferred_element_type=F32)
            if d in (-1, 0, 1):
                inside = (qi + d >= 0) & (qi + d < n_t)
                head = jnp.where(inside, h, DIFF_HEADS)
                corr = corr_ref[head, :, (d + 1) * tq:(d + 2) * tq]
                s = s + jnp.concatenate([corr, corr], axis=0)
            s_ref[:, pos * tq:(pos + 1) * tq] = s
            m_run = s if m_run is None else jnp.maximum(m_run, s)
        m_ref[...] = functools.reduce(
            jnp.maximum, [m_run[:, i * LANES:(i + 1) * LANES] for i in range(tq // LANES)])

    def attend(t, s_ref, m_ref, ol_ref):
        _, qi, cols = tile_ids(t)
        m = jnp.broadcast_to(jnp.max(m_ref[...], axis=-1, keepdims=True), (2 * tq, LANES))
        m = jnp.concatenate([m] * (tq // LANES), axis=1)
        ol = None
        for pos, d in enumerate(chunk_order):
            e = jnp.exp((s_ref[:, pos * tq:(pos + 1) * tq] - m).astype(BF16))
            v = v_ref[key_rows(qi, d), cols]
            part = jnp.dot(e, jnp.concatenate([v, jnp.ones_like(v)], axis=1),
                           preferred_element_type=F32)
            ol = part if ol is None else ol + part
        ol_ref[...] = ol

    def finish(t, ol_ref):
        _, qi, cols = tile_ids(t)
        o = (ol_ref[0:tq, 0:LANES] / ol_ref[0:tq, LANES:2 * LANES]
             - lam * (ol_ref[tq:2 * tq, 0:LANES] / ol_ref[tq:2 * tq, LANES:2 * LANES]))
        row0 = _aligned(qi * tq, tq)
        o_ref[pl.ds(row0, tq), cols] = (
            _rmsnorm_f32(o, g_ref[...]) * (1.0 - lam_init)).astype(BF16)

    n_tiles = DIFF_HEADS * n_t
    assert n_tiles % 2 == 0 and n_tiles >= 4
    scores(0, s0_ref, m0_ref)
    scores(1, s1_ref, m1_ref)
    attend(0, s0_ref, m0_ref, ol0_ref)

    def pair(j, carry):
        t = 2 * j
        scores(t + 2, s0_ref, m0_ref)
        finish(t, ol0_ref)
        attend(t + 1, s1_ref, m1_ref, ol1_ref)
        scores(t + 3, s1_ref, m1_ref)
        finish(t + 1, ol1_ref)
        attend(t + 2, s0_ref, m0_ref, ol0_ref)
        return carry

    lax.fori_loop(0, n_tiles // 2 - 1, pair, 0)
    finish(n_tiles - 2, ol0_ref)
    attend(n_tiles - 1, s1_ref, m1_ref, ol1_ref)
    finish(n_tiles - 1, ol1_ref)


def _diff_attention(qkv3, t5_bias, lam_params, subln_g, lam_init):
    B, S, _ = qkv3.shape
    tq = TQ_DIFF
    bucket_steps = _t5_bucket_steps(S)
    assert bucket_steps[2] <= tq and S % tq == 0 and tq % LANES == 0
    assert 2 * DIFF_QK_DIM == LANES and DIFF_V_DIM == LANES
    assert NA_WIDTH == DIFF_WIDTH
    q_col = 3
    return pl.pallas_call(
        functools.partial(_diff_kernel, tq=tq, lam_init=lam_init, bucket_steps=bucket_steps),
        out_shape=jax.ShapeDtypeStruct((B, S, DIFF_WIDTH), BF16),
        grid=(B,),
        in_specs=[
            pl.BlockSpec(memory_space=pltpu.SMEM),
            pl.BlockSpec((None, S, DIFF_WIDTH), lambda b: (b, 0, q_col)),
            pl.BlockSpec((None, S, DIFF_WIDTH), lambda b: (b, 0, q_col + 1)),
            pl.BlockSpec((None, S, DIFF_WIDTH), lambda b: (b, 0, q_col + 2)),
            pl.BlockSpec((4, DIFF_QK_DIM), lambda b: (0, 0)),
            pl.BlockSpec((1, DIFF_V_DIM), lambda b: (0, 0)),
        ],
        out_specs=pl.BlockSpec((None, S, DIFF_WIDTH), lambda b: (b, 0, 0)),
        scratch_shapes=[
            pltpu.VMEM((2 * tq, S), F32),
            pltpu.VMEM((2 * tq, S), F32),
            pltpu.VMEM((2 * tq, LANES), F32),
            pltpu.VMEM((2 * tq, LANES), F32),
            pltpu.VMEM((2 * tq, 2 * LANES), F32),
            pltpu.VMEM((2 * tq, 2 * LANES), F32),
            pltpu.VMEM((DIFF_HEADS + 1, tq, 3 * tq), F32),
            pltpu.VMEM((S, LANES), BF16),
        ],
        compiler_params=pltpu.CompilerParams(
            dimension_semantics=("arbitrary",),
            vmem_limit_bytes=_vmem_limit(48 * 2**20)),
        name="diff_attn",
    )(t5_bias, qkv3, qkv3, qkv3, lam_params, subln_g.reshape(1, DIFF_V_DIM))


def _merge_kernel(x_ref, yna_ref, ydf_ref, gate_ref, wna_ref, wdf_ref, wout_ref, o_ref):
    d = x_ref.shape[1]
    b_na = jnp.dot(yna_ref[...], wna_ref[...], preferred_element_type=F32)
    b_df = jnp.dot(ydf_ref[...], wdf_ref[...], preferred_element_type=F32)
    merged = jax.nn.sigmoid(gate_ref[:, 0:d]) * b_na + jax.nn.sigmoid(gate_ref[:, d:2 * d]) * b_df
    o_ref[...] = x_ref[...] + jnp.dot(merged.astype(BF16), wout_ref[...],
                                      preferred_element_type=F32)


def _merge_out(xt, y_na, y_df, gates, w_na, w_df, w_out, layer):
    T, D = xt.shape
    tm = TM_MERGE
    return pl.pallas_call(
        _merge_kernel,
        out_shape=jax.ShapeDtypeStruct((T, D), F32),
        grid=(T // tm,),
        in_specs=[
            pl.BlockSpec((tm, D), lambda i: (i, 0)),
            pl.BlockSpec((tm, NA_WIDTH), lambda i: (i, 0)),
            pl.BlockSpec((tm, DIFF_WIDTH), lambda i: (i, 0)),
            pl.BlockSpec((tm, 2 * D), lambda i: (i, 0)),
            pl.BlockSpec((None, NA_WIDTH, D), lambda i: (layer, 0, 0)),
            pl.BlockSpec((None, DIFF_WIDTH, D), lambda i: (layer, 0, 0)),
            pl.BlockSpec((None, D, D), lambda i: (layer, 0, 0)),
        ],
        out_specs=pl.BlockSpec((tm, D), lambda i: (i, 0)),
        compiler_params=pltpu.CompilerParams(
            dimension_semantics=("arbitrary",), vmem_limit_bytes=_vmem_limit(40 * 2**20)),
        name="merge_out",
    )(xt, y_na, y_df, gates, w_na, w_df, w_out)


def _mlp_kernel(x_ref, g_ref, w1_ref, w2_ref, fg_ref, o_ref, h_ref, *, final):
    j = pl.program_id(1)

    @pl.when(j == 0)
    def _start():
        x = x_ref[...]
        h_ref[...] = _rmsnorm_f32(x, g_ref[...]).astype(BF16)
        o_ref[...] = x

    u = jnp.dot(h_ref[...], w1_ref[...], preferred_element_type=F32)
    u = jnp.square(jnp.maximum(u, 0.0)).astype(BF16)
    o_ref[...] += jnp.dot(u, w2_ref[...], preferred_element_type=F32)

    if final:
        @pl.when(j == pl.num_programs(1) - 1)
        def _finish():
            o_ref[...] = _rmsnorm_f32(o_ref[...], fg_ref[...])


def _mlp(xt, g, w1, w2, final_g, layer, final):
    T, D = xt.shape
    F = w1.shape[2]
    tm, tf = TM_MLP, TF_MLP
    return pl.pallas_call(
        functools.partial(_mlp_kernel, final=final),
        out_shape=jax.ShapeDtypeStruct((T, D), F32),
        grid=(T // tm, F // tf),
        in_specs=[
            pl.BlockSpec((tm, D), lambda i, j: (i, 0)),
            pl.BlockSpec((1, D), lambda i, j: (0, 0)),
            pl.BlockSpec((None, D, tf), lambda i, j: (layer, 0, j)),
            pl.BlockSpec((None, tf, D), lambda i, j: (layer, j, 0)),
            pl.BlockSpec((1, D), lambda i, j: (0, 0)),
        ],
        out_specs=pl.BlockSpec((tm, D), lambda i, j: (i, 0)),
        scratch_shapes=[pltpu.VMEM((tm, D), BF16)],
        compiler_params=pltpu.CompilerParams(
            dimension_semantics=("arbitrary", "arbitrary"),
            vmem_limit_bytes=_vmem_limit(48 * 2**20)),
        name="mlp",
    )(xt, g.reshape(1, D), w1, w2, final_g.reshape(1, D))


def kernel(x, t5_bias, final_norm_g, norm1_g, w_in, na_rpb, diff_lambda, diff_subln_g, w_na_o,
           w_diff_o, w_out, norm2_g, w_ff1, w_ff2):
    B, S, D = x.shape
    depth = w_in.shape[0]
    T = B * S
    assert w_in.shape[2] == QKV_WIDTH + 2 * D and S % GRID_W == 0
    xt = x.reshape(T, D)
    na_bias = _na_bias_table(na_rpb, S // GRID_W)
    w_in, w_na_o, w_diff_o, w_out, w_ff1, w_ff2 = (
        w.astype(BF16) for w in (w_in, w_na_o, w_diff_o, w_out, w_ff1, w_ff2))
    for layer in range(depth):
        qkv, gates = _in_proj(xt, norm1_g[layer], w_in, layer)
        qkv3 = qkv.reshape(B, S, QKV_WIDTH)
        y_na = _na_attention(qkv3, na_bias, layer)
        lam_init = 0.8 - 0.6 * math.exp(-0.3 * layer)
        y_df = _diff_attention(qkv3, t5_bias, diff_lambda[layer], diff_subln_g[layer], lam_init)
        xt = _merge_out(xt, y_na.reshape(T, NA_WIDTH), y_df.reshape(T, DIFF_WIDTH), gates,
                        w_na_o, w_diff_o, w_out, layer)
        xt = _mlp(xt, norm2_g[layer], w_ff1, w_ff2, final_norm_g, layer,
                  final=(layer == depth - 1))
    return xt.reshape(B, S, D)
```

```python
import functools
import math

import numpy as np
import jax
import jax.numpy as jnp
from jax import lax
from jax.experimental import pallas as pl
from jax.experimental.pallas import tpu as pltpu

GRID_W = 64
NA_HEADS = 8
NA_HEAD_DIM = 64
NA_WIN_ROWS = 8
NA_WIN_COLS = 16
DIFF_HEADS = 4
DIFF_QK_DIM = 64
DIFF_V_DIM = 2 * DIFF_QK_DIM
T5_BUCKETS = 32
T5_MAX_DIST = 128
RMS_EPS = 1e-6
NEG_INF = -1e30

NA_WIDTH = NA_HEADS * NA_HEAD_DIM
DIFF_WIDTH = DIFF_HEADS * DIFF_V_DIM
QKV_WIDTH = 3 * NA_WIDTH + 3 * DIFF_WIDTH

LANES = 128
V7X_VMEM_BYTES = 64 * 2**20

TM_PROJ = 512
TM_MLP = 512
TF_MLP = 2048
NA_ROW_BLOCK = 4
NA_WIN_BLOCKS = 3
NA_BATCH_BLOCK = 4
TQ_DIFF = 256

BF16 = jnp.bfloat16
F32 = jnp.float32


def _vmem_limit(nbytes):
    assert nbytes < V7X_VMEM_BYTES
    return int(nbytes)


def _aligned(index, multiple):
    return index if isinstance(index, int) else pl.multiple_of(index, multiple)


def _rmsnorm_f32(x, g):
    return (x * lax.rsqrt(jnp.mean(x * x, axis=-1, keepdims=True) + RMS_EPS)) * g


def _in_proj_kernel(x_ref, g_ref, w_ref, qkv_ref, gate_ref, *, chunk):
    hb = _rmsnorm_f32(x_ref[...], g_ref[...]).astype(BF16)
    n_qkv = qkv_ref.shape[1]
    n_gate = gate_ref.shape[1]
    for c in range(n_qkv // chunk):
        cs = slice(c * chunk, (c + 1) * chunk)
        qkv_ref[:, cs] = jnp.dot(hb, w_ref[:, cs], preferred_element_type=F32).astype(BF16)
    for c in range(n_gate // chunk):
        cs = slice(c * chunk, (c + 1) * chunk)
        ws = slice(n_qkv + c * chunk, n_qkv + (c + 1) * chunk)
        gate_ref[:, cs] = jnp.dot(hb, w_ref[:, ws], preferred_element_type=F32)


def _in_proj(xt, g, w, layer):
    T, D = xt.shape
    n_all = w.shape[2]
    n_gate = n_all - QKV_WIDTH
    tm = TM_PROJ
    return pl.pallas_call(
        functools.partial(_in_proj_kernel, chunk=1024),
        out_shape=(jax.ShapeDtypeStruct((T, QKV_WIDTH), BF16),
                   jax.ShapeDtypeStruct((T, n_gate), F32)),
        grid=(T // tm,),
        in_specs=[
            pl.BlockSpec((tm, D), lambda i: (i, 0)),
            pl.BlockSpec((1, D), lambda i: (0, 0)),
            pl.BlockSpec((None, D, n_all), lambda i: (layer, 0, 0)),
        ],
        out_specs=(pl.BlockSpec((tm, QKV_WIDTH), lambda i: (i, 0)),
                   pl.BlockSpec((tm, n_gate), lambda i: (i, 0))),
        compiler_params=pltpu.CompilerParams(
            dimension_semantics=("arbitrary",), vmem_limit_bytes=_vmem_limit(56 * 2**20)),
        name="in_proj",
    )(xt, g.reshape(1, D), w)


def _na_geometry(rows):
    rb_rows = NA_ROW_BLOCK
    n_rb = rows // rb_rows
    win_rows = NA_WIN_BLOCKS * rb_rows
    wr = min(NA_WIN_ROWS, rows)
    kb = np.clip(np.arange(n_rb) - 1, 0, n_rb - NA_WIN_BLOCKS)
    dr_idx = np.zeros((n_rb, rb_rows, win_rows), np.int32)
    valid = np.zeros((n_rb, rb_rows, win_rows), bool)
    for rb in range(n_rb):
        w0 = kb[rb] * rb_rows
        for ri in range(rb_rows):
            r = rb * rb_rows + ri
            r0 = min(max(r - wr // 2, 0), rows - wr)
            assert w0 <= r0 and r0 + wr <= w0 + win_rows
            for wj in range(win_rows):
                krow = w0 + wj
                valid[rb, ri, wj] = r0 <= krow < r0 + wr
                dr_idx[rb, ri, wj] = min(max(krow - r + NA_WIN_ROWS - 1, 0), 2 * NA_WIN_ROWS - 2)
    for rb in range(2, n_rb - 1):
        assert (valid[rb] == valid[1]).all() and (dr_idx[rb] == dr_idx[1]).all()
    classes = [0, 1, n_rb - 1]
    return n_rb, kb, dr_idx[classes], valid[classes]


def _na_bias_kernel(rpb_ref, o_ref, *, dr_idx, row_valid):
    assert LANES == 2 * GRID_W
    n_cls, rb_rows, win_rows = dr_idx.shape
    c = lax.broadcasted_iota(jnp.int32, (GRID_W, LANES), 0)
    lane = lax.broadcasted_iota(jnp.int32, (GRID_W, LANES), 1)
    left = lane < GRID_W
    kc = jnp.where(left, lane, lane - GRID_W)
    c0 = jnp.clip(c - NA_WIN_COLS // 2, 0, GRID_W - NA_WIN_COLS)
    col_ok = (kc >= c0) & (kc < c0 + NA_WIN_COLS)
    neg = jnp.full((GRID_W, LANES), NEG_INF, F32)

    def half_tile(d, right):
        w = jnp.broadcast_to(rpb_ref[d:d + 1, :], (GRID_W, LANES))
        t = pltpu.roll(w, GRID_W if right else 0, 1, stride=1, stride_axis=0)
        return jnp.where(col_ok, t, neg)

    for cls in range(n_cls):
        for ri in range(rb_rows):
            for p in range(win_rows // 2):
                halves = []
                for side in range(2):
                    wj = 2 * p + side
                    ok = bool(row_valid[cls, ri, wj])
                    halves.append(half_tile(int(dr_idx[cls, ri, wj]), side == 1) if ok else neg)
                o_ref[cls, ri * GRID_W:(ri + 1) * GRID_W, p * LANES:(p + 1) * LANES] = (
                    jnp.where(left, halves[0], halves[1]))


def _na_bias_table(na_rpb, rows):
    depth, H, n_dr, n_dc = na_rpb.shape
    assert n_dr == 2 * NA_WIN_ROWS - 1 and n_dc == 2 * NA_WIN_COLS - 1
    _, _, dr_idx, row_valid = _na_geometry(rows)
    n_cls, rb_rows, win_rows = dr_idx.shape
    rq, wk = rb_rows * GRID_W, win_rows * GRID_W
    lanes = jnp.pad(na_rpb, ((0, 0), (0, 0), (0, 0), (0, LANES - n_dc)), constant_values=NEG_INF)
    lanes = jnp.roll(lanes, -(NA_WIN_COLS - 1), axis=-1)
    return pl.pallas_call(
        functools.partial(_na_bias_kernel, dr_idx=dr_idx, row_valid=row_valid),
        out_shape=jax.ShapeDtypeStruct((depth, n_cls, H, rq, wk), F32),
        grid=(depth, H),
        in_specs=[pl.BlockSpec((None, None, n_dr, LANES), lambda l, h: (l, h, 0, 0))],
        out_specs=pl.BlockSpec((None, n_cls, None, rq, wk), lambda l, h: (l, 0, h, 0, 0)),
        compiler_params=pltpu.CompilerParams(dimension_semantics=("arbitrary", "arbitrary")),
        name="na_bias",
    )(lanes)


def _na_kernel(q_ref, k0_ref, k1_ref, k2_ref, v0_ref, v1_ref, v2_ref, bias_ref, o_ref):
    nb, rq = q_ref.shape[0], q_ref.shape[1]
    scale = 1.0 / math.sqrt(NA_HEAD_DIM)
    lo = lax.broadcasted_iota(jnp.int32, (rq, LANES), 1) < NA_HEAD_DIM
    heads_per_vreg = LANES // NA_HEAD_DIM
    for bi, hp in np.ndindex(nb, NA_HEADS // heads_per_vreg):
        cs = (bi, slice(None), slice(hp * LANES, (hp + 1) * LANES))
        q = q_ref[cs] * scale
        k = jnp.concatenate([k0_ref[cs], k1_ref[cs], k2_ref[cs]], axis=0)
        v = jnp.concatenate([v0_ref[cs], v1_ref[cs], v2_ref[cs]], axis=0)
        v_ones = jnp.concatenate([v, jnp.ones_like(v)], axis=1)
        outs = []
        for e in range(heads_per_vreg):
            qm = jnp.where(lo if e == 0 else jnp.logical_not(lo), q, jnp.zeros_like(q))
            s = lax.dot_general(qm, k, (((1,), (1,)), ((), ())), preferred_element_type=F32)
            s = s + bias_ref[heads_per_vreg * hp + e]
            m = jnp.max(s, axis=-1, keepdims=True)
            p = jnp.exp((s - m).astype(BF16))
            ol = jnp.dot(p, v_ones, preferred_element_type=F32)
            outs.append(ol[:, 0:LANES] / ol[:, LANES:2 * LANES])
        o_ref[cs] = jnp.where(lo, outs[0], outs[1]).astype(BF16)


def _na_attention(qkv3, bias_table, layer):
    B, S, _ = qkv3.shape
    rows = S // GRID_W
    n_rb, kb, _, _ = _na_geometry(rows)
    rq = NA_ROW_BLOCK * GRID_W
    assert NA_HEADS * NA_HEAD_DIM == NA_WIDTH and LANES % NA_HEAD_DIM == 0
    k_col, v_col = 1, 2
    n_kb = n_rb - NA_WIN_BLOCKS

    nb = NA_BATCH_BLOCK
    assert B % nb == 0

    def kv_spec(col, t):
        return pl.BlockSpec((nb, rq, NA_WIDTH),
                            lambda rb, b: (b, jnp.clip(rb - 1, 0, n_kb) + t, col))

    def bias_map(rb, b):
        cls = (rb > 0).astype(jnp.int32) + (rb == n_rb - 1).astype(jnp.int32)
        return (layer, cls, 0, 0, 0)

    wk = NA_WIN_BLOCKS * rq
    return pl.pallas_call(
        _na_kernel,
        out_shape=jax.ShapeDtypeStruct((B, S, NA_WIDTH), BF16),
        grid=(n_rb, B // nb),
        in_specs=[pl.BlockSpec((nb, rq, NA_WIDTH), lambda rb, b: (b, rb, 0))]
        + [kv_spec(k_col, t) for t in range(NA_WIN_BLOCKS)]
        + [kv_spec(v_col, t) for t in range(NA_WIN_BLOCKS)]
        + [pl.BlockSpec((None, None, NA_HEADS, rq, wk), bias_map)],
        out_specs=pl.BlockSpec((nb, rq, NA_WIDTH), lambda rb, b: (b, rb, 0)),
        compiler_params=pltpu.CompilerParams(
            dimension_semantics=("arbitrary", "arbitrary"),
            vmem_limit_bytes=_vmem_limit(56 * 2**20)),
        name="na_attn",
    )(qkv3, qkv3, qkv3, qkv3, qkv3, qkv3, qkv3, bias_table)


def _t5_bucket_steps(seq):
    half = T5_BUCKETS // 2
    max_exact = half // 2
    rel = np.arange(-(seq - 1), seq)
    n = np.abs(rel)
    nf = np.maximum(n, 1).astype(np.float64)
    large = max_exact + np.floor(
        np.log(nf / max_exact) / math.log(T5_MAX_DIST / max_exact) * (half - max_exact) + 1e-9
    ).astype(np.int64)
    large = np.minimum(large, half - 1)
    bucket = np.where(rel > 0, half, 0) + np.where(n < max_exact, n, large)
    steps = [(int(rel[i]), int(bucket[i])) for i in range(1, len(rel)) if bucket[i] != bucket[i - 1]]
    sat = max(abs(steps[0][0]) + 1, abs(steps[-1][0]))
    return int(bucket[0]), steps, sat


def _diff_kernel(t5_ref, q_ref, k_ref, v_ref, lam_ref, g_ref, o_ref, s0_ref, s1_ref, m0_ref,
                 m1_ref, ol0_ref, ol1_ref, corr_ref, kfar_ref, *, tq, lam_init, bucket_steps):
    b = pl.program_id(0)
    seq = k_ref.shape[0]
    n_t = seq // tq
    band = 3 * tq
    first_bucket, steps, _ = bucket_steps
    last_bucket = steps[-1][1]
    split = tq + tq // 2
    one_lane = 2 * n_t
    assert one_lane + 2 <= LANES and n_t >= 4 and n_t % 2 == 0

    @pl.when(b == 0)
    def _init():
        row = lax.broadcasted_iota(jnp.int32, (tq, band), 0)
        col = lax.broadcasted_iota(jnp.int32, (tq, band), 1)
        rel = col - tq - row
        for hh in range(DIFF_HEADS):
            val = jnp.full((tq, band), t5_ref[first_bucket, hh], F32)
            for thr, bkt in steps:
                val = jnp.where(rel >= thr, t5_ref[bkt, hh], val)
            far = jnp.where(col < split, t5_ref[first_bucket, hh], t5_ref[last_bucket, hh])
            corr_ref[hh] = val - far
        corr_ref[DIFF_HEADS] = jnp.zeros((tq, band), F32)
        key = lax.broadcasted_iota(jnp.int32, (seq, LANES), 0)
        lane = lax.broadcasted_iota(jnp.int32, (seq, LANES), 1)
        tile = jnp.where(lane < n_t, lane, lane - n_t)
        step = jnp.where(key >= tile * tq + tq // 2, 1.0, 0.0)
        ones = jnp.where(lane < one_lane + 2, 1.0, 0.0)
        kfar_ref[...] = jnp.where(lane < one_lane, step, ones).astype(BF16)

    scale = 1.0 / math.sqrt(DIFF_QK_DIM)
    lane = lax.broadcasted_iota(jnp.int32, (tq, LANES), 1)
    lo = lane < DIFF_QK_DIM
    lp = lam_ref[...]
    lam = (jnp.exp(jnp.sum(lp[0:1] * lp[1:2], axis=-1, keepdims=True))
           - jnp.exp(jnp.sum(lp[2:3] * lp[3:4], axis=-1, keepdims=True)) + lam_init)

    chunk_order = [-1, 0, 1] + list(range(2, n_t - 1))

    def tile_ids(t):
        if isinstance(t, int):
            h, qi = divmod(t, n_t)
        else:
            h, qi = lax.div(t, n_t), lax.rem(t, n_t)
        return h, qi, pl.ds(_aligned(h * LANES, LANES), LANES)

    def key_rows(qi, d):
        chunk = (qi + d) % n_t if isinstance(qi, int) else lax.rem(qi + (d + n_t), n_t)
        return pl.ds(_aligned(chunk * tq, tq), tq)

    def scores(t, s_ref, m_ref):
        h, qi, cols = tile_ids(t)
        c_neg = t5_ref[first_bucket, h]
        c_step = t5_ref[last_bucket, h] - c_neg
        row0 = _aligned(qi * tq, tq)
        q = q_ref[pl.ds(row0, tq), cols] * scale
        zero = jnp.zeros_like(q)
        far = jnp.where((lane == qi) | (lane == n_t + qi), c_step,
                        jnp.where(lane >= one_lane, c_neg, 0.0))
        far = jnp.where(lane < one_lane + 2, far, 0.0)
        far_hi = far.astype(BF16)
        far_lo = (far - far_hi.astype(F32)).astype(BF16)
        q_far = jnp.where((lane < n_t) | (lane == one_lane), far_hi, far_lo)
        lhs = jnp.concatenate(
            [jnp.concatenate([jnp.where(lo, q, zero), q_far], axis=1),
             jnp.concatenate([jnp.where(lo, zero, q), q_far], axis=1)], axis=0)
        m_run = None
        for pos, d in enumerate(chunk_order):
            rows = key_rows(qi, d)
            rhs = jnp.concatenate([k_ref[rows, cols], kfar_ref[rows, :]], axis=1)
            s = lax.dot_general(lhs, rhs, (((1,), (1,)), ((), ())), preferred_element_type=F32)
            if d in (-1, 0, 1):
                inside = (qi + d >= 0) & (qi + d < n_t)
                head = jnp.where(inside, h, DIFF_HEADS)
                corr = corr_ref[head, :, (d + 1) * tq:(d + 2) * tq]
                s = s + jnp.concatenate([corr, corr], axis=0)
            s_ref[:, pos * tq:(pos + 1) * tq] = s
            m_run = s if m_run is None else jnp.maximum(m_run, s)
        m_ref[...] = functools.reduce(
            jnp.maximum, [m_run[:, i * LANES:(i + 1) * LANES] for i in range(tq // LANES)])

    def attend(t, s_ref, m_ref, ol_ref):
        _, qi, cols = tile_ids(t)
        m = jnp.broadcast_to(jnp.max(m_ref[...], axis=-1, keepdims=True), (2 * tq, LANES))
        m = jnp.concatenate([m] * (tq // LANES), axis=1)
        ol = None
        for pos, d in enumerate(chunk_order):
            e = jnp.exp((s_ref[:, pos * tq:(pos + 1) * tq] - m).astype(BF16))
            v = v_ref[key_rows(qi, d), cols]
            part = jnp.dot(e, jnp.concatenate([v, jnp.ones_like(v)], axis=1),
                           preferred_element_type=F32)
            ol = part if ol is None else ol + part
        ol_ref[...] = ol

    def finish(t, ol_ref):
        _, qi, cols = tile_ids(t)
        o = (ol_ref[0:tq, 0:LANES] / ol_ref[0:tq, LANES:2 * LANES]
             - lam * (ol_ref[tq:2 * tq, 0:LANES] / ol_ref[tq:2 * tq, LANES:2 * LANES]))
        row0 = _aligned(qi * tq, tq)
        o_ref[pl.ds(row0, tq), cols] = (
            _rmsnorm_f32(o, g_ref[...]) * (1.0 - lam_init)).astype(BF16)

    n_tiles = DIFF_HEADS * n_t
    assert n_tiles % 2 == 0 and n_tiles >= 4
    scores(0, s0_ref, m0_ref)
    scores(1, s1_ref, m1_ref)
    attend(0, s0_ref, m0_ref, ol0_ref)

    def pair(j, carry):
        t = 2 * j
        scores(t + 2, s0_ref, m0_ref)
        finish(t, ol0_ref)
        attend(t + 1, s1_ref, m1_ref, ol1_ref)
        scores(t + 3, s1_ref, m1_ref)
        finish(t + 1, ol1_ref)
        attend(t + 2, s0_ref, m0_ref, ol0_ref)
        return carry

    lax.fori_loop(0, n_tiles // 2 - 1, pair, 0)
    finish(n_tiles - 2, ol0_ref)
    attend(n_tiles - 1, s1_ref, m1_ref, ol1_ref)
    finish(n_tiles - 1, ol1_ref)


def _diff_attention(qkv3, t5_bias, lam_params, subln_g, lam_init):
    B, S, _ = qkv3.shape
    tq = TQ_DIFF
    bucket_steps = _t5_bucket_steps(S)
    assert bucket_steps[2] <= tq and S % tq == 0 and tq % LANES == 0
    assert 2 * DIFF_QK_DIM == LANES and DIFF_V_DIM == LANES
    assert NA_WIDTH == DIFF_WIDTH
    q_col = 3
    return pl.pallas_call(
        functools.partial(_diff_kernel, tq=tq, lam_init=lam_init, bucket_steps=bucket_steps),
        out_shape=jax.ShapeDtypeStruct((B, S, DIFF_WIDTH), BF16),
        grid=(B,),
        in_specs=[
            pl.BlockSpec(memory_space=pltpu.SMEM),
            pl.BlockSpec((None, S, DIFF_WIDTH), lambda b: (b, 0, q_col)),
            pl.BlockSpec((None, S, DIFF_WIDTH), lambda b: (b, 0, q_col + 1)),
            pl.BlockSpec((None, S, DIFF_WIDTH), lambda b: (b, 0, q_col + 2)),
            pl.BlockSpec((4, DIFF_QK_DIM), lambda b: (0, 0)),
            pl.BlockSpec((1, DIFF_V_DIM), lambda b: (0, 0)),
        ],
        out_specs=pl.BlockSpec((None, S, DIFF_WIDTH), lambda b: (b, 0, 0)),
        scratch_shapes=[
            pltpu.VMEM((2 * tq, S), F32),
            pltpu.VMEM((2 * tq, S), F32),
            pltpu.VMEM((2 * tq, LANES), F32),
            pltpu.VMEM((2 * tq, LANES), F32),
            pltpu.VMEM((2 * tq, 2 * LANES), F32),
            pltpu.VMEM((2 * tq, 2 * LANES), F32),
            pltpu.VMEM((DIFF_HEADS + 1, tq, 3 * tq), F32),
            pltpu.VMEM((S, LANES), BF16),
        ],
        compiler_params=pltpu.CompilerParams(
            dimension_semantics=("arbitrary",),
            vmem_limit_bytes=_vmem_limit(48 * 2**20)),
        name="diff_attn",
    )(t5_bias, qkv3, qkv3, qkv3, lam_params, subln_g.reshape(1, DIFF_V_DIM))


def _mix_mlp_kernel(x_ref, yna_ref, ydf_ref, gate_ref, wna_ref, wdf_ref, wout_ref, g_ref, w1_ref,
                    w2_ref, fg_ref, o_ref, h_ref, *, final):
    j = pl.program_id(1)

    @pl.when(j == 0)
    def _start():
        d = x_ref.shape[1]
        b_na = jnp.dot(yna_ref[...], wna_ref[...], preferred_element_type=F32)
        b_df = jnp.dot(ydf_ref[...], wdf_ref[...], preferred_element_type=F32)
        merged = (jax.nn.sigmoid(gate_ref[:, 0:d]) * b_na
                  + jax.nn.sigmoid(gate_ref[:, d:2 * d]) * b_df)
        x = x_ref[...] + jnp.dot(merged.astype(BF16), wout_ref[...],
                                 preferred_element_type=F32)
        h_ref[...] = _rmsnorm_f32(x, g_ref[...]).astype(BF16)
        o_ref[...] = x

    u = jnp.dot(h_ref[...], w1_ref[...], preferred_element_type=F32)
    u = jnp.square(jnp.maximum(u, 0.0)).astype(BF16)
    o_ref[...] += jnp.dot(u, w2_ref[...], preferred_element_type=F32)

    if final:
        @pl.when(j == pl.num_programs(1) - 1)
        def _finish():
            o_ref[...] = _rmsnorm_f32(o_ref[...], fg_ref[...])


def _mix_mlp(xt, y_na, y_df, gates, w_na, w_df, w_out, g, w1, w2, final_g, layer, final):
    T, D = xt.shape
    F = w1.shape[2]
    tm, tf = TM_MLP, TF_MLP
    once = pl.Buffered(1)
    return pl.pallas_call(
        functools.partial(_mix_mlp_kernel, final=final),
        out_shape=jax.ShapeDtypeStruct((T, D), F32),
        grid=(T // tm, F // tf),
        in_specs=[
            pl.BlockSpec((tm, D), lambda i, j: (i, 0)),
            pl.BlockSpec((tm, NA_WIDTH), lambda i, j: (i, 0)),
            pl.BlockSpec((tm, DIFF_WIDTH), lambda i, j: (i, 0)),
            pl.BlockSpec((tm, 2 * D), lambda i, j: (i, 0)),
            pl.BlockSpec((None, NA_WIDTH, D), lambda i, j: (layer, 0, 0), pipeline_mode=once),
            pl.BlockSpec((None, DIFF_WIDTH, D), lambda i, j: (layer, 0, 0), pipeline_mode=once),
            pl.BlockSpec((None, D, D), lambda i, j: (layer, 0, 0), pipeline_mode=once),
            pl.BlockSpec((1, D), lambda i, j: (0, 0)),
            pl.BlockSpec((None, D, tf), lambda i, j: (layer, 0, j)),
            pl.BlockSpec((None, tf, D), lambda i, j: (layer, j, 0)),
            pl.BlockSpec((1, D), lambda i, j: (0, 0)),
        ],
        out_specs=pl.BlockSpec((tm, D), lambda i, j: (i, 0)),
        scratch_shapes=[pltpu.VMEM((tm, D), BF16)],
        compiler_params=pltpu.CompilerParams(
            dimension_semantics=("arbitrary", "arbitrary"),
            vmem_limit_bytes=_vmem_limit(52 * 2**20)),
        name="mix_mlp",
    )(xt, y_na, y_df, gates, w_na, w_df, w_out, g.reshape(1, D), w1, w2, final_g.reshape(1, D))


def kernel(x, t5_bias, final_norm_g, norm1_g, w_in, na_rpb, diff_lambda, diff_subln_g, w_na_o,
           w_diff_o, w_out, norm2_g, w_ff1, w_ff2):
    B, S, D = x.shape
    depth = w_in.shape[0]
    T = B * S
    assert w_in.shape[2] == QKV_WIDTH + 2 * D and S % GRID_W == 0
    xt = x.reshape(T, D)
    na_bias = _na_bias_table(na_rpb, S // GRID_W)
    w_in, w_na_o, w_diff_o, w_out, w_ff1, w_ff2 = (
        w.astype(BF16) for w in (w_in, w_na_o, w_diff_o, w_out, w_ff1, w_ff2))
    for layer in range(depth):
        qkv, gates = _in_proj(xt, norm1_g[layer], w_in, layer)
        qkv3 = qkv.reshape(B, S, QKV_WIDTH)
        y_na = _na_attention(qkv3, na_bias, layer)
        lam_init = 0.8 - 0.6 * math.exp(-0.3 * layer)
        y_df = _diff_attention(qkv3, t5_bias, diff_lambda[layer], diff_subln_g[layer], lam_init)
        xt = _mix_mlp(xt, y_na.reshape(T, NA_WIDTH), y_df.reshape(T, DIFF_WIDTH), gates,
                      w_na_o, w_diff_o, w_out, norm2_g[layer], w_ff1, w_ff2, final_norm_g, layer,
                      final=(layer == depth - 1))
    return xt.reshape(B, S, D)
```

```python
import functools
import math

import numpy as np
import jax
import jax.numpy as jnp
from jax import lax
from jax.experimental import pallas as pl
from jax.experimental.pallas import tpu as pltpu

GRID_W = 64
NA_HEADS = 8
NA_HEAD_DIM = 64
NA_WIN_ROWS = 8
NA_WIN_COLS = 16
DIFF_HEADS = 4
DIFF_QK_DIM = 64
DIFF_V_DIM = 2 * DIFF_QK_DIM
T5_BUCKETS = 32
T5_MAX_DIST = 128
RMS_EPS = 1e-6
NEG_INF = -1e30

NA_WIDTH = NA_HEADS * NA_HEAD_DIM
DIFF_WIDTH = DIFF_HEADS * DIFF_V_DIM
QKV_WIDTH = 3 * NA_WIDTH + 3 * DIFF_WIDTH

LANES = 128
V7X_VMEM_BYTES = 64 * 2**20

TM_PROJ = 512
TM_MLP = 512
TF_MLP = 1024
NA_ROW_BLOCK = 4
NA_WIN_BLOCKS = 3
NA_BATCH_BLOCK = 4
TQ_DIFF = 256

BF16 = jnp.bfloat16
F32 = jnp.float32


def _vmem_limit(nbytes):
    assert nbytes < V7X_VMEM_BYTES
    return int(nbytes)


def _aligned(index, multiple):
    return index if isinstance(index, int) else pl.multiple_of(index, multiple)


def _rmsnorm_f32(x, g):
    return (x * lax.rsqrt(jnp.mean(x * x, axis=-1, keepdims=True) + RMS_EPS)) * g


def _in_proj_kernel(x_ref, g_ref, w_ref, qkv_ref, gate_ref, *, chunk):
    hb = _rmsnorm_f32(x_ref[...], g_ref[...]).astype(BF16)
    n_qkv = qkv_ref.shape[1]
    n_gate = gate_ref.shape[1]
    for c in range(n_qkv // chunk):
        cs = slice(c * chunk, (c + 1) * chunk)
        qkv_ref[:, cs] = jnp.dot(hb, w_ref[:, cs], preferred_element_type=F32).astype(BF16)
    for c in range(n_gate // chunk):
        cs = slice(c * chunk, (c + 1) * chunk)
        ws = slice(n_qkv + c * chunk, n_qkv + (c + 1) * chunk)
        gate_ref[:, cs] = jnp.dot(hb, w_ref[:, ws], preferred_element_type=F32)


def _in_proj(xt, g, w, layer):
    T, D = xt.shape
    n_all = w.shape[2]
    n_gate = n_all - QKV_WIDTH
    tm = TM_PROJ
    return pl.pallas_call(
        functools.partial(_in_proj_kernel, chunk=1024),
        out_shape=(jax.ShapeDtypeStruct((T, QKV_WIDTH), BF16),
                   jax.ShapeDtypeStruct((T, n_gate), F32)),
        grid=(T // tm,),
        in_specs=[
            pl.BlockSpec((tm, D), lambda i: (i, 0)),
            pl.BlockSpec((1, D), lambda i: (0, 0)),
            pl.BlockSpec((None, D, n_all), lambda i: (layer, 0, 0)),
        ],
        out_specs=(pl.BlockSpec((tm, QKV_WIDTH), lambda i: (i, 0)),
                   pl.BlockSpec((tm, n_gate), lambda i: (i, 0))),
        compiler_params=pltpu.CompilerParams(
            dimension_semantics=("arbitrary",), vmem_limit_bytes=_vmem_limit(56 * 2**20)),
        name="in_proj",
    )(xt, g.reshape(1, D), w)


def _na_geometry(rows):
    rb_rows = NA_ROW_BLOCK
    n_rb = rows // rb_rows
    win_rows = NA_WIN_BLOCKS * rb_rows
    wr = min(NA_WIN_ROWS, rows)
    kb = np.clip(np.arange(n_rb) - 1, 0, n_rb - NA_WIN_BLOCKS)
    dr_idx = np.zeros((n_rb, rb_rows, win_rows), np.int32)
    valid = np.zeros((n_rb, rb_rows, win_rows), bool)
    for rb in range(n_rb):
        w0 = kb[rb] * rb_rows
        for ri in range(rb_rows):
            r = rb * rb_rows + ri
            r0 = min(max(r - wr // 2, 0), rows - wr)
            assert w0 <= r0 and r0 + wr <= w0 + win_rows
            for wj in range(win_rows):
                krow = w0 + wj
                valid[rb, ri, wj] = r0 <= krow < r0 + wr
                dr_idx[rb, ri, wj] = min(max(krow - r + NA_WIN_ROWS - 1, 0), 2 * NA_WIN_ROWS - 2)
    for rb in range(2, n_rb - 1):
        assert (valid[rb] == valid[1]).all() and (dr_idx[rb] == dr_idx[1]).all()
    classes = [0, 1, n_rb - 1]
    return n_rb, kb, dr_idx[classes], valid[classes]


def _na_bias_kernel(rpb_ref, o_ref, *, dr_idx, row_valid):
    assert LANES == 2 * GRID_W
    n_cls, rb_rows, win_rows = dr_idx.shape
    c = lax.broadcasted_iota(jnp.int32, (GRID_W, LANES), 0)
    lane = lax.broadcasted_iota(jnp.int32, (GRID_W, LANES), 1)
    left = lane < GRID_W
    kc = jnp.where(left, lane, lane - GRID_W)
    c0 = jnp.clip(c - NA_WIN_COLS // 2, 0, GRID_W - NA_WIN_COLS)
    col_ok = (kc >= c0) & (kc < c0 + NA_WIN_COLS)
    neg = jnp.full((GRID_W, LANES), NEG_INF, F32)

    def half_tile(d, right):
        w = jnp.broadcast_to(rpb_ref[d:d + 1, :], (GRID_W, LANES))
        t = pltpu.roll(w, GRID_W if right else 0, 1, stride=1, stride_axis=0)
        return jnp.where(col_ok, t, neg)

    for cls in range(n_cls):
        for ri in range(rb_rows):
            for p in range(win_rows // 2):
                halves = []
                for side in range(2):
                    wj = 2 * p + side
                    ok = bool(row_valid[cls, ri, wj])
                    halves.append(half_tile(int(dr_idx[cls, ri, wj]), side == 1) if ok else neg)
                o_ref[cls, ri * GRID_W:(ri + 1) * GRID_W, p * LANES:(p + 1) * LANES] = (
                    jnp.where(left, halves[0], halves[1]))


def _na_bias_table(na_rpb, rows):
    depth, H, n_dr, n_dc = na_rpb.shape
    assert n_dr == 2 * NA_WIN_ROWS - 1 and n_dc == 2 * NA_WIN_COLS - 1
    _, _, dr_idx, row_valid = _na_geometry(rows)
    n_cls, rb_rows, win_rows = dr_idx.shape
    rq, wk = rb_rows * GRID_W, win_rows * GRID_W
    lanes = jnp.pad(na_rpb, ((0, 0), (0, 0), (0, 0), (0, LANES - n_dc)), constant_values=NEG_INF)
    lanes = jnp.roll(lanes, -(NA_WIN_COLS - 1), axis=-1)
    return pl.pallas_call(
        functools.partial(_na_bias_kernel, dr_idx=dr_idx, row_valid=row_valid),
        out_shape=jax.ShapeDtypeStruct((depth, n_cls, H, rq, wk), F32),
        grid=(depth, H),
        in_specs=[pl.BlockSpec((None, None, n_dr, LANES), lambda l, h: (l, h, 0, 0))],
        out_specs=pl.BlockSpec((None, n_cls, None, rq, wk), lambda l, h: (l, 0, h, 0, 0)),
        compiler_params=pltpu.CompilerParams(dimension_semantics=("arbitrary", "arbitrary")),
        name="na_bias",
    )(lanes)


def _na_kernel(q_ref, k0_ref, k1_ref, k2_ref, v0_ref, v1_ref, v2_ref, bias_ref, o_ref):
    nb, rq = q_ref.shape[0], q_ref.shape[1]
    scale = 1.0 / math.sqrt(NA_HEAD_DIM)
    lo = lax.broadcasted_iota(jnp.int32, (rq, LANES), 1) < NA_HEAD_DIM
    heads_per_vreg = LANES // NA_HEAD_DIM
    for bi, hp in np.ndindex(nb, NA_HEADS // heads_per_vreg):
        cs = (bi, slice(None), slice(hp * LANES, (hp + 1) * LANES))
        q = q_ref[cs] * scale
        k = jnp.concatenate([k0_ref[cs], k1_ref[cs], k2_ref[cs]], axis=0)
        v = jnp.concatenate([v0_ref[cs], v1_ref[cs], v2_ref[cs]], axis=0)
        v_ones = jnp.concatenate([v, jnp.ones_like(v)], axis=1)
        outs = []
        for e in range(heads_per_vreg):
            qm = jnp.where(lo if e == 0 else jnp.logical_not(lo), q, jnp.zeros_like(q))
            s = lax.dot_general(qm, k, (((1,), (1,)), ((), ())), preferred_element_type=F32)
            s = s + bias_ref[heads_per_vreg * hp + e]
            m = jnp.max(s, axis=-1, keepdims=True)
            p = jnp.exp((s - m).astype(BF16))
            ol = jnp.dot(p, v_ones, preferred_element_type=F32)
            outs.append(ol[:, 0:LANES] / ol[:, LANES:2 * LANES])
        o_ref[cs] = jnp.where(lo, outs[0], outs[1]).astype(BF16)


def _na_attention(qkv3, bias_table, layer):
    B, S, _ = qkv3.shape
    rows = S // GRID_W
    n_rb, kb, _, _ = _na_geometry(rows)
    rq = NA_ROW_BLOCK * GRID_W
    assert NA_HEADS * NA_HEAD_DIM == NA_WIDTH and LANES % NA_HEAD_DIM == 0
    k_col, v_col = 1, 2
    n_kb = n_rb - NA_WIN_BLOCKS

    nb = NA_BATCH_BLOCK
    assert B % nb == 0

    def kv_spec(col, t):
        return pl.BlockSpec((nb, rq, NA_WIDTH),
                            lambda rb, b: (b, jnp.clip(rb - 1, 0, n_kb) + t, col))

    def bias_map(rb, b):
        cls = (rb > 0).astype(jnp.int32) + (rb == n_rb - 1).astype(jnp.int32)
        return (layer, cls, 0, 0, 0)

    wk = NA_WIN_BLOCKS * rq
    return pl.pallas_call(
        _na_kernel,
        out_shape=jax.ShapeDtypeStruct((B, S, NA_WIDTH), BF16),
        grid=(n_rb, B // nb),
        in_specs=[pl.BlockSpec((nb, rq, NA_WIDTH), lambda rb, b: (b, rb, 0))]
        + [kv_spec(k_col, t) for t in range(NA_WIN_BLOCKS)]
        + [kv_spec(v_col, t) for t in range(NA_WIN_BLOCKS)]
        + [pl.BlockSpec((None, None, NA_HEADS, rq, wk), bias_map)],
        out_specs=pl.BlockSpec((nb, rq, NA_WIDTH), lambda rb, b: (b, rb, 0)),
        compiler_params=pltpu.CompilerParams(
            dimension_semantics=("arbitrary", "arbitrary"),
            vmem_limit_bytes=_vmem_limit(56 * 2**20)),
        name="na_attn",
    )(qkv3, qkv3, qkv3, qkv3, qkv3, qkv3, qkv3, bias_table)


def _t5_bucket_steps(seq):
    half = T5_BUCKETS // 2
    max_exact = half // 2
    rel = np.arange(-(seq - 1), seq)
    n = np.abs(rel)
    nf = np.maximum(n, 1).astype(np.float64)
    large = max_exact + np.floor(
        np.log(nf / max_exact) / math.log(T5_MAX_DIST / max_exact) * (half - max_exact) + 1e-9
    ).astype(np.int64)
    large = np.minimum(large, half - 1)
    bucket = np.where(rel > 0, half, 0) + np.where(n < max_exact, n, large)
    steps = [(int(rel[i]), int(bucket[i])) for i in range(1, len(rel)) if bucket[i] != bucket[i - 1]]
    sat = max(abs(steps[0][0]) + 1, abs(steps[-1][0]))
    return int(bucket[0]), steps, sat


def _diff_kernel(t5_ref, q_ref, k_ref, v_ref, lam_ref, g_ref, o_ref, s0_ref, s1_ref, m0_ref,
                 m1_ref, ol0_ref, ol1_ref, corr_ref, kfar_ref, *, tq, lam_init, bucket_steps):
    b = pl.program_id(0)
    seq = k_ref.shape[0]
    n_t = seq // tq
    band = 3 * tq
    first_bucket, steps, _ = bucket_steps
    last_bucket = steps[-1][1]
    split = tq + tq // 2
    one_lane = 2 * n_t
    assert one_lane + 2 <= LANES and n_t >= 4 and n_t % 2 == 0

    @pl.when(b == 0)
    def _init():
        row = lax.broadcasted_iota(jnp.int32, (tq, band), 0)
        col = lax.broadcasted_iota(jnp.int32, (tq, band), 1)
        rel = col - tq - row
        for hh in range(DIFF_HEADS):
            val = jnp.full((tq, band), t5_ref[first_bucket, hh], F32)
            for thr, bkt in steps:
                val = jnp.where(rel >= thr, t5_ref[bkt, hh], val)
            far = jnp.where(col < split, t5_ref[first_bucket, hh], t5_ref[last_bucket, hh])
            corr_ref[hh] = val - far
        corr_ref[DIFF_HEADS] = jnp.zeros((tq, band), F32)
        key = lax.broadcasted_iota(jnp.int32, (seq, LANES), 0)
        lane = lax.broadcasted_iota(jnp.int32, (seq, LANES), 1)
        tile = jnp.where(lane < n_t, lane, lane - n_t)
        step = jnp.where(key >= tile * tq + tq // 2, 1.0, 0.0)
        ones = jnp.where(lane < one_lane + 2, 1.0, 0.0)
        kfar_ref[...] = jnp.where(lane < one_lane, step, ones).astype(BF16)

    scale = 1.0 / math.sqrt(DIFF_QK_DIM)
    lane = lax.broadcasted_iota(jnp.int32, (tq, LANES), 1)
    lo = lane < DIFF_QK_DIM
    lp = lam_ref[...]
    lam = (jnp.exp(jnp.sum(lp[0:1] * lp[1:2], axis=-1, keepdims=True))
           - jnp.exp(jnp.sum(lp[2:3] * lp[3:4], axis=-1, keepdims=True)) + lam_init)

    chunk_order = [-1, 0, 1] + list(range(2, n_t - 1))

    def tile_ids(t):
        if isinstance(t, int):
            h, qi = divmod(t, n_t)
        else:
            h, qi = lax.div(t, n_t), lax.rem(t, n_t)
        return h, qi, pl.ds(_aligned(h * LANES, LANES), LANES)

    def key_rows(qi, d):
        chunk = (qi + d) % n_t if isinstance(qi, int) else lax.rem(qi + (d + n_t), n_t)
        return pl.ds(_aligned(chunk * tq, tq), tq)

    def scores(t, s_ref, m_ref):
        h, qi, cols = tile_ids(t)
        c_neg = t5_ref[first_bucket, h]
        c_step = t5_ref[last_bucket, h] - c_neg
        row0 = _aligned(qi * tq, tq)
        q = q_ref[pl.ds(row0, tq), cols] * scale
        zero = jnp.zeros_like(q)
        far = jnp.where((lane == qi) | (lane == n_t + qi), c_step,
                        jnp.where(lane >= one_lane, c_neg, 0.0))
        far = jnp.where(lane < one_lane + 2, far, 0.0)
        far_hi = far.astype(BF16)
        far_lo = (far - far_hi.astype(F32)).astype(BF16)
        q_far = jnp.where((lane < n_t) | (lane == one_lane), far_hi, far_lo)
        lhs = jnp.concatenate(
            [jnp.concatenate([jnp.where(lo, q, zero), q_far], axis=1),
             jnp.concatenate([jnp.where(lo, zero, q), q_far], axis=1)], axis=0)
        m_run = None
        for pos, d in enumerate(chunk_order):
            rows = key_rows(qi, d)
            rhs = jnp.concatenate([k_ref[rows, cols], kfar_ref[rows, :]], axis=1)
            s = lax.dot_general(lhs, rhs, (((1,), (1,)), ((), ())), preferred_element_type=F32)
            if d in (-1, 0, 1):
                inside = (qi + d >= 0) & (qi + d < n_t)
                head = jnp.where(inside, h, DIFF_HEADS)
                corr = corr_ref[head, :, (d + 1) * tq:(d + 2) * tq]
                s = s + jnp.concatenate([corr, corr], axis=0)
            s_ref[:, pos * tq:(pos + 1) * tq] = s
            m_run = s if m_run is None else jnp.maximum(m_run, s)
        m_ref[...] = functools.reduce(
            jnp.maximum, [m_run[:, i * LANES:(i + 1) * LANES] for i in range(tq // LANES)])

    def attend(t, s_ref, m_ref, ol_ref):
        _, qi, cols = tile_ids(t)
        m = jnp.broadcast_to(jnp.max(m_ref[...], axis=-1, keepdims=True), (2 * tq, LANES))
        m = jnp.concatenate([m] * (tq // LANES), axis=1)
        ol = None
        for pos, d in enumerate(chunk_order):
            e = jnp.exp((s_ref[:, pos * tq:(pos + 1) * tq] - m).astype(BF16))
            v = v_ref[key_rows(qi, d), cols]
            part = jnp.dot(e, jnp.concatenate([v, jnp.ones_like(v)], axis=1),
                           preferred_element_type=F32)
            ol = part if ol is None else ol + part
        ol_ref[...] = ol

    def finish(t, ol_ref):
        _, qi, cols = tile_ids(t)
        o = (ol_ref[0:tq, 0:LANES] / ol_ref[0:tq, LANES:2 * LANES]
             - lam * (ol_ref[tq:2 * tq, 0:LANES] / ol_ref[tq:2 * tq, LANES:2 * LANES]))
        row0 = _aligned(qi * tq, tq)
        o_ref[pl.ds(row0, tq), cols] = (
            _rmsnorm_f32(o, g_ref[...]) * (1.0 - lam_init)).astype(BF16)

    n_tiles = DIFF_HEADS * n_t
    assert n_tiles % 2 == 0 and n_tiles >= 4
    scores(0, s0_ref, m0_ref)
    scores(1, s1_ref, m1_ref)
    attend(0, s0_ref, m0_ref, ol0_ref)

    def pair(j, carry):
        t = 2 * j
        scores(t + 2, s0_ref, m0_ref)
        finish(t, ol0_ref)
        attend(t + 1, s1_ref, m1_ref, ol1_ref)
        scores(t + 3, s1_ref, m1_ref)
        finish(t + 1, ol1_ref)
        attend(t + 2, s0_ref, m0_ref, ol0_ref)
        return carry

    lax.fori_loop(0, n_tiles // 2 - 1, pair, 0)
    finish(n_tiles - 2, ol0_ref)
    attend(n_tiles - 1, s1_ref, m1_ref, ol1_ref)
    finish(n_tiles - 1, ol1_ref)


def _diff_attention(qkv3, t5_bias, lam_params, subln_g, lam_init):
    B, S, _ = qkv3.shape
    tq = TQ_DIFF
    bucket_steps = _t5_bucket_steps(S)
    assert bucket_steps[2] <= tq and S % tq == 0 and tq % LANES == 0
    assert 2 * DIFF_QK_DIM == LANES and DIFF_V_DIM == LANES
    assert NA_WIDTH == DIFF_WIDTH
    q_col = 3
    return pl.pallas_call(
        functools.partial(_diff_kernel, tq=tq, lam_init=lam_init, bucket_steps=bucket_steps),
        out_shape=jax.ShapeDtypeStruct((B, S, DIFF_WIDTH), BF16),
        grid=(B,),
        in_specs=[
            pl.BlockSpec(memory_space=pltpu.SMEM),
            pl.BlockSpec((None, S, DIFF_WIDTH), lambda b: (b, 0, q_col)),
            pl.BlockSpec((None, S, DIFF_WIDTH), lambda b: (b, 0, q_col + 1)),
            pl.BlockSpec((None, S, DIFF_WIDTH), lambda b: (b, 0, q_col + 2)),
            pl.BlockSpec((4, DIFF_QK_DIM), lambda b: (0, 0)),
            pl.BlockSpec((1, DIFF_V_DIM), lambda b: (0, 0)),
        ],
        out_specs=pl.BlockSpec((None, S, DIFF_WIDTH), lambda b: (b, 0, 0)),
        scratch_shapes=[
            pltpu.VMEM((2 * tq, S), F32),
            pltpu.VMEM((2 * tq, S), F32),
            pltpu.VMEM((2 * tq, LANES), F32),
            pltpu.VMEM((2 * tq, LANES), F32),
            pltpu.VMEM((2 * tq, 2 * LANES), F32),
            pltpu.VMEM((2 * tq, 2 * LANES), F32),
            pltpu.VMEM((DIFF_HEADS + 1, tq, 3 * tq), F32),
            pltpu.VMEM((S, LANES), BF16),
        ],
        compiler_params=pltpu.CompilerParams(
            dimension_semantics=("arbitrary",),
            vmem_limit_bytes=_vmem_limit(48 * 2**20)),
        name="diff_attn",
    )(t5_bias, qkv3, qkv3, qkv3, lam_params, subln_g.reshape(1, DIFF_V_DIM))


def _mix_mlp_kernel(x_ref, yna_ref, ydf_ref, gate_ref, wna_ref, wdf_ref, wout_ref, g_ref, w1_ref,
                    w2_ref, fg_ref, o_ref, *, tf, final):
    d = x_ref.shape[1]
    b_na = jnp.dot(yna_ref[...], wna_ref[...], preferred_element_type=F32)
    b_df = jnp.dot(ydf_ref[...], wdf_ref[...], preferred_element_type=F32)
    merged = (jax.nn.sigmoid(gate_ref[:, 0:d]) * b_na
              + jax.nn.sigmoid(gate_ref[:, d:2 * d]) * b_df)
    x = x_ref[...] + jnp.dot(merged.astype(BF16), wout_ref[...], preferred_element_type=F32)
    h = _rmsnorm_f32(x, g_ref[...]).astype(BF16)
    for c in range(w1_ref.shape[1] // tf):
        u = jnp.dot(h, w1_ref[:, c * tf:(c + 1) * tf], preferred_element_type=F32)
        u = jnp.square(jnp.maximum(u, 0.0)).astype(BF16)
        x = x + jnp.dot(u, w2_ref[c * tf:(c + 1) * tf, :], preferred_element_type=F32)
    o_ref[...] = _rmsnorm_f32(x, fg_ref[...]) if final else x


def _mix_mlp(xt, y_na, y_df, gates, w_na, w_df, w_out, g, w1, w2, final_g, layer, final):
    T, D = xt.shape
    F = w1.shape[2]
    tm, tf = TM_MLP, TF_MLP
    once = pl.Buffered(1)
    assert F % tf == 0
    return pl.pallas_call(
        functools.partial(_mix_mlp_kernel, tf=tf, final=final),
        out_shape=jax.ShapeDtypeStruct((T, D), F32),
        grid=(T // tm,),
        in_specs=[
            pl.BlockSpec((tm, D), lambda i: (i, 0)),
            pl.BlockSpec((tm, NA_WIDTH), lambda i: (i, 0)),
            pl.BlockSpec((tm, DIFF_WIDTH), lambda i: (i, 0)),
            pl.BlockSpec((tm, 2 * D), lambda i: (i, 0)),
            pl.BlockSpec((None, NA_WIDTH, D), lambda i: (layer, 0, 0), pipeline_mode=once),
            pl.BlockSpec((None, DIFF_WIDTH, D), lambda i: (layer, 0, 0), pipeline_mode=once),
            pl.BlockSpec((None, D, D), lambda i: (layer, 0, 0), pipeline_mode=once),
            pl.BlockSpec((1, D), lambda i: (0, 0)),
            pl.BlockSpec((None, D, F), lambda i: (layer, 0, 0), pipeline_mode=once),
            pl.BlockSpec((None, F, D), lambda i: (layer, 0, 0), pipeline_mode=once),
            pl.BlockSpec((1, D), lambda i: (0, 0)),
        ],
        out_specs=pl.BlockSpec((tm, D), lambda i: (i, 0)),
        compiler_params=pltpu.CompilerParams(
            dimension_semantics=("arbitrary",),
            vmem_limit_bytes=_vmem_limit(56 * 2**20)),
        name="mix_mlp",
    )(xt, y_na, y_df, gates, w_na, w_df, w_out, g.reshape(1, D), w1, w2, final_g.reshape(1, D))


def kernel(x, t5_bias, final_norm_g, norm1_g, w_in, na_rpb, diff_lambda, diff_subln_g, w_na_o,
           w_diff_o, w_out, norm2_g, w_ff1, w_ff2):
    B, S, D = x.shape
    depth = w_in.shape[0]
    T = B * S
    assert w_in.shape[2] == QKV_WIDTH + 2 * D and S % GRID_W == 0
    xt = x.reshape(T, D)
    na_bias = _na_bias_table(na_rpb, S // GRID_W)
    w_in, w_na_o, w_diff_o, w_out, w_ff1, w_ff2 = (
        w.astype(BF16) for w in (w_in, w_na_o, w_diff_o, w_out, w_ff1, w_ff2))
    for layer in range(depth):
        qkv, gates = _in_proj(xt, norm1_g[layer], w_in, layer)
        qkv3 = qkv.reshape(B, S, QKV_WIDTH)
        y_na = _na_attention(qkv3, na_bias, layer)
        lam_init = 0.8 - 0.6 * math.exp(-0.3 * layer)
        y_df = _diff_attention(qkv3, t5_bias, diff_lambda[layer], diff_subln_g[layer], lam_init)
        xt = _mix_mlp(xt, y_na.reshape(T, NA_WIDTH), y_df.reshape(T, DIFF_WIDTH), gates,
                      w_na_o, w_diff_o, w_out, norm2_g[layer], w_ff1, w_ff2, final_norm_g, layer,
                      final=(layer == depth - 1))
    return xt.reshape(B, S, D)
```

```python
import functools
import math

import numpy as np
import jax
import jax.numpy as jnp
from jax import lax
from jax.experimental import pallas as pl
from jax.experimental.pallas import tpu as pltpu

GRID_W = 64
NA_HEADS = 8
NA_HEAD_DIM = 64
NA_WIN_ROWS = 8
NA_WIN_COLS = 16
DIFF_HEADS = 4
DIFF_QK_DIM = 64
DIFF_V_DIM = 2 * DIFF_QK_DIM
T5_BUCKETS = 32
T5_MAX_DIST = 128
RMS_EPS = 1e-6
NEG_INF = -1e30

NA_WIDTH = NA_HEADS * NA_HEAD_DIM
DIFF_WIDTH = DIFF_HEADS * DIFF_V_DIM
QKV_WIDTH = 3 * NA_WIDTH + 3 * DIFF_WIDTH

LANES = 128
V7X_VMEM_BYTES = 64 * 2**20

TM_PROJ = 512
TM_MLP = 512
TF_MLP = 1024
NA_ROW_BLOCK = 4
NA_WIN_BLOCKS = 3
NA_BATCH_BLOCK = 4
TQ_DIFF = 256

BF16 = jnp.bfloat16
F32 = jnp.float32


def _vmem_limit(nbytes):
    assert nbytes < V7X_VMEM_BYTES
    return int(nbytes)


def _aligned(index, multiple):
    return index if isinstance(index, int) else pl.multiple_of(index, multiple)


def _rmsnorm_f32(x, g):
    return (x * lax.rsqrt(jnp.mean(x * x, axis=-1, keepdims=True) + RMS_EPS)) * g


def _in_proj_kernel(x_ref, g_ref, w_ref, qkv_ref, gate_ref, *, chunk):
    hb = _rmsnorm_f32(x_ref[...], g_ref[...]).astype(BF16)
    n_qkv = qkv_ref.shape[1]
    n_gate = gate_ref.shape[1]
    for c in range(n_qkv // chunk):
        cs = slice(c * chunk, (c + 1) * chunk)
        qkv_ref[:, cs] = jnp.dot(hb, w_ref[:, cs], preferred_element_type=F32).astype(BF16)
    for c in range(n_gate // chunk):
        cs = slice(c * chunk, (c + 1) * chunk)
        ws = slice(n_qkv + c * chunk, n_qkv + (c + 1) * chunk)
        gate_ref[:, cs] = jnp.dot(hb, w_ref[:, ws], preferred_element_type=F32)


def _in_proj(xt, g, w, layer):
    T, D = xt.shape
    n_all = w.shape[2]
    n_gate = n_all - QKV_WIDTH
    tm = TM_PROJ
    return pl.pallas_call(
        functools.partial(_in_proj_kernel, chunk=1024),
        out_shape=(jax.ShapeDtypeStruct((T, QKV_WIDTH), BF16),
                   jax.ShapeDtypeStruct((T, n_gate), F32)),
        grid=(T // tm,),
        in_specs=[
            pl.BlockSpec((tm, D), lambda i: (i, 0)),
            pl.BlockSpec((1, D), lambda i: (0, 0)),
            pl.BlockSpec((None, D, n_all), lambda i: (layer, 0, 0)),
        ],
        out_specs=(pl.BlockSpec((tm, QKV_WIDTH), lambda i: (i, 0)),
                   pl.BlockSpec((tm, n_gate), lambda i: (i, 0))),
        compiler_params=pltpu.CompilerParams(
            dimension_semantics=("arbitrary",), vmem_limit_bytes=_vmem_limit(56 * 2**20)),
        name="in_proj",
    )(xt, g.reshape(1, D), w)


def _na_geometry(rows):
    rb_rows = NA_ROW_BLOCK
    n_rb = rows // rb_rows
    win_rows = NA_WIN_BLOCKS * rb_rows
    wr = min(NA_WIN_ROWS, rows)
    kb = np.clip(np.arange(n_rb) - 1, 0, n_rb - NA_WIN_BLOCKS)
    dr_idx = np.zeros((n_rb, rb_rows, win_rows), np.int32)
    valid = np.zeros((n_rb, rb_rows, win_rows), bool)
    for rb in range(n_rb):
        w0 = kb[rb] * rb_rows
        for ri in range(rb_rows):
            r = rb * rb_rows + ri
            r0 = min(max(r - wr // 2, 0), rows - wr)
            assert w0 <= r0 and r0 + wr <= w0 + win_rows
            for wj in range(win_rows):
                krow = w0 + wj
                valid[rb, ri, wj] = r0 <= krow < r0 + wr
                dr_idx[rb, ri, wj] = min(max(krow - r + NA_WIN_ROWS - 1, 0), 2 * NA_WIN_ROWS - 2)
    for rb in range(2, n_rb - 1):
        assert (valid[rb] == valid[1]).all() and (dr_idx[rb] == dr_idx[1]).all()
    classes = [0, 1, n_rb - 1]
    return n_rb, kb, dr_idx[classes], valid[classes]


def _na_bias_kernel(rpb_ref, o_ref, *, dr_idx, row_valid):
    assert LANES == 2 * GRID_W
    n_cls, rb_rows, win_rows = dr_idx.shape
    c = lax.broadcasted_iota(jnp.int32, (GRID_W, LANES), 0)
    lane = lax.broadcasted_iota(jnp.int32, (GRID_W, LANES), 1)
    left = lane < GRID_W
    kc = jnp.where(left, lane, lane - GRID_W)
    c0 = jnp.clip(c - NA_WIN_COLS // 2, 0, GRID_W - NA_WIN_COLS)
    col_ok = (kc >= c0) & (kc < c0 + NA_WIN_COLS)
    neg = jnp.full((GRID_W, LANES), NEG_INF, F32)

    def half_tile(d, right):
        w = jnp.broadcast_to(rpb_ref[d:d + 1, :], (GRID_W, LANES))
        t = pltpu.roll(w, GRID_W if right else 0, 1, stride=1, stride_axis=0)
        return jnp.where(col_ok, t, neg)

    for cls in range(n_cls):
        for ri in range(rb_rows):
            for p in range(win_rows // 2):
                halves = []
                for side in range(2):
                    wj = 2 * p + side
                    ok = bool(row_valid[cls, ri, wj])
                    halves.append(half_tile(int(dr_idx[cls, ri, wj]), side == 1) if ok else neg)
                o_ref[cls, ri * GRID_W:(ri + 1) * GRID_W, p * LANES:(p + 1) * LANES] = (
                    jnp.where(left, halves[0], halves[1]))


def _na_bias_table(na_rpb, rows):
    depth, H, n_dr, n_dc = na_rpb.shape
    assert n_dr == 2 * NA_WIN_ROWS - 1 and n_dc == 2 * NA_WIN_COLS - 1
    _, _, dr_idx, row_valid = _na_geometry(rows)
    n_cls, rb_rows, win_rows = dr_idx.shape
    rq, wk = rb_rows * GRID_W, win_rows * GRID_W
    lanes = jnp.pad(na_rpb, ((0, 0), (0, 0), (0, 0), (0, LANES - n_dc)), constant_values=NEG_INF)
    lanes = jnp.roll(lanes, -(NA_WIN_COLS - 1), axis=-1)
    return pl.pallas_call(
        functools.partial(_na_bias_kernel, dr_idx=dr_idx, row_valid=row_valid),
        out_shape=jax.ShapeDtypeStruct((depth, n_cls, H, rq, wk), F32),
        grid=(depth, H),
        in_specs=[pl.BlockSpec((None, None, n_dr, LANES), lambda l, h: (l, h, 0, 0))],
        out_specs=pl.BlockSpec((None, n_cls, None, rq, wk), lambda l, h: (l, 0, h, 0, 0)),
        compiler_params=pltpu.CompilerParams(dimension_semantics=("arbitrary", "arbitrary")),
        name="na_bias",
    )(lanes)


def _na_kernel(q_ref, k0_ref, k1_ref, k2_ref, v0_ref, v1_ref, v2_ref, bias_ref, o_ref):
    nb, rq = q_ref.shape[0], q_ref.shape[1]
    scale = 1.0 / math.sqrt(NA_HEAD_DIM)
    lo = lax.broadcasted_iota(jnp.int32, (rq, LANES), 1) < NA_HEAD_DIM
    heads_per_vreg = LANES // NA_HEAD_DIM
    for bi, hp in np.ndindex(nb, NA_HEADS // heads_per_vreg):
        cs = (bi, slice(None), slice(hp * LANES, (hp + 1) * LANES))
        q = q_ref[cs] * scale
        k = jnp.concatenate([k0_ref[cs], k1_ref[cs], k2_ref[cs]], axis=0)
        v = jnp.concatenate([v0_ref[cs], v1_ref[cs], v2_ref[cs]], axis=0)
        v_ones = jnp.concatenate([v, jnp.ones_like(v)], axis=1)
        outs = []
        for e in range(heads_per_vreg):
            qm = jnp.where(lo if e == 0 else jnp.logical_not(lo), q, jnp.zeros_like(q))
            s = lax.dot_general(qm, k, (((1,), (1,)), ((), ())), preferred_element_type=F32)
            s = s + bias_ref[heads_per_vreg * hp + e]
            m = jnp.max(s, axis=-1, keepdims=True)
            p = jnp.exp((s - m).astype(BF16))
            ol = jnp.dot(p, v_ones, preferred_element_type=F32)
            outs.append(ol[:, 0:LANES] / ol[:, LANES:2 * LANES])
        o_ref[cs] = jnp.where(lo, outs[0], outs[1]).astype(BF16)


def _na_attention(qkv3, bias_table, layer):
    B, S, _ = qkv3.shape
    rows = S // GRID_W
    n_rb, kb, _, _ = _na_geometry(rows)
    rq = NA_ROW_BLOCK * GRID_W
    assert NA_HEADS * NA_HEAD_DIM == NA_WIDTH and LANES % NA_HEAD_DIM == 0
    k_col, v_col = 1, 2
    n_kb = n_rb - NA_WIN_BLOCKS

    nb = NA_BATCH_BLOCK
    assert B % nb == 0

    def kv_spec(col, t):
        return pl.BlockSpec((nb, rq, NA_WIDTH),
                            lambda rb, b: (b, jnp.clip(rb - 1, 0, n_kb) + t, col))

    def bias_map(rb, b):
        cls = (rb > 0).astype(jnp.int32) + (rb == n_rb - 1).astype(jnp.int32)
        return (layer, cls, 0, 0, 0)

    wk = NA_WIN_BLOCKS * rq
    return pl.pallas_call(
        _na_kernel,
        out_shape=jax.ShapeDtypeStruct((B, S, NA_WIDTH), BF16),
        grid=(n_rb, B // nb),
        in_specs=[pl.BlockSpec((nb, rq, NA_WIDTH), lambda rb, b: (b, rb, 0))]
        + [kv_spec(k_col, t) for t in range(NA_WIN_BLOCKS)]
        + [kv_spec(v_col, t) for t in range(NA_WIN_BLOCKS)]
        + [pl.BlockSpec((None, None, NA_HEADS, rq, wk), bias_map)],
        out_specs=pl.BlockSpec((nb, rq, NA_WIDTH), lambda rb, b: (b, rb, 0)),
        compiler_params=pltpu.CompilerParams(
            dimension_semantics=("arbitrary", "arbitrary"),
            vmem_limit_bytes=_vmem_limit(56 * 2**20)),
        name="na_attn",
    )(qkv3, qkv3, qkv3, qkv3, qkv3, qkv3, qkv3, bias_table)


def _t5_bucket_steps(seq):
    half = T5_BUCKETS // 2
    max_exact = half // 2
    rel = np.arange(-(seq - 1), seq)
    n = np.abs(rel)
    nf = np.maximum(n, 1).astype(np.float64)
    large = max_exact + np.floor(
        np.log(nf / max_exact) / math.log(T5_MAX_DIST / max_exact) * (half - max_exact) + 1e-9
    ).astype(np.int64)
    large = np.minimum(large, half - 1)
    bucket = np.where(rel > 0, half, 0) + np.where(n < max_exact, n, large)
    steps = [(int(rel[i]), int(bucket[i])) for i in range(1, len(rel)) if bucket[i] != bucket[i - 1]]
    sat = max(abs(steps[0][0]) + 1, abs(steps[-1][0]))
    return int(bucket[0]), steps, sat


def _diff_kernel(t5_ref, q_ref, k_ref, v_ref, lam_ref, g_ref, o_ref, *scratch, tq, n_buf,
                 lam_init, bucket_steps):
    s_refs, m_refs, ol_refs = (scratch[i * n_buf:(i + 1) * n_buf] for i in range(3))
    corr_ref, kfar_ref = scratch[3 * n_buf:]
    b = pl.program_id(0)
    seq = k_ref.shape[0]
    n_t = seq // tq
    band = 3 * tq
    first_bucket, steps, _ = bucket_steps
    last_bucket = steps[-1][1]
    split = tq + tq // 2
    one_lane = 2 * n_t
    assert one_lane + 2 <= LANES and n_t >= 4 and n_t % 2 == 0

    @pl.when(b == 0)
    def _init():
        row = lax.broadcasted_iota(jnp.int32, (tq, band), 0)
        col = lax.broadcasted_iota(jnp.int32, (tq, band), 1)
        rel = col - tq - row
        for hh in range(DIFF_HEADS):
            val = jnp.full((tq, band), t5_ref[first_bucket, hh], F32)
            for thr, bkt in steps:
                val = jnp.where(rel >= thr, t5_ref[bkt, hh], val)
            far = jnp.where(col < split, t5_ref[first_bucket, hh], t5_ref[last_bucket, hh])
            corr_ref[hh] = val - far
        corr_ref[DIFF_HEADS] = jnp.zeros((tq, band), F32)
        key = lax.broadcasted_iota(jnp.int32, (seq, LANES), 0)
        lane = lax.broadcasted_iota(jnp.int32, (seq, LANES), 1)
        tile = jnp.where(lane < n_t, lane, lane - n_t)
        step = jnp.where(key >= tile * tq + tq // 2, 1.0, 0.0)
        ones = jnp.where(lane < one_lane + 2, 1.0, 0.0)
        kfar_ref[...] = jnp.where(lane < one_lane, step, ones).astype(BF16)

    scale = 1.0 / math.sqrt(DIFF_QK_DIM)
    lane = lax.broadcasted_iota(jnp.int32, (tq, LANES), 1)
    lo = lane < DIFF_QK_DIM
    lp = lam_ref[...]
    lam = (jnp.exp(jnp.sum(lp[0:1] * lp[1:2], axis=-1, keepdims=True))
           - jnp.exp(jnp.sum(lp[2:3] * lp[3:4], axis=-1, keepdims=True)) + lam_init)

    chunk_order = [-1, 0, 1] + list(range(2, n_t - 1))

    def tile_ids(t):
        if isinstance(t, int):
            h, qi = divmod(t, n_t)
        else:
            h, qi = lax.div(t, n_t), lax.rem(t, n_t)
        return h, qi, pl.ds(_aligned(h * LANES, LANES), LANES)

    def key_rows(qi, d):
        chunk = (qi + d) % n_t if isinstance(qi, int) else lax.rem(qi + (d + n_t), n_t)
        return pl.ds(_aligned(chunk * tq, tq), tq)

    def scores(t, s_ref, m_ref):
        h, qi, cols = tile_ids(t)
        c_neg = t5_ref[first_bucket, h]
        c_step = t5_ref[last_bucket, h] - c_neg
        row0 = _aligned(qi * tq, tq)
        q = q_ref[pl.ds(row0, tq), cols] * scale
        zero = jnp.zeros_like(q)
        far = jnp.where((lane == qi) | (lane == n_t + qi), c_step,
                        jnp.where(lane >= one_lane, c_neg, 0.0))
        far = jnp.where(lane < one_lane + 2, far, 0.0)
        far_hi = far.astype(BF16)
        far_lo = (far - far_hi.astype(F32)).astype(BF16)
        q_far = jnp.where((lane < n_t) | (lane == one_lane), far_hi, far_lo)
        lhs = jnp.concatenate(
            [jnp.concatenate([jnp.where(lo, q, zero), q_far], axis=1),
             jnp.concatenate([jnp.where(lo, zero, q), q_far], axis=1)], axis=0)
        m_run = None
        for pos, d in enumerate(chunk_order):
            rows = key_rows(qi, d)
            rhs = jnp.concatenate([k_ref[rows, cols], kfar_ref[rows, :]], axis=1)
            s = lax.dot_general(lhs, rhs, (((1,), (1,)), ((), ())), preferred_element_type=F32)
            if d in (-1, 0, 1):
                inside = (qi + d >= 0) & (qi + d < n_t)
                head = jnp.where(inside, h, DIFF_HEADS)
                corr = corr_ref[head, :, (d + 1) * tq:(d + 2) * tq]
                s = s + jnp.concatenate([corr, corr], axis=0)
            s_ref[:, pos * tq:(pos + 1) * tq] = s
            m_run = s if m_run is None else jnp.maximum(m_run, s)
        m_ref[...] = functools.reduce(
            jnp.maximum, [m_run[:, i * LANES:(i + 1) * LANES] for i in range(tq // LANES)])

    def attend(t, s_ref, m_ref, ol_ref):
        _, qi, cols = tile_ids(t)
        m = jnp.broadcast_to(jnp.max(m_ref[...], axis=-1, keepdims=True), (2 * tq, LANES))
        m = jnp.concatenate([m] * (tq // LANES), axis=1)
        ol = None
        for pos, d in enumerate(chunk_order):
            e = jnp.exp((s_ref[:, pos * tq:(pos + 1) * tq] - m).astype(BF16))
            v = v_ref[key_rows(qi, d), cols]
            part = jnp.dot(e, jnp.concatenate([v, jnp.ones_like(v)], axis=1),
                           preferred_element_type=F32)
            ol = part if ol is None else ol + part
        ol_ref[...] = ol

    def finish(t, ol_ref):
        _, qi, cols = tile_ids(t)
        o = (ol_ref[0:tq, 0:LANES] / ol_ref[0:tq, LANES:2 * LANES]
             - lam * (ol_ref[tq:2 * tq, 0:LANES] / ol_ref[tq:2 * tq, LANES:2 * LANES]))
        row0 = _aligned(qi * tq, tq)
        o_ref[pl.ds(row0, tq), cols] = (
            _rmsnorm_f32(o, g_ref[...]) * (1.0 - lam_init)).astype(BF16)

    n_tiles = DIFF_HEADS * n_t

    def step(u, slot):
        live = (lambda t: True) if not isinstance(u, int) else (lambda t: 0 <= t < n_tiles)
        if live(u):
            scores(u, s_refs[slot], m_refs[slot])
        if live(u - 3):
            finish(u - 3, ol_refs[(slot - 3) % n_buf])
        if live(u - 2):
            a = (slot - 2) % n_buf
            attend(u - 2, s_refs[a], m_refs[a], ol_refs[a])

    first, trips = 3, (n_tiles - 3) // n_buf
    assert trips >= 1
    for u in range(first):
        step(u, u % n_buf)

    def rotation(j, carry):
        u = first + n_buf * j
        for i in range(n_buf):
            step(u + i, (first + i) % n_buf)
        return carry

    lax.fori_loop(0, trips, rotation, 0)
    for u in range(first + n_buf * trips, n_tiles + 3):
        step(u, u % n_buf)


def _diff_attention(qkv3, t5_bias, lam_params, subln_g, lam_init):
    B, S, _ = qkv3.shape
    tq = TQ_DIFF
    bucket_steps = _t5_bucket_steps(S)
    assert bucket_steps[2] <= tq and S % tq == 0 and tq % LANES == 0
    assert 2 * DIFF_QK_DIM == LANES and DIFF_V_DIM == LANES
    assert NA_WIDTH == DIFF_WIDTH
    q_col = 3
    n_buf = 3
    return pl.pallas_call(
        functools.partial(_diff_kernel, tq=tq, n_buf=n_buf, lam_init=lam_init,
                          bucket_steps=bucket_steps),
        out_shape=jax.ShapeDtypeStruct((B, S, DIFF_WIDTH), BF16),
        grid=(B,),
        in_specs=[
            pl.BlockSpec(memory_space=pltpu.SMEM),
            pl.BlockSpec((None, S, DIFF_WIDTH), lambda b: (b, 0, q_col)),
            pl.BlockSpec((None, S, DIFF_WIDTH), lambda b: (b, 0, q_col + 1)),
            pl.BlockSpec((None, S, DIFF_WIDTH), lambda b: (b, 0, q_col + 2)),
            pl.BlockSpec((4, DIFF_QK_DIM), lambda b: (0, 0)),
            pl.BlockSpec((1, DIFF_V_DIM), lambda b: (0, 0)),
        ],
        out_specs=pl.BlockSpec((None, S, DIFF_WIDTH), lambda b: (b, 0, 0)),
        scratch_shapes=(
            [pltpu.VMEM((2 * tq, S), F32)] * n_buf
            + [pltpu.VMEM((2 * tq, LANES), F32)] * n_buf
            + [pltpu.VMEM((2 * tq, 2 * LANES), F32)] * n_buf
            + [pltpu.VMEM((DIFF_HEADS + 1, tq, 3 * tq), F32),
               pltpu.VMEM((S, LANES), BF16)]),
        compiler_params=pltpu.CompilerParams(
            dimension_semantics=("arbitrary",),
            vmem_limit_bytes=_vmem_limit(48 * 2**20)),
        name="diff_attn",
    )(t5_bias, qkv3, qkv3, qkv3, lam_params, subln_g.reshape(1, DIFF_V_DIM))


def _mix_mlp_kernel(x_ref, yna_ref, ydf_ref, gate_ref, wna_ref, wdf_ref, wout_ref, g_ref, w1_ref,
                    w2_ref, fg_ref, o_ref, *, tf, final):
    d = x_ref.shape[1]
    b_na = jnp.dot(yna_ref[...], wna_ref[...], preferred_element_type=F32)
    b_df = jnp.dot(ydf_ref[...], wdf_ref[...], preferred_element_type=F32)
    merged = (jax.nn.sigmoid(gate_ref[:, 0:d]) * b_na
              + jax.nn.sigmoid(gate_ref[:, d:2 * d]) * b_df)
    x = x_ref[...] + jnp.dot(merged.astype(BF16), wout_ref[...], preferred_element_type=F32)
    h = _rmsnorm_f32(x, g_ref[...]).astype(BF16)
    for c in range(w1_ref.shape[1] // tf):
        u = jnp.dot(h, w1_ref[:, c * tf:(c + 1) * tf], preferred_element_type=F32)
        u = jnp.square(jnp.maximum(u, 0.0)).astype(BF16)
        x = x + jnp.dot(u, w2_ref[c * tf:(c + 1) * tf, :], preferred_element_type=F32)
    o_ref[...] = _rmsnorm_f32(x, fg_ref[...]) if final else x


def _mix_mlp(xt, y_na, y_df, gates, w_na, w_df, w_out, g, w1, w2, final_g, layer, final):
    T, D = xt.shape
    F = w1.shape[2]
    tm, tf = TM_MLP, TF_MLP
    once = pl.Buffered(1)
    assert F % tf == 0
    return pl.pallas_call(
        functools.partial(_mix_mlp_kernel, tf=tf, final=final),
        out_shape=jax.ShapeDtypeStruct((T, D), F32),
        grid=(T // tm,),
        in_specs=[
            pl.BlockSpec((tm, D), lambda i: (i, 0)),
            pl.BlockSpec((tm, NA_WIDTH), lambda i: (i, 0)),
            pl.BlockSpec((tm, DIFF_WIDTH), lambda i: (i, 0)),
            pl.BlockSpec((tm, 2 * D), lambda i: (i, 0)),
            pl.BlockSpec((None, NA_WIDTH, D), lambda i: (layer, 0, 0), pipeline_mode=once),
            pl.BlockSpec((None, DIFF_WIDTH, D), lambda i: (layer, 0, 0), pipeline_mode=once),
            pl.BlockSpec((None, D, D), lambda i: (layer, 0, 0), pipeline_mode=once),
            pl.BlockSpec((1, D), lambda i: (0, 0)),
            pl.BlockSpec((None, D, F), lambda i: (layer, 0, 0), pipeline_mode=once),
            pl.BlockSpec((None, F, D), lambda i: (layer, 0, 0), pipeline_mode=once),
            pl.BlockSpec((1, D), lambda i: (0, 0)),
        ],
        out_specs=pl.BlockSpec((tm, D), lambda i: (i, 0)),
        compiler_params=pltpu.CompilerParams(
            dimension_semantics=("arbitrary",),
            vmem_limit_bytes=_vmem_limit(56 * 2**20)),
        name="mix_mlp",
    )(xt, y_na, y_df, gates, w_na, w_df, w_out, g.reshape(1, D), w1, w2, final_g.reshape(1, D))


def kernel(x, t5_bias, final_norm_g, norm1_g, w_in, na_rpb, diff_lambda, diff_subln_g, w_na_o,
           w_diff_o, w_out, norm2_g, w_ff1, w_ff2):
    B, S, D = x.shape
    depth = w_in.shape[0]
    T = B * S
    assert w_in.shape[2] == QKV_WIDTH + 2 * D and S % GRID_W == 0
    xt = x.reshape(T, D)
    na_bias = _na_bias_table(na_rpb, S // GRID_W)
    w_in, w_na_o, w_diff_o, w_out, w_ff1, w_ff2 = (
        w.astype(BF16) for w in (w_in, w_na_o, w_diff_o, w_out, w_ff1, w_ff2))
    for layer in range(depth):
        qkv, gates = _in_proj(xt, norm1_g[layer], w_in, layer)
        qkv3 = qkv.reshape(B, S, QKV_WIDTH)
        y_na = _na_attention(qkv3, na_bias, layer)
        lam_init = 0.8 - 0.6 * math.exp(-0.3 * layer)
        y_df = _diff_attention(qkv3, t5_bias, diff_lambda[layer], diff_subln_g[layer], lam_init)
        xt = _mix_mlp(xt, y_na.reshape(T, NA_WIDTH), y_df.reshape(T, DIFF_WIDTH), gates,
                      w_na_o, w_diff_o, w_out, norm2_g[layer], w_ff1, w_ff2, final_norm_g, layer,
                      final=(layer == depth - 1))
    return xt.reshape(B, S, D)
```

```python
import functools
import math

import numpy as np
import jax
import jax.numpy as jnp
from jax import lax
from jax.experimental import pallas as pl
from jax.experimental.pallas import tpu as pltpu

GRID_W = 64
NA_HEADS = 8
NA_HEAD_DIM = 64
NA_WIN_ROWS = 8
NA_WIN_COLS = 16
DIFF_HEADS = 4
DIFF_QK_DIM = 64
DIFF_V_DIM = 2 * DIFF_QK_DIM
T5_BUCKETS = 32
T5_MAX_DIST = 128
RMS_EPS = 1e-6
NEG_INF = -1e30

NA_WIDTH = NA_HEADS * NA_HEAD_DIM
DIFF_WIDTH = DIFF_HEADS * DIFF_V_DIM
QKV_WIDTH = 3 * NA_WIDTH + 3 * DIFF_WIDTH

LANES = 128
BF16_SUBLANES = 16
V7X_VMEM_BYTES = 64 * 2**20

TM_PROJ = 512
TM_MLP = 512
TF_MLP = 1024
NA_ROW_BLOCK = 4
NA_WIN_BLOCKS = 3
NA_BATCH_BLOCK = 4
TQ_DIFF = 256

BF16 = jnp.bfloat16
F32 = jnp.float32


def _vmem_limit(nbytes):
    assert nbytes < V7X_VMEM_BYTES
    return int(nbytes)


def _aligned(index, multiple):
    return index if isinstance(index, int) else pl.multiple_of(index, multiple)


def _rmsnorm_f32(x, g):
    return (x * lax.rsqrt(jnp.mean(x * x, axis=-1, keepdims=True) + RMS_EPS)) * g


def _in_proj_kernel(x_ref, g_ref, w_ref, qkv_ref, gate_ref, *, chunk):
    hb = _rmsnorm_f32(x_ref[...], g_ref[...]).astype(BF16)
    n_qkv = qkv_ref.shape[1]
    n_gate = gate_ref.shape[1]
    for c in range(n_qkv // chunk):
        cs = slice(c * chunk, (c + 1) * chunk)
        qkv_ref[:, cs] = jnp.dot(hb, w_ref[:, cs], preferred_element_type=F32).astype(BF16)
    for c in range(n_gate // chunk):
        cs = slice(c * chunk, (c + 1) * chunk)
        ws = slice(n_qkv + c * chunk, n_qkv + (c + 1) * chunk)
        gate_ref[:, cs] = jnp.dot(hb, w_ref[:, ws], preferred_element_type=F32)


def _in_proj(xt, g, w):
    T, D = xt.shape
    n_all = w.shape[1]
    n_gate = n_all - QKV_WIDTH
    tm = TM_PROJ
    return pl.pallas_call(
        functools.partial(_in_proj_kernel, chunk=1024),
        out_shape=(jax.ShapeDtypeStruct((T, QKV_WIDTH), BF16),
                   jax.ShapeDtypeStruct((T, n_gate), F32)),
        grid=(T // tm,),
        in_specs=[
            pl.BlockSpec((tm, D), lambda i: (i, 0)),
            pl.BlockSpec((1, D), lambda i: (0, 0)),
            pl.BlockSpec((D, n_all), lambda i: (0, 0), pipeline_mode=pl.Buffered(1)),
        ],
        out_specs=(pl.BlockSpec((tm, QKV_WIDTH), lambda i: (i, 0)),
                   pl.BlockSpec((tm, n_gate), lambda i: (i, 0))),
        compiler_params=pltpu.CompilerParams(
            dimension_semantics=("arbitrary",), vmem_limit_bytes=_vmem_limit(56 * 2**20)),
        name="in_proj",
    )(xt, g.reshape(1, D), w)


def _na_geometry(rows):
    rb_rows = NA_ROW_BLOCK
    n_rb = rows // rb_rows
    win_rows = NA_WIN_BLOCKS * rb_rows
    wr = min(NA_WIN_ROWS, rows)
    kb = np.clip(np.arange(n_rb) - 1, 0, n_rb - NA_WIN_BLOCKS)
    dr_idx = np.zeros((n_rb, rb_rows, win_rows), np.int32)
    valid = np.zeros((n_rb, rb_rows, win_rows), bool)
    for rb in range(n_rb):
        w0 = kb[rb] * rb_rows
        for ri in range(rb_rows):
            r = rb * rb_rows + ri
            r0 = min(max(r - wr // 2, 0), rows - wr)
            assert w0 <= r0 and r0 + wr <= w0 + win_rows
            for wj in range(win_rows):
                krow = w0 + wj
                valid[rb, ri, wj] = r0 <= krow < r0 + wr
                dr_idx[rb, ri, wj] = min(max(krow - r + NA_WIN_ROWS - 1, 0), 2 * NA_WIN_ROWS - 2)
    for rb in range(2, n_rb - 1):
        assert (valid[rb] == valid[1]).all() and (dr_idx[rb] == dr_idx[1]).all()
    classes = [0, 1, n_rb - 1]
    return n_rb, kb, dr_idx[classes], valid[classes]


def _na_bias_kernel(rpb_ref, o_ref, *, dr_idx, row_valid):
    assert LANES == 2 * GRID_W
    n_cls, rb_rows, win_rows = dr_idx.shape
    c = lax.broadcasted_iota(jnp.int32, (GRID_W, LANES), 0)
    lane = lax.broadcasted_iota(jnp.int32, (GRID_W, LANES), 1)
    left = lane < GRID_W
    kc = jnp.where(left, lane, lane - GRID_W)
    c0 = jnp.clip(c - NA_WIN_COLS // 2, 0, GRID_W - NA_WIN_COLS)
    col_ok = (kc >= c0) & (kc < c0 + NA_WIN_COLS)
    neg = jnp.full((GRID_W, LANES), NEG_INF, F32)

    def half_tile(d, right):
        w = jnp.broadcast_to(rpb_ref[d:d + 1, :], (GRID_W, LANES))
        t = pltpu.roll(w, GRID_W if right else 0, 1, stride=1, stride_axis=0)
        return jnp.where(col_ok, t, neg)

    for cls in range(n_cls):
        for ri in range(rb_rows):
            for p in range(win_rows // 2):
                halves = []
                for side in range(2):
                    wj = 2 * p + side
                    ok = bool(row_valid[cls, ri, wj])
                    halves.append(half_tile(int(dr_idx[cls, ri, wj]), side == 1) if ok else neg)
                o_ref[cls, ri * GRID_W:(ri + 1) * GRID_W, p * LANES:(p + 1) * LANES] = (
                    jnp.where(left, halves[0], halves[1]))


def _na_bias_table(na_rpb, rows):
    depth, H, n_dr, n_dc = na_rpb.shape
    assert n_dr == 2 * NA_WIN_ROWS - 1 and n_dc == 2 * NA_WIN_COLS - 1
    _, _, dr_idx, row_valid = _na_geometry(rows)
    n_cls, rb_rows, win_rows = dr_idx.shape
    rq, wk = rb_rows * GRID_W, win_rows * GRID_W
    lanes = jnp.pad(na_rpb, ((0, 0), (0, 0), (0, 0), (0, LANES - n_dc)), constant_values=NEG_INF)
    lanes = jnp.roll(lanes, -(NA_WIN_COLS - 1), axis=-1)
    return pl.pallas_call(
        functools.partial(_na_bias_kernel, dr_idx=dr_idx, row_valid=row_valid),
        out_shape=jax.ShapeDtypeStruct((depth, n_cls, H, rq, wk), F32),
        grid=(depth, H),
        in_specs=[pl.BlockSpec((None, None, n_dr, LANES), lambda l, h: (l, h, 0, 0))],
        out_specs=pl.BlockSpec((None, n_cls, None, rq, wk), lambda l, h: (l, 0, h, 0, 0)),
        compiler_params=pltpu.CompilerParams(dimension_semantics=("arbitrary", "arbitrary")),
        name="na_bias",
    )(lanes)


def _na_kernel(q_ref, k0_ref, k1_ref, k2_ref, v0_ref, v1_ref, v2_ref, bias_ref, *rest):
    n_cast = len(rest) // 2
    o_ref = rest[n_cast]
    for w_ref, wb_ref in zip(rest[:n_cast], rest[n_cast + 1:]):
        wb_ref[...] = w_ref[...].astype(BF16)
    nb, rq = q_ref.shape[0], q_ref.shape[1]
    scale = 1.0 / math.sqrt(NA_HEAD_DIM)
    lo = lax.broadcasted_iota(jnp.int32, (rq, LANES), 1) < NA_HEAD_DIM
    heads_per_vreg = LANES // NA_HEAD_DIM
    for bi, hp in np.ndindex(nb, NA_HEADS // heads_per_vreg):
        cs = (bi, slice(None), slice(hp * LANES, (hp + 1) * LANES))
        q = q_ref[cs] * scale
        k = jnp.concatenate([k0_ref[cs], k1_ref[cs], k2_ref[cs]], axis=0)
        v = jnp.concatenate([v0_ref[cs], v1_ref[cs], v2_ref[cs]], axis=0)
        v_ones = jnp.concatenate([v, jnp.ones_like(v)], axis=1)
        outs = []
        for e in range(heads_per_vreg):
            qm = jnp.where(lo if e == 0 else jnp.logical_not(lo), q, jnp.zeros_like(q))
            s = lax.dot_general(qm, k, (((1,), (1,)), ((), ())), preferred_element_type=F32)
            s = s + bias_ref[heads_per_vreg * hp + e]
            m = jnp.max(s, axis=-1, keepdims=True)
            p = jnp.exp((s - m).astype(BF16))
            ol = jnp.dot(p, v_ones, preferred_element_type=F32)
            outs.append(ol[:, 0:LANES] / ol[:, LANES:2 * LANES])
        o_ref[cs] = jnp.where(lo, outs[0], outs[1]).astype(BF16)


def _na_attention(qkv3, bias_table, layer, casts):
    B, S, _ = qkv3.shape
    rows = S // GRID_W
    n_rb, kb, _, _ = _na_geometry(rows)
    rq = NA_ROW_BLOCK * GRID_W
    assert NA_HEADS * NA_HEAD_DIM == NA_WIDTH and LANES % NA_HEAD_DIM == 0
    k_col, v_col = 1, 2
    n_kb = n_rb - NA_WIN_BLOCKS

    nb = NA_BATCH_BLOCK
    assert B % nb == 0

    def kv_spec(col, t):
        return pl.BlockSpec((nb, rq, NA_WIDTH),
                            lambda rb, b: (b, jnp.clip(rb - 1, 0, n_kb) + t, col))

    def bias_map(rb, b):
        cls = (rb > 0).astype(jnp.int32) + (rb == n_rb - 1).astype(jnp.int32)
        return (layer, cls, 0, 0, 0)

    n_bb = B // nb
    n_steps = n_rb * n_bb
    cast_in, cast_in_specs, cast_out, cast_out_specs = [], [], [], []
    for w, w_layer in casts:
        depth, R, C = w.shape
        slab = R // n_steps
        assert R % n_steps == 0 and slab % BF16_SUBLANES == 0
        cast_in.append(w.reshape(depth * n_steps, slab, C))
        first = w_layer * n_steps
        cast_in_specs.append(pl.BlockSpec(
            (None, slab, C), lambda rb, b, first=first: (first + rb * n_bb + b, 0, 0)))
        cast_out.append(jax.ShapeDtypeStruct((n_steps, slab, C), BF16))
        cast_out_specs.append(pl.BlockSpec((None, slab, C), lambda rb, b: (rb * n_bb + b, 0, 0)))

    wk = NA_WIN_BLOCKS * rq
    y, *cast = pl.pallas_call(
        _na_kernel,
        out_shape=[jax.ShapeDtypeStruct((B, S, NA_WIDTH), BF16)] + cast_out,
        grid=(n_rb, n_bb),
        in_specs=[pl.BlockSpec((nb, rq, NA_WIDTH), lambda rb, b: (b, rb, 0))]
        + [kv_spec(k_col, t) for t in range(NA_WIN_BLOCKS)]
        + [kv_spec(v_col, t) for t in range(NA_WIN_BLOCKS)]
        + [pl.BlockSpec((None, None, NA_HEADS, rq, wk), bias_map)]
        + cast_in_specs,
        out_specs=[pl.BlockSpec((nb, rq, NA_WIDTH), lambda rb, b: (b, rb, 0))] + cast_out_specs,
        compiler_params=pltpu.CompilerParams(
            dimension_semantics=("arbitrary", "arbitrary"),
            vmem_limit_bytes=_vmem_limit(56 * 2**20)),
        name="na_attn",
    )(qkv3, qkv3, qkv3, qkv3, qkv3, qkv3, qkv3, bias_table, *cast_in)
    return y, [c.reshape(w.shape[1], w.shape[2]) for c, (w, _) in zip(cast, casts)]


def _t5_bucket_steps(seq):
    half = T5_BUCKETS // 2
    max_exact = half // 2
    rel = np.arange(-(seq - 1), seq)
    n = np.abs(rel)
    nf = np.maximum(n, 1).astype(np.float64)
    large = max_exact + np.floor(
        np.log(nf / max_exact) / math.log(T5_MAX_DIST / max_exact) * (half - max_exact) + 1e-9
    ).astype(np.int64)
    large = np.minimum(large, half - 1)
    bucket = np.where(rel > 0, half, 0) + np.where(n < max_exact, n, large)
    steps = [(int(rel[i]), int(bucket[i])) for i in range(1, len(rel)) if bucket[i] != bucket[i - 1]]
    sat = max(abs(steps[0][0]) + 1, abs(steps[-1][0]))
    return int(bucket[0]), steps, sat


def _diff_kernel(t5_ref, q_ref, k_ref, v_ref, lam_ref, g_ref, o_ref, *scratch, tq, n_buf,
                 lam_init, bucket_steps):
    s_refs, m_refs, ol_refs = (scratch[i * n_buf:(i + 1) * n_buf] for i in range(3))
    corr_ref, kfar_ref = scratch[3 * n_buf:]
    b = pl.program_id(0)
    seq = k_ref.shape[0]
    n_t = seq // tq
    band = 3 * tq
    first_bucket, steps, _ = bucket_steps
    last_bucket = steps[-1][1]
    split = tq + tq // 2
    one_lane = 2 * n_t
    assert one_lane + 2 <= LANES and n_t >= 4 and n_t % 2 == 0

    @pl.when(b == 0)
    def _init():
        row = lax.broadcasted_iota(jnp.int32, (tq, band), 0)
        col = lax.broadcasted_iota(jnp.int32, (tq, band), 1)
        rel = col - tq - row
        for hh in range(DIFF_HEADS):
            val = jnp.full((tq, band), t5_ref[first_bucket, hh], F32)
            for thr, bkt in steps:
                val = jnp.where(rel >= thr, t5_ref[bkt, hh], val)
            far = jnp.where(col < split, t5_ref[first_bucket, hh], t5_ref[last_bucket, hh])
            corr_ref[hh] = val - far
        corr_ref[DIFF_HEADS] = jnp.zeros((tq, band), F32)
        key = lax.broadcasted_iota(jnp.int32, (seq, LANES), 0)
        lane = lax.broadcasted_iota(jnp.int32, (seq, LANES), 1)
        tile = jnp.where(lane < n_t, lane, lane - n_t)
        step = jnp.where(key >= tile * tq + tq // 2, 1.0, 0.0)
        ones = jnp.where(lane < one_lane + 2, 1.0, 0.0)
        kfar_ref[...] = jnp.where(lane < one_lane, step, ones).astype(BF16)

    scale = 1.0 / math.sqrt(DIFF_QK_DIM)
    lane = lax.broadcasted_iota(jnp.int32, (tq, LANES), 1)
    lo = lane < DIFF_QK_DIM
    lp = lam_ref[...]
    lam = (jnp.exp(jnp.sum(lp[0:1] * lp[1:2], axis=-1, keepdims=True))
           - jnp.exp(jnp.sum(lp[2:3] * lp[3:4], axis=-1, keepdims=True)) + lam_init)

    chunk_order = [-1, 0, 1] + list(range(2, n_t - 1))

    def tile_ids(t):
        if isinstance(t, int):
            h, qi = divmod(t, n_t)
        else:
            h, qi = lax.div(t, n_t), lax.rem(t, n_t)
        return h, qi, pl.ds(_aligned(h * LANES, LANES), LANES)

    def key_rows(qi, d):
        chunk = (qi + d) % n_t if isinstance(qi, int) else lax.rem(qi + (d + n_t), n_t)
        return pl.ds(_aligned(chunk * tq, tq), tq)

    def scores(t, s_ref, m_ref):
        h, qi, cols = tile_ids(t)
        c_neg = t5_ref[first_bucket, h]
        c_step = t5_ref[last_bucket, h] - c_neg
        row0 = _aligned(qi * tq, tq)
        q = q_ref[pl.ds(row0, tq), cols] * scale
        zero = jnp.zeros_like(q)
        far = jnp.where((lane == qi) | (lane == n_t + qi), c_step,
                        jnp.where(lane >= one_lane, c_neg, 0.0))
        far = jnp.where(lane < one_lane + 2, far, 0.0)
        far_hi = far.astype(BF16)
        far_lo = (far - far_hi.astype(F32)).astype(BF16)
        q_far = jnp.where((lane < n_t) | (lane == one_lane), far_hi, far_lo)
        lhs = jnp.concatenate(
            [jnp.concatenate([jnp.where(lo, q, zero), q_far], axis=1),
             jnp.concatenate([jnp.where(lo, zero, q), q_far], axis=1)], axis=0)
        m_run = None
        for pos, d in enumerate(chunk_order):
            rows = key_rows(qi, d)
            rhs = jnp.concatenate([k_ref[rows, cols], kfar_ref[rows, :]], axis=1)
            s = lax.dot_general(lhs, rhs, (((1,), (1,)), ((), ())), preferred_element_type=F32)
            if d in (-1, 0, 1):
                inside = (qi + d >= 0) & (qi + d < n_t)
                head = jnp.where(inside, h, DIFF_HEADS)
                corr = corr_ref[head, :, (d + 1) * tq:(d + 2) * tq]
                s = s + jnp.concatenate([corr, corr], axis=0)
            s_ref[:, pos * tq:(pos + 1) * tq] = s
            m_run = s if m_run is None else jnp.maximum(m_run, s)
        m_ref[...] = functools.reduce(
            jnp.maximum, [m_run[:, i * LANES:(i + 1) * LANES] for i in range(tq // LANES)])

    def attend(t, s_ref, m_ref, ol_ref):
        _, qi, cols = tile_ids(t)
        m = jnp.broadcast_to(jnp.max(m_ref[...], axis=-1, keepdims=True), (2 * tq, LANES))
        m = jnp.concatenate([m] * (tq // LANES), axis=1)
        ol = None
        for pos, d in enumerate(chunk_order):
            e = jnp.exp((s_ref[:, pos * tq:(pos + 1) * tq] - m).astype(BF16))
            v = v_ref[key_rows(qi, d), cols]
            part = jnp.dot(e, jnp.concatenate([v, jnp.ones_like(v)], axis=1),
                           preferred_element_type=F32)
            ol = part if ol is None else ol + part
        ol_ref[...] = ol

    def finish(t, ol_ref):
        _, qi, cols = tile_ids(t)
        o = (ol_ref[0:tq, 0:LANES] / ol_ref[0:tq, LANES:2 * LANES]
             - lam * (ol_ref[tq:2 * tq, 0:LANES] / ol_ref[tq:2 * tq, LANES:2 * LANES]))
        row0 = _aligned(qi * tq, tq)
        o_ref[pl.ds(row0, tq), cols] = (
            _rmsnorm_f32(o, g_ref[...]) * (1.0 - lam_init)).astype(BF16)

    n_tiles = DIFF_HEADS * n_t

    def step(u, slot):
        live = (lambda t: True) if not isinstance(u, int) else (lambda t: 0 <= t < n_tiles)
        if live(u):
            scores(u, s_refs[slot], m_refs[slot])
        if live(u - 3):
            finish(u - 3, ol_refs[(slot - 3) % n_buf])
        if live(u - 2):
            a = (slot - 2) % n_buf
            attend(u - 2, s_refs[a], m_refs[a], ol_refs[a])

    first, trips = 3, (n_tiles - 3) // n_buf
    assert trips >= 1
    for u in range(first):
        step(u, u % n_buf)

    def rotation(j, carry):
        u = first + n_buf * j
        for i in range(n_buf):
            step(u + i, (first + i) % n_buf)
        return carry

    lax.fori_loop(0, trips, rotation, 0)
    for u in range(first + n_buf * trips, n_tiles + 3):
        step(u, u % n_buf)


def _diff_attention(qkv3, t5_bias, lam_params, subln_g, lam_init):
    B, S, _ = qkv3.shape
    tq = TQ_DIFF
    bucket_steps = _t5_bucket_steps(S)
    assert bucket_steps[2] <= tq and S % tq == 0 and tq % LANES == 0
    assert 2 * DIFF_QK_DIM == LANES and DIFF_V_DIM == LANES
    assert NA_WIDTH == DIFF_WIDTH
    q_col = 3
    n_buf = 3
    return pl.pallas_call(
        functools.partial(_diff_kernel, tq=tq, n_buf=n_buf, lam_init=lam_init,
                          bucket_steps=bucket_steps),
        out_shape=jax.ShapeDtypeStruct((B, S, DIFF_WIDTH), BF16),
        grid=(B,),
        in_specs=[
            pl.BlockSpec(memory_space=pltpu.SMEM),
            pl.BlockSpec((None, S, DIFF_WIDTH), lambda b: (b, 0, q_col)),
            pl.BlockSpec((None, S, DIFF_WIDTH), lambda b: (b, 0, q_col + 1)),
            pl.BlockSpec((None, S, DIFF_WIDTH), lambda b: (b, 0, q_col + 2)),
            pl.BlockSpec((4, DIFF_QK_DIM), lambda b: (0, 0)),
            pl.BlockSpec((1, DIFF_V_DIM), lambda b: (0, 0)),
        ],
        out_specs=pl.BlockSpec((None, S, DIFF_WIDTH), lambda b: (b, 0, 0)),
        scratch_shapes=(
            [pltpu.VMEM((2 * tq, S), F32)] * n_buf
            + [pltpu.VMEM((2 * tq, LANES), F32)] * n_buf
            + [pltpu.VMEM((2 * tq, 2 * LANES), F32)] * n_buf
            + [pltpu.VMEM((DIFF_HEADS + 1, tq, 3 * tq), F32),
               pltpu.VMEM((S, LANES), BF16)]),
        compiler_params=pltpu.CompilerParams(
            dimension_semantics=("arbitrary",),
            vmem_limit_bytes=_vmem_limit(48 * 2**20)),
        name="diff_attn",
    )(t5_bias, qkv3, qkv3, qkv3, lam_params, subln_g.reshape(1, DIFF_V_DIM))


def _mix_mlp_kernel(x_ref, yna_ref, ydf_ref, gate_ref, wna_ref, wdf_ref, wout_ref, g_ref, w1_ref,
                    w2_ref, fg_ref, o_ref, *, tf, final):
    d = x_ref.shape[1]
    b_na = jnp.dot(yna_ref[...], wna_ref[...], preferred_element_type=F32)
    b_df = jnp.dot(ydf_ref[...], wdf_ref[...], preferred_element_type=F32)
    merged = (jax.nn.sigmoid(gate_ref[:, 0:d]) * b_na
              + jax.nn.sigmoid(gate_ref[:, d:2 * d]) * b_df)
    x = x_ref[...] + jnp.dot(merged.astype(BF16), wout_ref[...], preferred_element_type=F32)
    h = _rmsnorm_f32(x, g_ref[...]).astype(BF16)
    for c in range(w1_ref.shape[1] // tf):
        u = jnp.dot(h, w1_ref[:, c * tf:(c + 1) * tf], preferred_element_type=F32)
        u = jnp.square(jnp.maximum(u, 0.0)).astype(BF16)
        x = x + jnp.dot(u, w2_ref[c * tf:(c + 1) * tf, :], preferred_element_type=F32)
    o_ref[...] = _rmsnorm_f32(x, fg_ref[...]) if final else x


def _mix_mlp(xt, y_na, y_df, gates, w_na, w_df, w_out, g, w1, w2, final_g, final):
    T, D = xt.shape
    F = w1.shape[1]
    tm, tf = TM_MLP, TF_MLP
    once = pl.Buffered(1)
    assert F % tf == 0
    return pl.pallas_call(
        functools.partial(_mix_mlp_kernel, tf=tf, final=final),
        out_shape=jax.ShapeDtypeStruct((T, D), F32),
        grid=(T // tm,),
        in_specs=[
            pl.BlockSpec((tm, D), lambda i: (i, 0)),
            pl.BlockSpec((tm, NA_WIDTH), lambda i: (i, 0)),
            pl.BlockSpec((tm, DIFF_WIDTH), lambda i: (i, 0)),
            pl.BlockSpec((tm, 2 * D), lambda i: (i, 0)),
            pl.BlockSpec((NA_WIDTH, D), lambda i: (0, 0), pipeline_mode=once),
            pl.BlockSpec((DIFF_WIDTH, D), lambda i: (0, 0), pipeline_mode=once),
            pl.BlockSpec((D, D), lambda i: (0, 0), pipeline_mode=once),
            pl.BlockSpec((1, D), lambda i: (0, 0)),
            pl.BlockSpec((D, F), lambda i: (0, 0), pipeline_mode=once),
            pl.BlockSpec((F, D), lambda i: (0, 0), pipeline_mode=once),
            pl.BlockSpec((1, D), lambda i: (0, 0)),
        ],
        out_specs=pl.BlockSpec((tm, D), lambda i: (i, 0)),
        compiler_params=pltpu.CompilerParams(
            dimension_semantics=("arbitrary",),
            vmem_limit_bytes=_vmem_limit(56 * 2**20)),
        name="mix_mlp",
    )(xt, y_na, y_df, gates, w_na, w_df, w_out, g.reshape(1, D), w1, w2, final_g.reshape(1, D))


def kernel(x, t5_bias, final_norm_g, norm1_g, w_in, na_rpb, diff_lambda, diff_subln_g, w_na_o,
           w_diff_o, w_out, norm2_g, w_ff1, w_ff2):
    B, S, D = x.shape
    depth = w_in.shape[0]
    T = B * S
    assert w_in.shape[2] == QKV_WIDTH + 2 * D and S % GRID_W == 0
    xt = x.reshape(T, D)
    na_bias = _na_bias_table(na_rpb, S // GRID_W)
    w_in_b = w_in[0].astype(BF16)
    for layer in range(depth):
        qkv, gates = _in_proj(xt, norm1_g[layer], w_in_b)
        qkv3 = qkv.reshape(B, S, QKV_WIDTH)
        casts = [(w, layer) for w in (w_na_o, w_diff_o, w_out, w_ff1, w_ff2)]
        if layer + 1 < depth:
            casts.append((w_in, layer + 1))
        y_na, (w_na_b, w_df_b, w_out_b, w1_b, w2_b, *w_in_next) = _na_attention(
            qkv3, na_bias, layer, casts)
        lam_init = 0.8 - 0.6 * math.exp(-0.3 * layer)
        y_df = _diff_attention(qkv3, t5_bias, diff_lambda[layer], diff_subln_g[layer], lam_init)
        xt = _mix_mlp(xt, y_na.reshape(T, NA_WIDTH), y_df.reshape(T, DIFF_WIDTH), gates,
                      w_na_b, w_df_b, w_out_b, norm2_g[layer], w1_b, w2_b, final_norm_g,
                      final=(layer == depth - 1))
        if w_in_next:
            w_in_b = w_in_next[0]
    return xt.reshape(B, S, D)
```

```python
import functools
import math

import numpy as np
import jax
import jax.numpy as jnp
from jax import lax
from jax.experimental import pallas as pl
from jax.experimental.pallas import tpu as pltpu

GRID_W = 64
NA_HEADS = 8
NA_HEAD_DIM = 64
NA_WIN_ROWS = 8
NA_WIN_COLS = 16
DIFF_HEADS = 4
DIFF_QK_DIM = 64
DIFF_V_DIM = 2 * DIFF_QK_DIM
T5_BUCKETS = 32
T5_MAX_DIST = 128
RMS_EPS = 1e-6
NEG_INF = -1e30

NA_WIDTH = NA_HEADS * NA_HEAD_DIM
DIFF_WIDTH = DIFF_HEADS * DIFF_V_DIM
QKV_WIDTH = 3 * NA_WIDTH + 3 * DIFF_WIDTH

LANES = 128
BF16_SUBLANES = 16
V7X_VMEM_BYTES = 64 * 2**20
VMEM_TEMP_BYTES = 12 * 2**20

TM_PROJ = 512
TN_PROJ = 1024
TM_MLP = 512
TF_MLP = 1024
NA_ROW_BLOCK = 4
NA_WIN_BLOCKS = 3
NA_BATCH_BLOCK = 4
TQ_DIFF = 256

BF16 = jnp.bfloat16
F32 = jnp.float32


def _nbytes(shape, dtype, buffers=1):
    return buffers * math.prod(shape) * jnp.dtype(dtype).itemsize


def _vmem_limit(*window_bytes):
    total = sum(window_bytes) + VMEM_TEMP_BYTES
    assert total < V7X_VMEM_BYTES
    return int(total)


def _aligned(index, multiple):
    return index if isinstance(index, int) else pl.multiple_of(index, multiple)


def _rmsnorm_f32(x, g):
    return (x * lax.rsqrt(jnp.mean(x * x, axis=-1, keepdims=True) + RMS_EPS)) * g


def _in_proj_kernel(x_ref, g_ref, w_ref, qkv_ref, gate_ref, *, chunk):
    hb = _rmsnorm_f32(x_ref[...], g_ref[...]).astype(BF16)
    n_qkv = qkv_ref.shape[1]
    n_gate = gate_ref.shape[1]
    for c in range(n_qkv // chunk):
        cs = slice(c * chunk, (c + 1) * chunk)
        qkv_ref[:, cs] = jnp.dot(hb, w_ref[:, cs], preferred_element_type=F32).astype(BF16)
    for c in range(n_gate // chunk):
        cs = slice(c * chunk, (c + 1) * chunk)
        ws = slice(n_qkv + c * chunk, n_qkv + (c + 1) * chunk)
        gate_ref[:, cs] = jnp.dot(hb, w_ref[:, ws], preferred_element_type=F32)


def _in_proj(xt, g, w):
    T, D = xt.shape
    n_all = w.shape[1]
    n_gate = n_all - QKV_WIDTH
    tm = TM_PROJ
    vmem = _vmem_limit(_nbytes((tm, D), F32, 2), _nbytes((D, n_all), BF16),
                       _nbytes((tm, QKV_WIDTH), BF16, 2), _nbytes((tm, n_gate), F32, 2))
    return pl.pallas_call(
        functools.partial(_in_proj_kernel, chunk=TN_PROJ),
        out_shape=(jax.ShapeDtypeStruct((T, QKV_WIDTH), BF16),
                   jax.ShapeDtypeStruct((T, n_gate), F32)),
        grid=(T // tm,),
        in_specs=[
            pl.BlockSpec((tm, D), lambda i: (i, 0)),
            pl.BlockSpec((1, D), lambda i: (0, 0)),
            pl.BlockSpec((D, n_all), lambda i: (0, 0), pipeline_mode=pl.Buffered(1)),
        ],
        out_specs=(pl.BlockSpec((tm, QKV_WIDTH), lambda i: (i, 0)),
                   pl.BlockSpec((tm, n_gate), lambda i: (i, 0))),
        compiler_params=pltpu.CompilerParams(
            dimension_semantics=("arbitrary",), vmem_limit_bytes=vmem),
        name="in_proj",
    )(xt, g.reshape(1, D), w)


def _na_geometry(rows):
    rb_rows = NA_ROW_BLOCK
    n_rb = rows // rb_rows
    win_rows = NA_WIN_BLOCKS * rb_rows
    wr = min(NA_WIN_ROWS, rows)
    kb = np.clip(np.arange(n_rb) - 1, 0, n_rb - NA_WIN_BLOCKS)
    dr_idx = np.zeros((n_rb, rb_rows, win_rows), np.int32)
    valid = np.zeros((n_rb, rb_rows, win_rows), bool)
    for rb in range(n_rb):
        w0 = kb[rb] * rb_rows
        for ri in range(rb_rows):
            r = rb * rb_rows + ri
            r0 = min(max(r - wr // 2, 0), rows - wr)
            assert w0 <= r0 and r0 + wr <= w0 + win_rows
            for wj in range(win_rows):
                krow = w0 + wj
                valid[rb, ri, wj] = r0 <= krow < r0 + wr
                dr_idx[rb, ri, wj] = min(max(krow - r + NA_WIN_ROWS - 1, 0), 2 * NA_WIN_ROWS - 2)
    for rb in range(2, n_rb - 1):
        assert (valid[rb] == valid[1]).all() and (dr_idx[rb] == dr_idx[1]).all()
    classes = [0, 1, n_rb - 1]
    return n_rb, kb, dr_idx[classes], valid[classes]


def _na_bias_kernel(rpb_ref, o_ref, *, dr_idx, row_valid):
    assert LANES == 2 * GRID_W
    n_cls, rb_rows, win_rows = dr_idx.shape
    c = lax.broadcasted_iota(jnp.int32, (GRID_W, LANES), 0)
    lane = lax.broadcasted_iota(jnp.int32, (GRID_W, LANES), 1)
    left = lane < GRID_W
    kc = jnp.where(left, lane, lane - GRID_W)
    c0 = jnp.clip(c - NA_WIN_COLS // 2, 0, GRID_W - NA_WIN_COLS)
    col_ok = (kc >= c0) & (kc < c0 + NA_WIN_COLS)
    neg = jnp.full((GRID_W, LANES), NEG_INF, F32)

    def half_tile(d, right):
        w = jnp.broadcast_to(rpb_ref[d:d + 1, :], (GRID_W, LANES))
        t = pltpu.roll(w, GRID_W if right else 0, 1, stride=1, stride_axis=0)
        return jnp.where(col_ok, t, neg)

    for cls in range(n_cls):
        for ri in range(rb_rows):
            for p in range(win_rows // 2):
                halves = []
                for side in range(2):
                    wj = 2 * p + side
                    ok = bool(row_valid[cls, ri, wj])
                    halves.append(half_tile(int(dr_idx[cls, ri, wj]), side == 1) if ok else neg)
                o_ref[cls, ri * GRID_W:(ri + 1) * GRID_W, p * LANES:(p + 1) * LANES] = (
                    jnp.where(left, halves[0], halves[1]))


def _na_bias_table(na_rpb, rows):
    depth, H, n_dr, n_dc = na_rpb.shape
    assert n_dr == 2 * NA_WIN_ROWS - 1 and n_dc == 2 * NA_WIN_COLS - 1
    _, _, dr_idx, row_valid = _na_geometry(rows)
    n_cls, rb_rows, win_rows = dr_idx.shape
    rq, wk = rb_rows * GRID_W, win_rows * GRID_W
    lanes = jnp.pad(na_rpb, ((0, 0), (0, 0), (0, 0), (0, LANES - n_dc)), constant_values=NEG_INF)
    lanes = jnp.roll(lanes, -(NA_WIN_COLS - 1), axis=-1)
    return pl.pallas_call(
        functools.partial(_na_bias_kernel, dr_idx=dr_idx, row_valid=row_valid),
        out_shape=jax.ShapeDtypeStruct((depth, n_cls, H, rq, wk), F32),
        grid=(depth, H),
        in_specs=[pl.BlockSpec((None, None, n_dr, LANES), lambda l, h: (l, h, 0, 0))],
        out_specs=pl.BlockSpec((None, n_cls, None, rq, wk), lambda l, h: (l, 0, h, 0, 0)),
        compiler_params=pltpu.CompilerParams(dimension_semantics=("arbitrary", "arbitrary")),
        name="na_bias",
    )(lanes)


def _na_kernel(q_ref, k0_ref, k1_ref, k2_ref, v0_ref, v1_ref, v2_ref, bias_ref, *rest):
    n_cast = len(rest) // 2
    o_ref = rest[n_cast]
    for w_ref, wb_ref in zip(rest[:n_cast], rest[n_cast + 1:]):
        wb_ref[...] = w_ref[...].astype(BF16)
    nb, rq = q_ref.shape[0], q_ref.shape[1]
    scale = 1.0 / math.sqrt(NA_HEAD_DIM)
    lo = lax.broadcasted_iota(jnp.int32, (rq, LANES), 1) < NA_HEAD_DIM
    heads_per_vreg = LANES // NA_HEAD_DIM
    for bi, hp in np.ndindex(nb, NA_HEADS // heads_per_vreg):
        cs = (bi, slice(None), slice(hp * LANES, (hp + 1) * LANES))
        q = q_ref[cs] * scale
        k = jnp.concatenate([k0_ref[cs], k1_ref[cs], k2_ref[cs]], axis=0)
        v = jnp.concatenate([v0_ref[cs], v1_ref[cs], v2_ref[cs]], axis=0)
        v_ones = jnp.concatenate([v, jnp.ones_like(v)], axis=1)
        outs = []
        for e in range(heads_per_vreg):
            qm = jnp.where(lo if e == 0 else jnp.logical_not(lo), q, jnp.zeros_like(q))
            s = lax.dot_general(qm, k, (((1,), (1,)), ((), ())), preferred_element_type=F32)
            s = s + bias_ref[heads_per_vreg * hp + e]
            m = jnp.max(s, axis=-1, keepdims=True)
            p = jnp.exp((s - m).astype(BF16))
            ol = jnp.dot(p, v_ones, preferred_element_type=F32)
            outs.append(ol[:, 0:LANES] / ol[:, LANES:2 * LANES])
        o_ref[cs] = jnp.where(lo, outs[0], outs[1]).astype(BF16)


def _na_attention(qkv3, bias_table, layer, casts):
    B, S, _ = qkv3.shape
    rows = S // GRID_W
    n_rb, kb, _, _ = _na_geometry(rows)
    rq = NA_ROW_BLOCK * GRID_W
    assert NA_HEADS * NA_HEAD_DIM == NA_WIDTH and LANES % NA_HEAD_DIM == 0
    k_col, v_col = 1, 2
    n_kb = n_rb - NA_WIN_BLOCKS

    nb = NA_BATCH_BLOCK
    assert B % nb == 0

    def kv_spec(col, t):
        return pl.BlockSpec((nb, rq, NA_WIDTH),
                            lambda rb, b: (b, jnp.clip(rb - 1, 0, n_kb) + t, col))

    def bias_map(rb, b):
        cls = (rb > 0).astype(jnp.int32) + (rb == n_rb - 1).astype(jnp.int32)
        return (layer, cls, 0, 0, 0)

    n_bb = B // nb
    n_steps = n_rb * n_bb
    cast_in, cast_in_specs, cast_out, cast_out_specs = [], [], [], []
    for w, w_layer in casts:
        depth, R, C = w.shape
        slab = R // n_steps
        assert R % n_steps == 0 and slab % BF16_SUBLANES == 0
        cast_in.append(w.reshape(depth * n_steps, slab, C))
        first = w_layer * n_steps
        cast_in_specs.append(pl.BlockSpec(
            (None, slab, C), lambda rb, b, first=first: (first + rb * n_bb + b, 0, 0)))
        cast_out.append(jax.ShapeDtypeStruct((n_steps, slab, C), BF16))
        cast_out_specs.append(pl.BlockSpec((None, slab, C), lambda rb, b: (rb * n_bb + b, 0, 0)))

    wk = NA_WIN_BLOCKS * rq
    n_qkv_windows = 2 + 2 * NA_WIN_BLOCKS
    vmem = _vmem_limit(
        n_qkv_windows * _nbytes((nb, rq, NA_WIDTH), BF16, 2),
        _nbytes((NA_HEADS, rq, wk), F32, 2),
        *[_nbytes(o.shape[1:], F32, 2) + _nbytes(o.shape[1:], BF16, 2) for o in cast_out])
    y, *cast = pl.pallas_call(
        _na_kernel,
        out_shape=[jax.ShapeDtypeStruct((B, S, NA_WIDTH), BF16)] + cast_out,
        grid=(n_rb, n_bb),
        in_specs=[pl.BlockSpec((nb, rq, NA_WIDTH), lambda rb, b: (b, rb, 0))]
        + [kv_spec(k_col, t) for t in range(NA_WIN_BLOCKS)]
        + [kv_spec(v_col, t) for t in range(NA_WIN_BLOCKS)]
        + [pl.BlockSpec((None, None, NA_HEADS, rq, wk), bias_map)]
        + cast_in_specs,
        out_specs=[pl.BlockSpec((nb, rq, NA_WIDTH), lambda rb, b: (b, rb, 0))] + cast_out_specs,
        compiler_params=pltpu.CompilerParams(
            dimension_semantics=("arbitrary", "arbitrary"), vmem_limit_bytes=vmem),
        name="na_attn",
    )(qkv3, qkv3, qkv3, qkv3, qkv3, qkv3, qkv3, bias_table, *cast_in)
    return y, [c.reshape(w.shape[1], w.shape[2]) for c, (w, _) in zip(cast, casts)]


def _t5_bucket_steps(seq):
    half = T5_BUCKETS // 2
    max_exact = half // 2
    rel = np.arange(-(seq - 1), seq)
    n = np.abs(rel)
    nf = np.maximum(n, 1).astype(np.float64)
    large = max_exact + np.floor(
        np.log(nf / max_exact) / math.log(T5_MAX_DIST / max_exact) * (half - max_exact) + 1e-9
    ).astype(np.int64)
    large = np.minimum(large, half - 1)
    bucket = np.where(rel > 0, half, 0) + np.where(n < max_exact, n, large)
    steps = [(int(rel[i]), int(bucket[i])) for i in range(1, len(rel)) if bucket[i] != bucket[i - 1]]
    sat = max(abs(steps[0][0]) + 1, abs(steps[-1][0]))
    return int(bucket[0]), steps, sat


def _diff_kernel(t5_ref, q_ref, k_ref, v_ref, lam_ref, g_ref, o_ref, *scratch, tq, n_buf,
                 lam_init, bucket_steps):
    s_refs, m_refs, ol_refs = (scratch[i * n_buf:(i + 1) * n_buf] for i in range(3))
    corr_ref, kfar_ref = scratch[3 * n_buf:]
    b = pl.program_id(0)
    seq = k_ref.shape[0]
    n_t = seq // tq
    band = 3 * tq
    first_bucket, steps, _ = bucket_steps
    last_bucket = steps[-1][1]
    split = tq + tq // 2
    one_lane = 2 * n_t
    assert one_lane + 2 <= LANES and n_t >= 4 and n_t % 2 == 0

    @pl.when(b == 0)
    def _init():
        row = lax.broadcasted_iota(jnp.int32, (tq, band), 0)
        col = lax.broadcasted_iota(jnp.int32, (tq, band), 1)
        rel = col - tq - row
        for hh in range(DIFF_HEADS):
            val = jnp.full((tq, band), t5_ref[first_bucket, hh], F32)
            for thr, bkt in steps:
                val = jnp.where(rel >= thr, t5_ref[bkt, hh], val)
            far = jnp.where(col < split, t5_ref[first_bucket, hh], t5_ref[last_bucket, hh])
            corr_ref[hh] = val - far
        corr_ref[DIFF_HEADS] = jnp.zeros((tq, band), F32)
        key = lax.broadcasted_iota(jnp.int32, (seq, LANES), 0)
        lane = lax.broadcasted_iota(jnp.int32, (seq, LANES), 1)
        tile = jnp.where(lane < n_t, lane, lane - n_t)
        step = jnp.where(key >= tile * tq + tq // 2, 1.0, 0.0)
        ones = jnp.where(lane < one_lane + 2, 1.0, 0.0)
        kfar_ref[...] = jnp.where(lane < one_lane, step, ones).astype(BF16)

    scale = 1.0 / math.sqrt(DIFF_QK_DIM)
    lane = lax.broadcasted_iota(jnp.int32, (tq, LANES), 1)
    lo = lane < DIFF_QK_DIM
    lp = lam_ref[...]
    lam = (jnp.exp(jnp.sum(lp[0:1] * lp[1:2], axis=-1, keepdims=True))
           - jnp.exp(jnp.sum(lp[2:3] * lp[3:4], axis=-1, keepdims=True)) + lam_init)

    chunk_order = [-1, 0, 1] + list(range(2, n_t - 1))

    def tile_ids(t):
        if isinstance(t, int):
            h, qi = divmod(t, n_t)
        else:
            h, qi = lax.div(t, n_t), lax.rem(t, n_t)
        return h, qi, pl.ds(_aligned(h * LANES, LANES), LANES)

    def key_rows(qi, d):
        chunk = (qi + d) % n_t if isinstance(qi, int) else lax.rem(qi + (d + n_t), n_t)
        return pl.ds(_aligned(chunk * tq, tq), tq)

    def scores(t, s_ref, m_ref):
        h, qi, cols = tile_ids(t)
        c_neg = t5_ref[first_bucket, h]
        c_step = t5_ref[last_bucket, h] - c_neg
        row0 = _aligned(qi * tq, tq)
        q = q_ref[pl.ds(row0, tq), cols] * scale
        zero = jnp.zeros_like(q)
        far = jnp.where((lane == qi) | (lane == n_t + qi), c_step,
                        jnp.where(lane >= one_lane, c_neg, 0.0))
        far = jnp.where(lane < one_lane + 2, far, 0.0)
        far_hi = far.astype(BF16)
        far_lo = (far - far_hi.astype(F32)).astype(BF16)
        q_far = jnp.where((lane < n_t) | (lane == one_lane), far_hi, far_lo)
        lhs = jnp.concatenate(
            [jnp.concatenate([jnp.where(lo, q, zero), q_far], axis=1),
             jnp.concatenate([jnp.where(lo, zero, q), q_far], axis=1)], axis=0)
        m_run = None
        for pos, d in enumerate(chunk_order):
            rows = key_rows(qi, d)
            rhs = jnp.concatenate([k_ref[rows, cols], kfar_ref[rows, :]], axis=1)
            s = lax.dot_general(lhs, rhs, (((1,), (1,)), ((), ())), preferred_element_type=F32)
            if d in (-1, 0, 1):
                inside = (qi + d >= 0) & (qi + d < n_t)
                head = jnp.where(inside, h, DIFF_HEADS)
                corr = corr_ref[head, :, (d + 1) * tq:(d + 2) * tq]
                s = s + jnp.concatenate([corr, corr], axis=0)
            s_ref[:, pos * tq:(pos + 1) * tq] = s
            m_run = s if m_run is None else jnp.maximum(m_run, s)
        m_ref[...] = functools.reduce(
            jnp.maximum, [m_run[:, i * LANES:(i + 1) * LANES] for i in range(tq // LANES)])

    def attend(t, s_ref, m_ref, ol_ref):
        _, qi, cols = tile_ids(t)
        m = jnp.broadcast_to(jnp.max(m_ref[...], axis=-1, keepdims=True), (2 * tq, LANES))
        m = jnp.concatenate([m] * (tq // LANES), axis=1)
        ol = None
        for pos, d in enumerate(chunk_order):
            e = jnp.exp((s_ref[:, pos * tq:(pos + 1) * tq] - m).astype(BF16))
            v = v_ref[key_rows(qi, d), cols]
            part = jnp.dot(e, jnp.concatenate([v, jnp.ones_like(v)], axis=1),
                           preferred_element_type=F32)
            ol = part if ol is None else ol + part
        ol_ref[...] = ol

    def finish(t, ol_ref):
        _, qi, cols = tile_ids(t)
        o = (ol_ref[0:tq, 0:LANES] / ol_ref[0:tq, LANES:2 * LANES]
             - lam * (ol_ref[tq:2 * tq, 0:LANES] / ol_ref[tq:2 * tq, LANES:2 * LANES]))
        row0 = _aligned(qi * tq, tq)
        o_ref[pl.ds(row0, tq), cols] = (
            _rmsnorm_f32(o, g_ref[...]) * (1.0 - lam_init)).astype(BF16)

    n_tiles = DIFF_HEADS * n_t

    def step(u, slot):
        live = (lambda t: True) if not isinstance(u, int) else (lambda t: 0 <= t < n_tiles)
        if live(u):
            scores(u, s_refs[slot], m_refs[slot])
        if live(u - 3):
            finish(u - 3, ol_refs[(slot - 3) % n_buf])
        if live(u - 2):
            a = (slot - 2) % n_buf
            attend(u - 2, s_refs[a], m_refs[a], ol_refs[a])

    first, trips = 3, (n_tiles - 3) // n_buf
    assert trips >= 1
    for u in range(first):
        step(u, u % n_buf)

    def rotation(j, carry):
        u = first + n_buf * j
        for i in range(n_buf):
            step(u + i, (first + i) % n_buf)
        return carry

    lax.fori_loop(0, trips, rotation, 0)
    for u in range(first + n_buf * trips, n_tiles + 3):
        step(u, u % n_buf)


def _diff_attention(qkv3, t5_bias, lam_params, subln_g, lam_init):
    B, S, _ = qkv3.shape
    tq = TQ_DIFF
    bucket_steps = _t5_bucket_steps(S)
    assert bucket_steps[2] <= tq and S % tq == 0 and tq % LANES == 0
    assert 2 * DIFF_QK_DIM == LANES and DIFF_V_DIM == LANES
    assert NA_WIDTH == DIFF_WIDTH
    q_col = 3
    n_buf = 3
    scratch = (
        [((2 * tq, S), F32)] * n_buf
        + [((2 * tq, LANES), F32)] * n_buf
        + [((2 * tq, 2 * LANES), F32)] * n_buf
        + [((DIFF_HEADS + 1, tq, 3 * tq), F32),
           ((S, LANES), BF16)])
    vmem = _vmem_limit(4 * _nbytes((S, DIFF_WIDTH), BF16, 2), *[_nbytes(*s) for s in scratch])
    return pl.pallas_call(
        functools.partial(_diff_kernel, tq=tq, n_buf=n_buf, lam_init=lam_init,
                          bucket_steps=bucket_steps),
        out_shape=jax.ShapeDtypeStruct((B, S, DIFF_WIDTH), BF16),
        grid=(B,),
        in_specs=[
            pl.BlockSpec(memory_space=pltpu.SMEM),
            pl.BlockSpec((None, S, DIFF_WIDTH), lambda b: (b, 0, q_col)),
            pl.BlockSpec((None, S, DIFF_WIDTH), lambda b: (b, 0, q_col + 1)),
            pl.BlockSpec((None, S, DIFF_WIDTH), lambda b: (b, 0, q_col + 2)),
            pl.BlockSpec((4, DIFF_QK_DIM), lambda b: (0, 0)),
            pl.BlockSpec((1, DIFF_V_DIM), lambda b: (0, 0)),
        ],
        out_specs=pl.BlockSpec((None, S, DIFF_WIDTH), lambda b: (b, 0, 0)),
        scratch_shapes=[pltpu.VMEM(*s) for s in scratch],
        compiler_params=pltpu.CompilerParams(
            dimension_semantics=("arbitrary",), vmem_limit_bytes=vmem),
        name="diff_attn",
    )(t5_bias, qkv3, qkv3, qkv3, lam_params, subln_g.reshape(1, DIFF_V_DIM))


def _mix_mlp_kernel(x_ref, yna_ref, ydf_ref, gate_ref, wna_ref, wdf_ref, wout_ref, g_ref, w1_ref,
                    w2_ref, fg_ref, o_ref, *, tf, final):
    d = x_ref.shape[1]
    b_na = jnp.dot(yna_ref[...], wna_ref[...], preferred_element_type=F32)
    b_df = jnp.dot(ydf_ref[...], wdf_ref[...], preferred_element_type=F32)
    merged = (jax.nn.sigmoid(gate_ref[:, 0:d]) * b_na
              + jax.nn.sigmoid(gate_ref[:, d:2 * d]) * b_df)
    x = x_ref[...] + jnp.dot(merged.astype(BF16), wout_ref[...], preferred_element_type=F32)
    h = _rmsnorm_f32(x, g_ref[...]).astype(BF16)
    for c in range(w1_ref.shape[1] // tf):
        u = jnp.dot(h, w1_ref[:, c * tf:(c + 1) * tf], preferred_element_type=F32)
        u = jnp.square(jnp.maximum(u, 0.0)).astype(BF16)
        x = x + jnp.dot(u, w2_ref[c * tf:(c + 1) * tf, :], preferred_element_type=F32)
    o_ref[...] = _rmsnorm_f32(x, fg_ref[...]) if final else x


def _mix_mlp(xt, y_na, y_df, gates, w_na, w_df, w_out, g, w1, w2, final_g, final):
    T, D = xt.shape
    F = w1.shape[1]
    tm, tf = TM_MLP, TF_MLP
    once = pl.Buffered(1)
    assert F % tf == 0
    weights = (w_na, w_df, w_out, w1, w2)
    vmem = _vmem_limit(
        2 * _nbytes((tm, D), F32, 2),
        _nbytes((tm, NA_WIDTH), BF16, 2), _nbytes((tm, DIFF_WIDTH), BF16, 2),
        _nbytes((tm, 2 * D), F32, 2), *[_nbytes(w.shape, BF16) for w in weights])
    return pl.pallas_call(
        functools.partial(_mix_mlp_kernel, tf=tf, final=final),
        out_shape=jax.ShapeDtypeStruct((T, D), F32),
        grid=(T // tm,),
        in_specs=[
            pl.BlockSpec((tm, D), lambda i: (i, 0)),
            pl.BlockSpec((tm, NA_WIDTH), lambda i: (i, 0)),
            pl.BlockSpec((tm, DIFF_WIDTH), lambda i: (i, 0)),
            pl.BlockSpec((tm, 2 * D), lambda i: (i, 0)),
            pl.BlockSpec((NA_WIDTH, D), lambda i: (0, 0), pipeline_mode=once),
            pl.BlockSpec((DIFF_WIDTH, D), lambda i: (0, 0), pipeline_mode=once),
            pl.BlockSpec((D, D), lambda i: (0, 0), pipeline_mode=once),
            pl.BlockSpec((1, D), lambda i: (0, 0)),
            pl.BlockSpec((D, F), lambda i: (0, 0), pipeline_mode=once),
            pl.BlockSpec((F, D), lambda i: (0, 0), pipeline_mode=once),
            pl.BlockSpec((1, D), lambda i: (0, 0)),
        ],
        out_specs=pl.BlockSpec((tm, D), lambda i: (i, 0)),
        compiler_params=pltpu.CompilerParams(
            dimension_semantics=("arbitrary",), vmem_limit_bytes=vmem),
        name="mix_mlp",
    )(xt, y_na, y_df, gates, w_na, w_df, w_out, g.reshape(1, D), w1, w2, final_g.reshape(1, D))


def kernel(x, t5_bias, final_norm_g, norm1_g, w_in, na_rpb, diff_lambda, diff_subln_g, w_na_o,
           w_diff_o, w_out, norm2_g, w_ff1, w_ff2):
    B, S, D = x.shape
    depth = w_in.shape[0]
    T = B * S
    assert w_in.shape[2] == QKV_WIDTH + 2 * D and S % GRID_W == 0
    xt = x.reshape(T, D)
    na_bias = _na_bias_table(na_rpb, S // GRID_W)
    w_in_b = w_in[0].astype(BF16)
    for layer in range(depth):
        qkv, gates = _in_proj(xt, norm1_g[layer], w_in_b)
        qkv3 = qkv.reshape(B, S, QKV_WIDTH)
        casts = [(w, layer) for w in (w_na_o, w_diff_o, w_out, w_ff1, w_ff2)]
        if layer + 1 < depth:
            casts.append((w_in, layer + 1))
        y_na, (w_na_b, w_df_b, w_out_b, w1_b, w2_b, *w_in_next) = _na_attention(
            qkv3, na_bias, layer, casts)
        lam_init = 0.8 - 0.6 * math.exp(-0.3 * layer)
        y_df = _diff_attention(qkv3, t5_bias, diff_lambda[layer], diff_subln_g[layer], lam_init)
        xt = _mix_mlp(xt, y_na.reshape(T, NA_WIDTH), y_df.reshape(T, DIFF_WIDTH), gates,
                      w_na_b, w_df_b, w_out_b, norm2_g[layer], w1_b, w2_b, final_norm_g,
                      final=(layer == depth - 1))
        if w_in_next:
            w_in_b = w_in_next[0]
    return xt.reshape(B, S, D)
```

```python
import functools
import math

import numpy as np
import jax
import jax.numpy as jnp
from jax import lax
from jax.experimental import pallas as pl
from jax.experimental.pallas import tpu as pltpu

GRID_W = 64
NA_HEADS = 8
NA_HEAD_DIM = 64
NA_WIN_ROWS = 8
NA_WIN_COLS = 16
DIFF_HEADS = 4
DIFF_QK_DIM = 64
DIFF_V_DIM = 2 * DIFF_QK_DIM
T5_BUCKETS = 32
T5_MAX_DIST = 128
RMS_EPS = 1e-6
NEG_INF = -1e30

NA_WIDTH = NA_HEADS * NA_HEAD_DIM
DIFF_WIDTH = DIFF_HEADS * DIFF_V_DIM
QKV_WIDTH = 3 * NA_WIDTH + 3 * DIFF_WIDTH

LANES = 128
BF16_SUBLANES = 16
V7X_VMEM_BYTES = 64 * 2**20
VMEM_TEMP_BYTES = 12 * 2**20

TM_PROJ = 1024
TR_PROJ = 512
TN_PROJ = 1024
TM_MLP = 512
TF_MLP = 1024
NA_ROW_BLOCK = 4
NA_WIN_BLOCKS = 3
NA_BATCH_BLOCK = 4
TQ_DIFF = 256

BF16 = jnp.bfloat16
F32 = jnp.float32


def _nbytes(shape, dtype, buffers=1):
    return buffers * math.prod(shape) * jnp.dtype(dtype).itemsize


def _vmem_limit(*window_bytes):
    total = sum(window_bytes) + VMEM_TEMP_BYTES
    assert total < V7X_VMEM_BYTES
    return int(total)


def _aligned(index, multiple):
    return index if isinstance(index, int) else pl.multiple_of(index, multiple)


def _rmsnorm_f32(x, g):
    return (x * lax.rsqrt(jnp.mean(x * x, axis=-1, keepdims=True) + RMS_EPS)) * g


def _in_proj_kernel(x_ref, g_ref, w_ref, qkv_ref, gate_ref, *, rows, chunk):
    n_qkv = qkv_ref.shape[1]
    n_gate = gate_ref.shape[1]
    for r in range(x_ref.shape[0] // rows):
        rs = slice(r * rows, (r + 1) * rows)
        hb = _rmsnorm_f32(x_ref[rs, :], g_ref[...]).astype(BF16)
        for c in range(n_qkv // chunk):
            cs = slice(c * chunk, (c + 1) * chunk)
            qkv_ref[rs, cs] = jnp.dot(hb, w_ref[:, cs], preferred_element_type=F32).astype(BF16)
        for c in range(n_gate // chunk):
            cs = slice(c * chunk, (c + 1) * chunk)
            ws = slice(n_qkv + c * chunk, n_qkv + (c + 1) * chunk)
            gate_ref[rs, cs] = jnp.dot(hb, w_ref[:, ws], preferred_element_type=F32)


def _in_proj(xt, g, w):
    T, D = xt.shape
    n_all = w.shape[1]
    n_gate = n_all - QKV_WIDTH
    tm = TM_PROJ
    vmem = _vmem_limit(_nbytes((tm, D), F32, 2), _nbytes((D, n_all), BF16),
                       _nbytes((tm, QKV_WIDTH), BF16, 2), _nbytes((tm, n_gate), F32, 2))
    return pl.pallas_call(
        functools.partial(_in_proj_kernel, rows=TR_PROJ, chunk=TN_PROJ),
        out_shape=(jax.ShapeDtypeStruct((T, QKV_WIDTH), BF16),
                   jax.ShapeDtypeStruct((T, n_gate), F32)),
        grid=(T // tm,),
        in_specs=[
            pl.BlockSpec((tm, D), lambda i: (i, 0)),
            pl.BlockSpec((1, D), lambda i: (0, 0)),
            pl.BlockSpec((D, n_all), lambda i: (0, 0), pipeline_mode=pl.Buffered(1)),
        ],
        out_specs=(pl.BlockSpec((tm, QKV_WIDTH), lambda i: (i, 0)),
                   pl.BlockSpec((tm, n_gate), lambda i: (i, 0))),
        compiler_params=pltpu.CompilerParams(
            dimension_semantics=("arbitrary",), vmem_limit_bytes=vmem),
        name="in_proj",
    )(xt, g.reshape(1, D), w)


def _na_geometry(rows):
    rb_rows = NA_ROW_BLOCK
    n_rb = rows // rb_rows
    win_rows = NA_WIN_BLOCKS * rb_rows
    wr = min(NA_WIN_ROWS, rows)
    kb = np.clip(np.arange(n_rb) - 1, 0, n_rb - NA_WIN_BLOCKS)
    dr_idx = np.zeros((n_rb, rb_rows, win_rows), np.int32)
    valid = np.zeros((n_rb, rb_rows, win_rows), bool)
    for rb in range(n_rb):
        w0 = kb[rb] * rb_rows
        for ri in range(rb_rows):
            r = rb * rb_rows + ri
            r0 = min(max(r - wr // 2, 0), rows - wr)
            assert w0 <= r0 and r0 + wr <= w0 + win_rows
            for wj in range(win_rows):
                krow = w0 + wj
                valid[rb, ri, wj] = r0 <= krow < r0 + wr
                dr_idx[rb, ri, wj] = min(max(krow - r + NA_WIN_ROWS - 1, 0), 2 * NA_WIN_ROWS - 2)
    for rb in range(2, n_rb - 1):
        assert (valid[rb] == valid[1]).all() and (dr_idx[rb] == dr_idx[1]).all()
    classes = [0, 1, n_rb - 1]
    return n_rb, kb, dr_idx[classes], valid[classes]


def _na_bias_kernel(rpb_ref, o_ref, *, dr_idx, row_valid):
    assert LANES == 2 * GRID_W
    n_cls, rb_rows, win_rows = dr_idx.shape
    c = lax.broadcasted_iota(jnp.int32, (GRID_W, LANES), 0)
    lane = lax.broadcasted_iota(jnp.int32, (GRID_W, LANES), 1)
    left = lane < GRID_W
    kc = jnp.where(left, lane, lane - GRID_W)
    c0 = jnp.clip(c - NA_WIN_COLS // 2, 0, GRID_W - NA_WIN_COLS)
    col_ok = (kc >= c0) & (kc < c0 + NA_WIN_COLS)
    neg = jnp.full((GRID_W, LANES), NEG_INF, F32)

    def half_tile(d, right):
        w = jnp.broadcast_to(rpb_ref[d:d + 1, :], (GRID_W, LANES))
        t = pltpu.roll(w, GRID_W if right else 0, 1, stride=1, stride_axis=0)
        return jnp.where(col_ok, t, neg)

    for cls in range(n_cls):
        for ri in range(rb_rows):
            for p in range(win_rows // 2):
                halves = []
                for side in range(2):
                    wj = 2 * p + side
                    ok = bool(row_valid[cls, ri, wj])
                    halves.append(half_tile(int(dr_idx[cls, ri, wj]), side == 1) if ok else neg)
                o_ref[cls, ri * GRID_W:(ri + 1) * GRID_W, p * LANES:(p + 1) * LANES] = (
                    jnp.where(left, halves[0], halves[1]))


def _na_bias_table(na_rpb, rows):
    depth, H, n_dr, n_dc = na_rpb.shape
    assert n_dr == 2 * NA_WIN_ROWS - 1 and n_dc == 2 * NA_WIN_COLS - 1
    _, _, dr_idx, row_valid = _na_geometry(rows)
    n_cls, rb_rows, win_rows = dr_idx.shape
    rq, wk = rb_rows * GRID_W, win_rows * GRID_W
    lanes = jnp.pad(na_rpb, ((0, 0), (0, 0), (0, 0), (0, LANES - n_dc)), constant_values=NEG_INF)
    lanes = jnp.roll(lanes, -(NA_WIN_COLS - 1), axis=-1)
    return pl.pallas_call(
        functools.partial(_na_bias_kernel, dr_idx=dr_idx, row_valid=row_valid),
        out_shape=jax.ShapeDtypeStruct((depth, n_cls, H, rq, wk), F32),
        grid=(depth, H),
        in_specs=[pl.BlockSpec((None, None, n_dr, LANES), lambda l, h: (l, h, 0, 0))],
        out_specs=pl.BlockSpec((None, n_cls, None, rq, wk), lambda l, h: (l, 0, h, 0, 0)),
        compiler_params=pltpu.CompilerParams(dimension_semantics=("arbitrary", "arbitrary")),
        name="na_bias",
    )(lanes)


def _na_kernel(q_ref, k0_ref, k1_ref, k2_ref, v0_ref, v1_ref, v2_ref, bias_ref, *rest):
    n_cast = len(rest) // 2
    o_ref = rest[n_cast]
    for w_ref, wb_ref in zip(rest[:n_cast], rest[n_cast + 1:]):
        wb_ref[...] = w_ref[...].astype(BF16)
    nb, rq = q_ref.shape[0], q_ref.shape[1]
    scale = 1.0 / math.sqrt(NA_HEAD_DIM)
    lo = lax.broadcasted_iota(jnp.int32, (rq, LANES), 1) < NA_HEAD_DIM
    heads_per_vreg = LANES // NA_HEAD_DIM
    for bi, hp in np.ndindex(nb, NA_HEADS // heads_per_vreg):
        cs = (bi, slice(None), slice(hp * LANES, (hp + 1) * LANES))
        q = q_ref[cs] * scale
        k = jnp.concatenate([k0_ref[cs], k1_ref[cs], k2_ref[cs]], axis=0)
        v = jnp.concatenate([v0_ref[cs], v1_ref[cs], v2_ref[cs]], axis=0)
        v_ones = jnp.concatenate([v, jnp.ones_like(v)], axis=1)
        outs = []
        for e in range(heads_per_vreg):
            qm = jnp.where(lo if e == 0 else jnp.logical_not(lo), q, jnp.zeros_like(q))
            s = lax.dot_general(qm, k, (((1,), (1,)), ((), ())), preferred_element_type=F32)
            s = s + bias_ref[heads_per_vreg * hp + e]
            m = jnp.max(s, axis=-1, keepdims=True)
            p = jnp.exp((s - m).astype(BF16))
            ol = jnp.dot(p, v_ones, preferred_element_type=F32)
            outs.append(ol[:, 0:LANES] / ol[:, LANES:2 * LANES])
        o_ref[cs] = jnp.where(lo, outs[0], outs[1]).astype(BF16)


def _na_attention(qkv3, bias_table, layer, casts):
    B, S, _ = qkv3.shape
    rows = S // GRID_W
    n_rb, kb, _, _ = _na_geometry(rows)
    rq = NA_ROW_BLOCK * GRID_W
    assert NA_HEADS * NA_HEAD_DIM == NA_WIDTH and LANES % NA_HEAD_DIM == 0
    k_col, v_col = 1, 2
    n_kb = n_rb - NA_WIN_BLOCKS

    nb = NA_BATCH_BLOCK
    assert B % nb == 0

    def kv_spec(col, t):
        return pl.BlockSpec((nb, rq, NA_WIDTH),
                            lambda rb, b: (b, jnp.clip(rb - 1, 0, n_kb) + t, col))

    def bias_map(rb, b):
        cls = (rb > 0).astype(jnp.int32) + (rb == n_rb - 1).astype(jnp.int32)
        return (layer, cls, 0, 0, 0)

    n_bb = B // nb
    n_steps = n_rb * n_bb
    cast_in, cast_in_specs, cast_out, cast_out_specs = [], [], [], []
    for w, w_layer in casts:
        depth, R, C = w.shape
        slab = R // n_steps
        assert R % n_steps == 0 and slab % BF16_SUBLANES == 0
        cast_in.append(w.reshape(depth * n_steps, slab, C))
        first = w_layer * n_steps
        cast_in_specs.append(pl.BlockSpec(
            (None, slab, C), lambda rb, b, first=first: (first + rb * n_bb + b, 0, 0)))
        cast_out.append(jax.ShapeDtypeStruct((n_steps, slab, C), BF16))
        cast_out_specs.append(pl.BlockSpec((None, slab, C), lambda rb, b: (rb * n_bb + b, 0, 0)))

    wk = NA_WIN_BLOCKS * rq
    n_qkv_windows = 2 + 2 * NA_WIN_BLOCKS
    vmem = _vmem_limit(
        n_qkv_windows * _nbytes((nb, rq, NA_WIDTH), BF16, 2),
        _nbytes((NA_HEADS, rq, wk), F32, 2),
        *[_nbytes(o.shape[1:], F32, 2) + _nbytes(o.shape[1:], BF16, 2) for o in cast_out])
    y, *cast = pl.pallas_call(
        _na_kernel,
        out_shape=[jax.ShapeDtypeStruct((B, S, NA_WIDTH), BF16)] + cast_out,
        grid=(n_rb, n_bb),
        in_specs=[pl.BlockSpec((nb, rq, NA_WIDTH), lambda rb, b: (b, rb, 0))]
        + [kv_spec(k_col, t) for t in range(NA_WIN_BLOCKS)]
        + [kv_spec(v_col, t) for t in range(NA_WIN_BLOCKS)]
        + [pl.BlockSpec((None, None, NA_HEADS, rq, wk), bias_map)]
        + cast_in_specs,
        out_specs=[pl.BlockSpec((nb, rq, NA_WIDTH), lambda rb, b: (b, rb, 0))] + cast_out_specs,
        compiler_params=pltpu.CompilerParams(
            dimension_semantics=("arbitrary", "arbitrary"), vmem_limit_bytes=vmem),
        name="na_attn",
    )(qkv3, qkv3, qkv3, qkv3, qkv3, qkv3, qkv3, bias_table, *cast_in)
    return y, [c.reshape(w.shape[1], w.shape[2]) for c, (w, _) in zip(cast, casts)]


def _t5_bucket_steps(seq):
    half = T5_BUCKETS // 2
    max_exact = half // 2
    rel = np.arange(-(seq - 1), seq)
    n = np.abs(rel)
    nf = np.maximum(n, 1).astype(np.float64)
    large = max_exact + np.floor(
        np.log(nf / max_exact) / math.log(T5_MAX_DIST / max_exact) * (half - max_exact) + 1e-9
    ).astype(np.int64)
    large = np.minimum(large, half - 1)
    bucket = np.where(rel > 0, half, 0) + np.where(n < max_exact, n, large)
    steps = [(int(rel[i]), int(bucket[i])) for i in range(1, len(rel)) if bucket[i] != bucket[i - 1]]
    sat = max(abs(steps[0][0]) + 1, abs(steps[-1][0]))
    return int(bucket[0]), steps, sat


def _diff_kernel(t5_ref, q_ref, k_ref, v_ref, lam_ref, g_ref, o_ref, *scratch, tq, n_buf,
                 lam_init, bucket_steps):
    s_refs, m_refs, ol_refs = (scratch[i * n_buf:(i + 1) * n_buf] for i in range(3))
    corr_ref, kfar_ref = scratch[3 * n_buf:]
    b = pl.program_id(0)
    seq = k_ref.shape[0]
    n_t = seq // tq
    band = 3 * tq
    first_bucket, steps, _ = bucket_steps
    last_bucket = steps[-1][1]
    split = tq + tq // 2
    one_lane = 2 * n_t
    assert one_lane + 2 <= LANES and n_t >= 4 and n_t % 2 == 0

    @pl.when(b == 0)
    def _init():
        row = lax.broadcasted_iota(jnp.int32, (tq, band), 0)
        col = lax.broadcasted_iota(jnp.int32, (tq, band), 1)
        rel = col - tq - row
        for hh in range(DIFF_HEADS):
            val = jnp.full((tq, band), t5_ref[first_bucket, hh], F32)
            for thr, bkt in steps:
                val = jnp.where(rel >= thr, t5_ref[bkt, hh], val)
            far = jnp.where(col < split, t5_ref[first_bucket, hh], t5_ref[last_bucket, hh])
            corr_ref[hh] = val - far
        corr_ref[DIFF_HEADS] = jnp.zeros((tq, band), F32)
        key = lax.broadcasted_iota(jnp.int32, (seq, LANES), 0)
        lane = lax.broadcasted_iota(jnp.int32, (seq, LANES), 1)
        tile = jnp.where(lane < n_t, lane, lane - n_t)
        step = jnp.where(key >= tile * tq + tq // 2, 1.0, 0.0)
        ones = jnp.where(lane < one_lane + 2, 1.0, 0.0)
        kfar_ref[...] = jnp.where(lane < one_lane, step, ones).astype(BF16)

    scale = 1.0 / math.sqrt(DIFF_QK_DIM)
    lane = lax.broadcasted_iota(jnp.int32, (tq, LANES), 1)
    lo = lane < DIFF_QK_DIM
    lp = lam_ref[...]
    lam = (jnp.exp(jnp.sum(lp[0:1] * lp[1:2], axis=-1, keepdims=True))
           - jnp.exp(jnp.sum(lp[2:3] * lp[3:4], axis=-1, keepdims=True)) + lam_init)

    chunk_order = [-1, 0, 1] + list(range(2, n_t - 1))

    def tile_ids(t):
        if isinstance(t, int):
            h, qi = divmod(t, n_t)
        else:
            h, qi = lax.div(t, n_t), lax.rem(t, n_t)
        return h, qi, pl.ds(_aligned(h * LANES, LANES), LANES)

    def key_rows(qi, d):
        chunk = (qi + d) % n_t if isinstance(qi, int) else lax.rem(qi + (d + n_t), n_t)
        return pl.ds(_aligned(chunk * tq, tq), tq)

    def scores(t, s_ref, m_ref):
        h, qi, cols = tile_ids(t)
        c_neg = t5_ref[first_bucket, h]
        c_step = t5_ref[last_bucket, h] - c_neg
        row0 = _aligned(qi * tq, tq)
        q = q_ref[pl.ds(row0, tq), cols] * scale
        zero = jnp.zeros_like(q)
        far = jnp.where((lane == qi) | (lane == n_t + qi), c_step,
                        jnp.where(lane >= one_lane, c_neg, 0.0))
        far = jnp.where(lane < one_lane + 2, far, 0.0)
        far_hi = far.astype(BF16)
        far_lo = (far - far_hi.astype(F32)).astype(BF16)
        q_far = jnp.where((lane < n_t) | (lane == one_lane), far_hi, far_lo)
        lhs = jnp.concatenate(
            [jnp.concatenate([jnp.where(lo, q, zero), q_far], axis=1),
             jnp.concatenate([jnp.where(lo, zero, q), q_far], axis=1)], axis=0)
        m_run = None
        for pos, d in enumerate(chunk_order):
            rows = key_rows(qi, d)
            rhs = jnp.concatenate([k_ref[rows, cols], kfar_ref[rows, :]], axis=1)
            s = lax.dot_general(lhs, rhs, (((1,), (1,)), ((), ())), preferred_element_type=F32)
            if d in (-1, 0, 1):
                inside = (qi + d >= 0) & (qi + d < n_t)
                head = jnp.where(inside, h, DIFF_HEADS)
                corr = corr_ref[head, :, (d + 1) * tq:(d + 2) * tq]
                s = s + jnp.concatenate([corr, corr], axis=0)
            s_ref[:, pos * tq:(pos + 1) * tq] = s
            m_chunk = functools.reduce(
                jnp.maximum, [s[:, i * LANES:(i + 1) * LANES] for i in range(tq // LANES)])
            m_run = m_chunk if m_run is None else jnp.maximum(m_run, m_chunk)
        m_ref[...] = m_run

    def attend(t, s_ref, m_ref, ol_ref):
        _, qi, cols = tile_ids(t)
        m = jnp.broadcast_to(jnp.max(m_ref[...], axis=-1, keepdims=True), (2 * tq, LANES))
        m = jnp.concatenate([m] * (tq // LANES), axis=1)
        ol = None
        for pos, d in enumerate(chunk_order):
            e = jnp.exp((s_ref[:, pos * tq:(pos + 1) * tq] - m).astype(BF16))
            v = v_ref[key_rows(qi, d), cols]
            part = jnp.dot(e, jnp.concatenate([v, jnp.ones_like(v)], axis=1),
                           preferred_element_type=F32)
            ol = part if ol is None else ol + part
        ol_ref[...] = ol

    def finish(t, ol_ref):
        _, qi, cols = tile_ids(t)
        o = (ol_ref[0:tq, 0:LANES] / ol_ref[0:tq, LANES:2 * LANES]
             - lam * (ol_ref[tq:2 * tq, 0:LANES] / ol_ref[tq:2 * tq, LANES:2 * LANES]))
        row0 = _aligned(qi * tq, tq)
        o_ref[pl.ds(row0, tq), cols] = (
            _rmsnorm_f32(o, g_ref[...]) * (1.0 - lam_init)).astype(BF16)

    n_tiles = DIFF_HEADS * n_t

    def step(u, slot):
        live = (lambda t: True) if not isinstance(u, int) else (lambda t: 0 <= t < n_tiles)
        if live(u):
            scores(u, s_refs[slot], m_refs[slot])
        if live(u - 3):
            finish(u - 3, ol_refs[(slot - 3) % n_buf])
        if live(u - 2):
            a = (slot - 2) % n_buf
            attend(u - 2, s_refs[a], m_refs[a], ol_refs[a])

    first, trips = 3, (n_tiles - 3) // n_buf
    assert trips >= 1
    for u in range(first):
        step(u, u % n_buf)

    def rotation(j, carry):
        u = first + n_buf * j
        for i in range(n_buf):
            step(u + i, (first + i) % n_buf)
        return carry

    lax.fori_loop(0, trips, rotation, 0)
    for u in range(first + n_buf * trips, n_tiles + 3):
        step(u, u % n_buf)


def _diff_attention(qkv3, t5_bias, lam_params, subln_g, lam_init):
    B, S, _ = qkv3.shape
    tq = TQ_DIFF
    bucket_steps = _t5_bucket_steps(S)
    assert bucket_steps[2] <= tq and S % tq == 0 and tq % LANES == 0
    assert 2 * DIFF_QK_DIM == LANES and DIFF_V_DIM == LANES
    assert NA_WIDTH == DIFF_WIDTH
    q_col = 3
    n_buf = 3
    scratch = (
        [((2 * tq, S), F32)] * n_buf
        + [((2 * tq, LANES), F32)] * n_buf
        + [((2 * tq, 2 * LANES), F32)] * n_buf
        + [((DIFF_HEADS + 1, tq, 3 * tq), F32),
           ((S, LANES), BF16)])
    vmem = _vmem_limit(4 * _nbytes((S, DIFF_WIDTH), BF16, 2), *[_nbytes(*s) for s in scratch])
    return pl.pallas_call(
        functools.partial(_diff_kernel, tq=tq, n_buf=n_buf, lam_init=lam_init,
                          bucket_steps=bucket_steps),
        out_shape=jax.ShapeDtypeStruct((B, S, DIFF_WIDTH), BF16),
        grid=(B,),
        in_specs=[
            pl.BlockSpec(memory_space=pltpu.SMEM),
            pl.BlockSpec((None, S, DIFF_WIDTH), lambda b: (b, 0, q_col)),
            pl.BlockSpec((None, S, DIFF_WIDTH), lambda b: (b, 0, q_col + 1)),
            pl.BlockSpec((None, S, DIFF_WIDTH), lambda b: (b, 0, q_col + 2)),
            pl.BlockSpec((4, DIFF_QK_DIM), lambda b: (0, 0)),
            pl.BlockSpec((1, DIFF_V_DIM), lambda b: (0, 0)),
        ],
        out_specs=pl.BlockSpec((None, S, DIFF_WIDTH), lambda b: (b, 0, 0)),
        scratch_shapes=[pltpu.VMEM(*s) for s in scratch],
        compiler_params=pltpu.CompilerParams(
            dimension_semantics=("arbitrary",), vmem_limit_bytes=vmem),
        name="diff_attn",
    )(t5_bias, qkv3, qkv3, qkv3, lam_params, subln_g.reshape(1, DIFF_V_DIM))


def _mix_mlp_kernel(x_ref, yna_ref, ydf_ref, gate_ref, wna_ref, wdf_ref, wout_ref, g_ref, w1_ref,
                    w2_ref, fg_ref, o_ref, *, tf, final):
    d = x_ref.shape[1]
    b_na = jnp.dot(yna_ref[...], wna_ref[...], preferred_element_type=F32)
    b_df = jnp.dot(ydf_ref[...], wdf_ref[...], preferred_element_type=F32)
    merged = (jax.nn.sigmoid(gate_ref[:, 0:d]) * b_na
              + jax.nn.sigmoid(gate_ref[:, d:2 * d]) * b_df)
    x = x_ref[...] + jnp.dot(merged.astype(BF16), wout_ref[...], preferred_element_type=F32)
    h = _rmsnorm_f32(x, g_ref[...]).astype(BF16)
    for c in range(w1_ref.shape[1] // tf):
        u = jnp.dot(h, w1_ref[:, c * tf:(c + 1) * tf], preferred_element_type=F32)
        u = jnp.square(jnp.maximum(u, 0.0)).astype(BF16)
        x = x + jnp.dot(u, w2_ref[c * tf:(c + 1) * tf, :], preferred_element_type=F32)
    o_ref[...] = _rmsnorm_f32(x, fg_ref[...]) if final else x


def _mix_mlp(xt, y_na, y_df, gates, w_na, w_df, w_out, g, w1, w2, final_g, final):
    T, D = xt.shape
    F = w1.shape[1]
    tm, tf = TM_MLP, TF_MLP
    once = pl.Buffered(1)
    assert F % tf == 0
    weights = (w_na, w_df, w_out, w1, w2)
    vmem = _vmem_limit(
        2 * _nbytes((tm, D), F32, 2),
        _nbytes((tm, NA_WIDTH), BF16, 2), _nbytes((tm, DIFF_WIDTH), BF16, 2),
        _nbytes((tm, 2 * D), F32, 2), *[_nbytes(w.shape, BF16) for w in weights])
    return pl.pallas_call(
        functools.partial(_mix_mlp_kernel, tf=tf, final=final),
        out_shape=jax.ShapeDtypeStruct((T, D), F32),
        grid=(T // tm,),
        in_specs=[
            pl.BlockSpec((tm, D), lambda i: (i, 0)),
            pl.BlockSpec((tm, NA_WIDTH), lambda i: (i, 0)),
            pl.BlockSpec((tm, DIFF_WIDTH), lambda i: (i, 0)),
            pl.BlockSpec((tm, 2 * D), lambda i: (i, 0)),
            pl.BlockSpec((NA_WIDTH, D), lambda i: (0, 0), pipeline_mode=once),
            pl.BlockSpec((DIFF_WIDTH, D), lambda i: (0, 0), pipeline_mode=once),
            pl.BlockSpec((D, D), lambda i: (0, 0), pipeline_mode=once),
            pl.BlockSpec((1, D), lambda i: (0, 0)),
            pl.BlockSpec((D, F), lambda i: (0, 0), pipeline_mode=once),
            pl.BlockSpec((F, D), lambda i: (0, 0), pipeline_mode=once),
            pl.BlockSpec((1, D), lambda i: (0, 0)),
        ],
        out_specs=pl.BlockSpec((tm, D), lambda i: (i, 0)),
        compiler_params=pltpu.CompilerParams(
            dimension_semantics=("arbitrary",), vmem_limit_bytes=vmem),
        name="mix_mlp",
    )(xt, y_na, y_df, gates, w_na, w_df, w_out, g.reshape(1, D), w1, w2, final_g.reshape(1, D))


def kernel(x, t5_bias, final_norm_g, norm1_g, w_in, na_rpb, diff_lambda, diff_subln_g, w_na_o,
           w_diff_o, w_out, norm2_g, w_ff1, w_ff2):
    B, S, D = x.shape
    depth = w_in.shape[0]
    T = B * S
    assert w_in.shape[2] == QKV_WIDTH + 2 * D and S % GRID_W == 0
    xt = x.reshape(T, D)
    na_bias = _na_bias_table(na_rpb, S // GRID_W)
    w_in_b = w_in[0].astype(BF16)
    for layer in range(depth):
        qkv, gates = _in_proj(xt, norm1_g[layer], w_in_b)
        qkv3 = qkv.reshape(B, S, QKV_WIDTH)
        casts = [(w, layer) for w in (w_na_o, w_diff_o, w_out, w_ff1, w_ff2)]
        if layer + 1 < depth:
            casts.append((w_in, layer + 1))
        y_na, (w_na_b, w_df_b, w_out_b, w1_b, w2_b, *w_in_next) = _na_attention(
            qkv3, na_bias, layer, casts)
        lam_init = 0.8 - 0.6 * math.exp(-0.3 * layer)
        y_df = _diff_attention(qkv3, t5_bias, diff_lambda[layer], diff_subln_g[layer], lam_init)
        xt = _mix_mlp(xt, y_na.reshape(T, NA_WIDTH), y_df.reshape(T, DIFF_WIDTH), gates,
                      w_na_b, w_df_b, w_out_b, norm2_g[layer], w1_b, w2_b, final_norm_g,
                      final=(layer == depth - 1))
        if w_in_next:
            w_in_b = w_in_next[0]
    return xt.reshape(B, S, D)
```

```python
import functools
import math

import numpy as np
import jax
import jax.numpy as jnp
from jax import lax
from jax.experimental import pallas as pl
from jax.experimental.pallas import tpu as pltpu

GRID_W = 64
NA_HEADS = 8
NA_HEAD_DIM = 64
NA_WIN_ROWS = 8
NA_WIN_COLS = 16
DIFF_HEADS = 4
DIFF_QK_DIM = 64
DIFF_V_DIM = 2 * DIFF_QK_DIM
T5_BUCKETS = 32
T5_MAX_DIST = 128
RMS_EPS = 1e-6
NEG_INF = -1e30

NA_WIDTH = NA_HEADS * NA_HEAD_DIM
DIFF_WIDTH = DIFF_HEADS * DIFF_V_DIM
QKV_WIDTH = 3 * NA_WIDTH + 3 * DIFF_WIDTH

LANES = 128
BF16_SUBLANES = 16
V7X_VMEM_BYTES = 64 * 2**20
VMEM_TEMP_BYTES = 12 * 2**20

TM_PROJ = 1024
TR_PROJ = 512
TN_PROJ = 1024
TM_MLP = 512
TF_MLP = 1024
NA_ROW_BLOCK = 4
NA_WIN_BLOCKS = 3
NA_BATCH_BLOCK = 4
TQ_DIFF = 256
FAR_PARTS = 3

BF16 = jnp.bfloat16
F32 = jnp.float32


def _nbytes(shape, dtype, buffers=1):
    return buffers * math.prod(shape) * jnp.dtype(dtype).itemsize


def _vmem_limit(*window_bytes):
    total = sum(window_bytes) + VMEM_TEMP_BYTES
    assert total < V7X_VMEM_BYTES
    return int(total)


def _aligned(index, multiple):
    return index if isinstance(index, int) else pl.multiple_of(index, multiple)


def _rmsnorm_f32(x, g):
    return (x * lax.rsqrt(jnp.mean(x * x, axis=-1, keepdims=True) + RMS_EPS)) * g


def _in_proj_kernel(x_ref, g_ref, w_ref, qkv_ref, gate_ref, *, rows, chunk):
    n_qkv = qkv_ref.shape[1]
    n_gate = gate_ref.shape[1]
    for r in range(x_ref.shape[0] // rows):
        rs = slice(r * rows, (r + 1) * rows)
        hb = _rmsnorm_f32(x_ref[rs, :], g_ref[...]).astype(BF16)
        for c in range(n_qkv // chunk):
            cs = slice(c * chunk, (c + 1) * chunk)
            qkv_ref[rs, cs] = jnp.dot(hb, w_ref[:, cs], preferred_element_type=F32).astype(BF16)
        for c in range(n_gate // chunk):
            cs = slice(c * chunk, (c + 1) * chunk)
            ws = slice(n_qkv + c * chunk, n_qkv + (c + 1) * chunk)
            gate_ref[rs, cs] = jnp.dot(hb, w_ref[:, ws], preferred_element_type=F32)


def _in_proj(xt, g, w):
    T, D = xt.shape
    n_all = w.shape[1]
    n_gate = n_all - QKV_WIDTH
    tm = TM_PROJ
    vmem = _vmem_limit(_nbytes((tm, D), F32, 2), _nbytes((D, n_all), BF16),
                       _nbytes((tm, QKV_WIDTH), BF16, 2), _nbytes((tm, n_gate), F32, 2))
    return pl.pallas_call(
        functools.partial(_in_proj_kernel, rows=TR_PROJ, chunk=TN_PROJ),
        out_shape=(jax.ShapeDtypeStruct((T, QKV_WIDTH), BF16),
                   jax.ShapeDtypeStruct((T, n_gate), F32)),
        grid=(T // tm,),
        in_specs=[
            pl.BlockSpec((tm, D), lambda i: (i, 0)),
            pl.BlockSpec((1, D), lambda i: (0, 0)),
            pl.BlockSpec((D, n_all), lambda i: (0, 0), pipeline_mode=pl.Buffered(1)),
        ],
        out_specs=(pl.BlockSpec((tm, QKV_WIDTH), lambda i: (i, 0)),
                   pl.BlockSpec((tm, n_gate), lambda i: (i, 0))),
        compiler_params=pltpu.CompilerParams(
            dimension_semantics=("arbitrary",), vmem_limit_bytes=vmem),
        name="in_proj",
    )(xt, g.reshape(1, D), w)


def _na_geometry(rows):
    rb_rows = NA_ROW_BLOCK
    n_rb = rows // rb_rows
    win_rows = NA_WIN_BLOCKS * rb_rows
    wr = min(NA_WIN_ROWS, rows)
    kb = np.clip(np.arange(n_rb) - 1, 0, n_rb - NA_WIN_BLOCKS)
    dr_idx = np.zeros((n_rb, rb_rows, win_rows), np.int32)
    valid = np.zeros((n_rb, rb_rows, win_rows), bool)
    for rb in range(n_rb):
        w0 = kb[rb] * rb_rows
        for ri in range(rb_rows):
            r = rb * rb_rows + ri
            r0 = min(max(r - wr // 2, 0), rows - wr)
            assert w0 <= r0 and r0 + wr <= w0 + win_rows
            for wj in range(win_rows):
                krow = w0 + wj
                valid[rb, ri, wj] = r0 <= krow < r0 + wr
                dr_idx[rb, ri, wj] = min(max(krow - r + NA_WIN_ROWS - 1, 0), 2 * NA_WIN_ROWS - 2)
    for rb in range(2, n_rb - 1):
        assert (valid[rb] == valid[1]).all() and (dr_idx[rb] == dr_idx[1]).all()
    classes = [0, 1, n_rb - 1]
    return n_rb, kb, dr_idx[classes], valid[classes]


def _na_bias_kernel(rpb_ref, o_ref, *, dr_idx, row_valid):
    assert LANES == 2 * GRID_W
    n_cls, rb_rows, win_rows = dr_idx.shape
    c = lax.broadcasted_iota(jnp.int32, (GRID_W, LANES), 0)
    lane = lax.broadcasted_iota(jnp.int32, (GRID_W, LANES), 1)
    left = lane < GRID_W
    kc = jnp.where(left, lane, lane - GRID_W)
    c0 = jnp.clip(c - NA_WIN_COLS // 2, 0, GRID_W - NA_WIN_COLS)
    col_ok = (kc >= c0) & (kc < c0 + NA_WIN_COLS)
    neg = jnp.full((GRID_W, LANES), NEG_INF, F32)

    def half_tile(d, right):
        w = jnp.broadcast_to(rpb_ref[d:d + 1, :], (GRID_W, LANES))
        t = pltpu.roll(w, GRID_W if right else 0, 1, stride=1, stride_axis=0)
        return jnp.where(col_ok, t, neg)

    for cls in range(n_cls):
        for ri in range(rb_rows):
            for p in range(win_rows // 2):
                halves = []
                for side in range(2):
                    wj = 2 * p + side
                    ok = bool(row_valid[cls, ri, wj])
                    halves.append(half_tile(int(dr_idx[cls, ri, wj]), side == 1) if ok else neg)
                o_ref[cls, ri * GRID_W:(ri + 1) * GRID_W, p * LANES:(p + 1) * LANES] = (
                    jnp.where(left, halves[0], halves[1]))


def _na_bias_table(na_rpb, rows):
    depth, H, n_dr, n_dc = na_rpb.shape
    assert n_dr == 2 * NA_WIN_ROWS - 1 and n_dc == 2 * NA_WIN_COLS - 1
    _, _, dr_idx, row_valid = _na_geometry(rows)
    n_cls, rb_rows, win_rows = dr_idx.shape
    rq, wk = rb_rows * GRID_W, win_rows * GRID_W
    lanes = jnp.pad(na_rpb, ((0, 0), (0, 0), (0, 0), (0, LANES - n_dc)), constant_values=NEG_INF)
    lanes = jnp.roll(lanes, -(NA_WIN_COLS - 1), axis=-1)
    return pl.pallas_call(
        functools.partial(_na_bias_kernel, dr_idx=dr_idx, row_valid=row_valid),
        out_shape=jax.ShapeDtypeStruct((depth, n_cls, H, rq, wk), F32),
        grid=(depth, H),
        in_specs=[pl.BlockSpec((None, None, n_dr, LANES), lambda l, h: (l, h, 0, 0))],
        out_specs=pl.BlockSpec((None, n_cls, None, rq, wk), lambda l, h: (l, 0, h, 0, 0)),
        compiler_params=pltpu.CompilerParams(dimension_semantics=("arbitrary", "arbitrary")),
        name="na_bias",
    )(lanes)


def _na_kernel(q_ref, k0_ref, k1_ref, k2_ref, v0_ref, v1_ref, v2_ref, bias_ref, *rest):
    n_cast = len(rest) // 2
    o_ref = rest[n_cast]
    for w_ref, wb_ref in zip(rest[:n_cast], rest[n_cast + 1:]):
        wb_ref[...] = w_ref[...].astype(BF16)
    nb, rq = q_ref.shape[0], q_ref.shape[1]
    scale = 1.0 / math.sqrt(NA_HEAD_DIM)
    lo = lax.broadcasted_iota(jnp.int32, (rq, LANES), 1) < NA_HEAD_DIM
    heads_per_vreg = LANES // NA_HEAD_DIM
    for bi, hp in np.ndindex(nb, NA_HEADS // heads_per_vreg):
        cs = (bi, slice(None), slice(hp * LANES, (hp + 1) * LANES))
        q = q_ref[cs] * scale
        k = jnp.concatenate([k0_ref[cs], k1_ref[cs], k2_ref[cs]], axis=0)
        v = jnp.concatenate([v0_ref[cs], v1_ref[cs], v2_ref[cs]], axis=0)
        v_ones = jnp.concatenate([v, jnp.ones_like(v)], axis=1)
        outs = []
        for e in range(heads_per_vreg):
            qm = jnp.where(lo if e == 0 else jnp.logical_not(lo), q, jnp.zeros_like(q))
            s = lax.dot_general(qm, k, (((1,), (1,)), ((), ())), preferred_element_type=F32)
            s = s + bias_ref[heads_per_vreg * hp + e]
            m = jnp.max(s, axis=-1, keepdims=True)
            p = jnp.exp((s - m).astype(BF16))
            ol = jnp.dot(p, v_ones, preferred_element_type=F32)
            outs.append(ol[:, 0:LANES] / ol[:, LANES:2 * LANES])
        o_ref[cs] = jnp.where(lo, outs[0], outs[1]).astype(BF16)


def _na_attention(qkv3, bias_table, layer, casts):
    B, S, _ = qkv3.shape
    rows = S // GRID_W
    n_rb, kb, _, _ = _na_geometry(rows)
    rq = NA_ROW_BLOCK * GRID_W
    assert NA_HEADS * NA_HEAD_DIM == NA_WIDTH and LANES % NA_HEAD_DIM == 0
    k_col, v_col = 1, 2
    n_kb = n_rb - NA_WIN_BLOCKS

    nb = NA_BATCH_BLOCK
    assert B % nb == 0

    def kv_spec(col, t):
        return pl.BlockSpec((nb, rq, NA_WIDTH),
                            lambda rb, b: (b, jnp.clip(rb - 1, 0, n_kb) + t, col))

    def bias_map(rb, b):
        cls = (rb > 0).astype(jnp.int32) + (rb == n_rb - 1).astype(jnp.int32)
        return (layer, cls, 0, 0, 0)

    n_bb = B // nb
    n_steps = n_rb * n_bb
    cast_in, cast_in_specs, cast_out, cast_out_specs = [], [], [], []
    for w, w_layer in casts:
        depth, R, C = w.shape
        slab = R // n_steps
        assert R % n_steps == 0 and slab % BF16_SUBLANES == 0
        cast_in.append(w.reshape(depth * n_steps, slab, C))
        first = w_layer * n_steps
        cast_in_specs.append(pl.BlockSpec(
            (None, slab, C), lambda rb, b, first=first: (first + rb * n_bb + b, 0, 0)))
        cast_out.append(jax.ShapeDtypeStruct((n_steps, slab, C), BF16))
        cast_out_specs.append(pl.BlockSpec((None, slab, C), lambda rb, b: (rb * n_bb + b, 0, 0)))

    wk = NA_WIN_BLOCKS * rq
    n_qkv_windows = 2 + 2 * NA_WIN_BLOCKS
    vmem = _vmem_limit(
        n_qkv_windows * _nbytes((nb, rq, NA_WIDTH), BF16, 2),
        _nbytes((NA_HEADS, rq, wk), F32, 2),
        *[_nbytes(o.shape[1:], F32, 2) + _nbytes(o.shape[1:], BF16, 2) for o in cast_out])
    y, *cast = pl.pallas_call(
        _na_kernel,
        out_shape=[jax.ShapeDtypeStruct((B, S, NA_WIDTH), BF16)] + cast_out,
        grid=(n_rb, n_bb),
        in_specs=[pl.BlockSpec((nb, rq, NA_WIDTH), lambda rb, b: (b, rb, 0))]
        + [kv_spec(k_col, t) for t in range(NA_WIN_BLOCKS)]
        + [kv_spec(v_col, t) for t in range(NA_WIN_BLOCKS)]
        + [pl.BlockSpec((None, None, NA_HEADS, rq, wk), bias_map)]
        + cast_in_specs,
        out_specs=[pl.BlockSpec((nb, rq, NA_WIDTH), lambda rb, b: (b, rb, 0))] + cast_out_specs,
        compiler_params=pltpu.CompilerParams(
            dimension_semantics=("arbitrary", "arbitrary"), vmem_limit_bytes=vmem),
        name="na_attn",
    )(qkv3, qkv3, qkv3, qkv3, qkv3, qkv3, qkv3, bias_table, *cast_in)
    return y, [c.reshape(w.shape[1], w.shape[2]) for c, (w, _) in zip(cast, casts)]


def _t5_bucket_steps(seq):
    half = T5_BUCKETS // 2
    max_exact = half // 2
    rel = np.arange(-(seq - 1), seq)
    n = np.abs(rel)
    nf = np.maximum(n, 1).astype(np.float64)
    large = max_exact + np.floor(
        np.log(nf / max_exact) / math.log(T5_MAX_DIST / max_exact) * (half - max_exact) + 1e-9
    ).astype(np.int64)
    large = np.minimum(large, half - 1)
    bucket = np.where(rel > 0, half, 0) + np.where(n < max_exact, n, large)
    steps = [(int(rel[i]), int(bucket[i])) for i in range(1, len(rel)) if bucket[i] != bucket[i - 1]]
    sat = max(abs(steps[0][0]) + 1, abs(steps[-1][0]))
    return int(bucket[0]), steps, sat


def _diff_kernel(t5_ref, q_ref, k_ref, v_ref, lam_ref, g_ref, o_ref, *scratch, tq, n_buf,
                 lam_init, bucket_steps):
    s_refs, m_refs, ol_refs = (scratch[i * n_buf:(i + 1) * n_buf] for i in range(3))
    corr_ref, kfar_ref = scratch[3 * n_buf:]
    b = pl.program_id(0)
    seq = k_ref.shape[0]
    n_t = seq // tq
    band = 3 * tq
    first_bucket, steps, _ = bucket_steps
    last_bucket = steps[-1][1]
    split = tq + tq // 2
    one_lane = FAR_PARTS * n_t
    assert one_lane + FAR_PARTS <= LANES and n_t >= 4

    @pl.when(b == 0)
    def _init():
        row = lax.broadcasted_iota(jnp.int32, (tq, band), 0)
        col = lax.broadcasted_iota(jnp.int32, (tq, band), 1)
        rel = col - tq - row
        for hh in range(DIFF_HEADS):
            val = jnp.full((tq, band), t5_ref[first_bucket, hh], F32)
            for thr, bkt in steps:
                val = jnp.where(rel >= thr, t5_ref[bkt, hh], val)
            far = jnp.where(col < split, t5_ref[first_bucket, hh], t5_ref[last_bucket, hh])
            corr_ref[hh] = val - far
        corr_ref[DIFF_HEADS] = jnp.zeros((tq, band), F32)
        key = lax.broadcasted_iota(jnp.int32, (seq, LANES), 0)
        lane = lax.broadcasted_iota(jnp.int32, (seq, LANES), 1)
        tile = lax.rem(lane, n_t)
        step = jnp.where(key >= tile * tq + tq // 2, 1.0, 0.0)
        ones = jnp.where(lane < one_lane + FAR_PARTS, 1.0, 0.0)
        kfar_ref[...] = jnp.where(lane < one_lane, step, ones).astype(BF16)

    scale = 1.0 / math.sqrt(DIFF_QK_DIM)
    lane = lax.broadcasted_iota(jnp.int32, (tq, LANES), 1)
    lo = lane < DIFF_QK_DIM
    lp = lam_ref[...]
    lam = (jnp.exp(jnp.sum(lp[0:1] * lp[1:2], axis=-1, keepdims=True))
           - jnp.exp(jnp.sum(lp[2:3] * lp[3:4], axis=-1, keepdims=True)) + lam_init)

    chunk_order = [-1, 0, 1] + list(range(2, n_t - 1))

    def tile_ids(t):
        if isinstance(t, int):
            h, qi = divmod(t, n_t)
        else:
            h, qi = lax.div(t, n_t), lax.rem(t, n_t)
        return h, qi, pl.ds(_aligned(h * LANES, LANES), LANES)

    def key_rows(qi, d):
        chunk = (qi + d) % n_t if isinstance(qi, int) else lax.rem(qi + (d + n_t), n_t)
        return pl.ds(_aligned(chunk * tq, tq), tq)

    def scores(t, s_ref, m_ref):
        h, qi, cols = tile_ids(t)
        c_neg = t5_ref[first_bucket, h]
        c_step = t5_ref[last_bucket, h] - c_neg
        row0 = _aligned(qi * tq, tq)
        q = q_ref[pl.ds(row0, tq), cols] * scale
        zero = jnp.zeros_like(q)
        far = jnp.where(lane < one_lane, jnp.where(lax.rem(lane, n_t) == qi, c_step, 0.0),
                        jnp.where(lane < one_lane + FAR_PARTS, c_neg, 0.0))
        part = jnp.where(lane < one_lane, lax.div(lane, n_t), lane - one_lane)
        q_far = jnp.zeros((tq, LANES), BF16)
        for i in range(FAR_PARTS):
            piece = far.astype(BF16)
            q_far = jnp.where(part == i, piece, q_far)
            far = far - piece.astype(F32)
        lhs = jnp.concatenate(
            [jnp.concatenate([jnp.where(lo, q, zero), q_far], axis=1),
             jnp.concatenate([jnp.where(lo, zero, q), q_far], axis=1)], axis=0)
        m_run = None
        for pos, d in enumerate(chunk_order):
            rows = key_rows(qi, d)
            rhs = jnp.concatenate([k_ref[rows, cols], kfar_ref[rows, :]], axis=1)
            s = lax.dot_general(lhs, rhs, (((1,), (1,)), ((), ())), preferred_element_type=F32)
            if d in (-1, 0, 1):
                inside = (qi + d >= 0) & (qi + d < n_t)
                head = jnp.where(inside, h, DIFF_HEADS)
                corr = corr_ref[head, :, (d + 1) * tq:(d + 2) * tq]
                s = s + jnp.concatenate([corr, corr], axis=0)
            s_ref[:, pos * tq:(pos + 1) * tq] = s
            m_chunk = functools.reduce(
                jnp.maximum, [s[:, i * LANES:(i + 1) * LANES] for i in range(tq // LANES)])
            m_run = m_chunk if m_run is None else jnp.maximum(m_run, m_chunk)
        m_ref[...] = m_run

    def attend(t, s_ref, m_ref, ol_ref):
        _, qi, cols = tile_ids(t)
        m = jnp.broadcast_to(jnp.max(m_ref[...], axis=-1, keepdims=True), (2 * tq, LANES))
        m = jnp.concatenate([m] * (tq // LANES), axis=1)
        ol = None
        for pos, d in enumerate(chunk_order):
            e = jnp.exp((s_ref[:, pos * tq:(pos + 1) * tq] - m).astype(BF16))
            v = v_ref[key_rows(qi, d), cols]
            part = jnp.dot(e, jnp.concatenate([v, jnp.ones_like(v)], axis=1),
                           preferred_element_type=F32)
            ol = part if ol is None else ol + part
        ol_ref[...] = ol

    def finish(t, ol_ref):
        _, qi, cols = tile_ids(t)
        o = (ol_ref[0:tq, 0:LANES] / ol_ref[0:tq, LANES:2 * LANES]
             - lam * (ol_ref[tq:2 * tq, 0:LANES] / ol_ref[tq:2 * tq, LANES:2 * LANES]))
        row0 = _aligned(qi * tq, tq)
        o_ref[pl.ds(row0, tq), cols] = (
            _rmsnorm_f32(o, g_ref[...]) * (1.0 - lam_init)).astype(BF16)

    n_tiles = DIFF_HEADS * n_t

    def step(u, slot):
        live = (lambda t: True) if not isinstance(u, int) else (lambda t: 0 <= t < n_tiles)
        if live(u):
            scores(u, s_refs[slot], m_refs[slot])
        if live(u - 3):
            finish(u - 3, ol_refs[(slot - 3) % n_buf])
        if live(u - 2):
            a = (slot - 2) % n_buf
            attend(u - 2, s_refs[a], m_refs[a], ol_refs[a])

    first, trips = 3, (n_tiles - 3) // n_buf
    assert trips >= 1
    for u in range(first):
        step(u, u % n_buf)

    def rotation(j, carry):
        u = first + n_buf * j
        for i in range(n_buf):
            step(u + i, (first + i) % n_buf)
        return carry

    lax.fori_loop(0, trips, rotation, 0)
    for u in range(first + n_buf * trips, n_tiles + 3):
        step(u, u % n_buf)


def _diff_attention(qkv3, t5_bias, lam_params, subln_g, lam_init):
    B, S, _ = qkv3.shape
    tq = TQ_DIFF
    bucket_steps = _t5_bucket_steps(S)
    assert bucket_steps[2] <= tq and S % tq == 0 and tq % LANES == 0
    assert 2 * DIFF_QK_DIM == LANES and DIFF_V_DIM == LANES
    assert NA_WIDTH == DIFF_WIDTH
    q_col = 3
    n_buf = 3
    scratch = (
        [((2 * tq, S), F32)] * n_buf
        + [((2 * tq, LANES), F32)] * n_buf
        + [((2 * tq, 2 * LANES), F32)] * n_buf
        + [((DIFF_HEADS + 1, tq, 3 * tq), F32),
           ((S, LANES), BF16)])
    vmem = _vmem_limit(4 * _nbytes((S, DIFF_WIDTH), BF16, 2), *[_nbytes(*s) for s in scratch])
    return pl.pallas_call(
        functools.partial(_diff_kernel, tq=tq, n_buf=n_buf, lam_init=lam_init,
                          bucket_steps=bucket_steps),
        out_shape=jax.ShapeDtypeStruct((B, S, DIFF_WIDTH), BF16),
        grid=(B,),
        in_specs=[
            pl.BlockSpec(memory_space=pltpu.SMEM),
            pl.BlockSpec((None, S, DIFF_WIDTH), lambda b: (b, 0, q_col)),
            pl.BlockSpec((None, S, DIFF_WIDTH), lambda b: (b, 0, q_col + 1)),
            pl.BlockSpec((None, S, DIFF_WIDTH), lambda b: (b, 0, q_col + 2)),
            pl.BlockSpec((4, DIFF_QK_DIM), lambda b: (0, 0)),
            pl.BlockSpec((1, DIFF_V_DIM), lambda b: (0, 0)),
        ],
        out_specs=pl.BlockSpec((None, S, DIFF_WIDTH), lambda b: (b, 0, 0)),
        scratch_shapes=[pltpu.VMEM(*s) for s in scratch],
        compiler_params=pltpu.CompilerParams(
            dimension_semantics=("arbitrary",), vmem_limit_bytes=vmem),
        name="diff_attn",
    )(t5_bias, qkv3, qkv3, qkv3, lam_params, subln_g.reshape(1, DIFF_V_DIM))


def _mix_mlp_kernel(x_ref, yna_ref, ydf_ref, gate_ref, wna_ref, wdf_ref, wout_ref, g_ref, w1_ref,
                    w2_ref, fg_ref, o_ref, *, tf, final):
    d = x_ref.shape[1]
    b_na = jnp.dot(yna_ref[...], wna_ref[...], preferred_element_type=F32)
    b_df = jnp.dot(ydf_ref[...], wdf_ref[...], preferred_element_type=F32)
    merged = (jax.nn.sigmoid(gate_ref[:, 0:d]) * b_na
              + jax.nn.sigmoid(gate_ref[:, d:2 * d]) * b_df)
    x = x_ref[...] + jnp.dot(merged.astype(BF16), wout_ref[...], preferred_element_type=F32)
    h = _rmsnorm_f32(x, g_ref[...]).astype(BF16)
    for c in range(w1_ref.shape[1] // tf):
        u = jnp.dot(h, w1_ref[:, c * tf:(c + 1) * tf], preferred_element_type=F32)
        u = jnp.square(jnp.maximum(u, 0.0)).astype(BF16)
        x = x + jnp.dot(u, w2_ref[c * tf:(c + 1) * tf, :], preferred_element_type=F32)
    o_ref[...] = _rmsnorm_f32(x, fg_ref[...]) if final else x


def _mix_mlp(xt, y_na, y_df, gates, w_na, w_df, w_out, g, w1, w2, final_g, final):
    T, D = xt.shape
    F = w1.shape[1]
    tm, tf = TM_MLP, TF_MLP
    once = pl.Buffered(1)
    assert F % tf == 0
    weights = (w_na, w_df, w_out, w1, w2)
    vmem = _vmem_limit(
        2 * _nbytes((tm, D), F32, 2),
        _nbytes((tm, NA_WIDTH), BF16, 2), _nbytes((tm, DIFF_WIDTH), BF16, 2),
        _nbytes((tm, 2 * D), F32, 2), *[_nbytes(w.shape, BF16) for w in weights])
    return pl.pallas_call(
        functools.partial(_mix_mlp_kernel, tf=tf, final=final),
        out_shape=jax.ShapeDtypeStruct((T, D), F32),
        grid=(T // tm,),
        in_specs=[
            pl.BlockSpec((tm, D), lambda i: (i, 0)),
            pl.BlockSpec((tm, NA_WIDTH), lambda i: (i, 0)),
            pl.BlockSpec((tm, DIFF_WIDTH), lambda i: (i, 0)),
            pl.BlockSpec((tm, 2 * D), lambda i: (i, 0)),
            pl.BlockSpec((NA_WIDTH, D), lambda i: (0, 0), pipeline_mode=once),
            pl.BlockSpec((DIFF_WIDTH, D), lambda i: (0, 0), pipeline_mode=once),
            pl.BlockSpec((D, D), lambda i: (0, 0), pipeline_mode=once),
            pl.BlockSpec((1, D), lambda i: (0, 0)),
            pl.BlockSpec((D, F), lambda i: (0, 0), pipeline_mode=once),
            pl.BlockSpec((F, D), lambda i: (0, 0), pipeline_mode=once),
            pl.BlockSpec((1, D), lambda i: (0, 0)),
        ],
        out_specs=pl.BlockSpec((tm, D), lambda i: (i, 0)),
        compiler_params=pltpu.CompilerParams(
            dimension_semantics=("arbitrary",), vmem_limit_bytes=vmem),
        name="mix_mlp",
    )(xt, y_na, y_df, gates, w_na, w_df, w_out, g.reshape(1, D), w1, w2, final_g.reshape(1, D))


def kernel(x, t5_bias, final_norm_g, norm1_g, w_in, na_rpb, diff_lambda, diff_subln_g, w_na_o,
           w_diff_o, w_out, norm2_g, w_ff1, w_ff2):
    B, S, D = x.shape
    depth = w_in.shape[0]
    T = B * S
    assert w_in.shape[2] == QKV_WIDTH + 2 * D and S % GRID_W == 0
    xt = x.reshape(T, D)
    na_bias = _na_bias_table(na_rpb, S // GRID_W)
    w_in_b = w_in[0].astype(BF16)
    for layer in range(depth):
        qkv, gates = _in_proj(xt, norm1_g[layer], w_in_b)
        qkv3 = qkv.reshape(B, S, QKV_WIDTH)
        casts = [(w, layer) for w in (w_na_o, w_diff_o, w_out, w_ff1, w_ff2)]
        if layer + 1 < depth:
            casts.append((w_in, layer + 1))
        y_na, (w_na_b, w_df_b, w_out_b, w1_b, w2_b, *w_in_next) = _na_attention(
            qkv3, na_bias, layer, casts)
        lam_init = 0.8 - 0.6 * math.exp(-0.3 * layer)
        y_df = _diff_attention(qkv3, t5_bias, diff_lambda[layer], diff_subln_g[layer], lam_init)
        xt = _mix_mlp(xt, y_na.reshape(T, NA_WIDTH), y_df.reshape(T, DIFF_WIDTH), gates,
                      w_na_b, w_df_b, w_out_b, norm2_g[layer], w1_b, w2_b, final_norm_g,
                      final=(layer == depth - 1))
        if w_in_next:
            w_in_b = w_in_next[0]
    return xt.reshape(B, S, D)
```

```python
import functools
import math

import numpy as np
import jax
import jax.numpy as jnp
from jax import lax
from jax.experimental import pallas as pl
from jax.experimental.pallas import tpu as pltpu

GRID_W = 64
NA_HEADS = 8
NA_HEAD_DIM = 64
NA_WIN_ROWS = 8
NA_WIN_COLS = 16
DIFF_HEADS = 4
DIFF_QK_DIM = 64
DIFF_V_DIM = 2 * DIFF_QK_DIM
T5_BUCKETS = 32
T5_MAX_DIST = 128
RMS_EPS = 1e-6
NEG_INF = -1e30

NA_WIDTH = NA_HEADS * NA_HEAD_DIM
DIFF_WIDTH = DIFF_HEADS * DIFF_V_DIM
QKV_WIDTH = 3 * NA_WIDTH + 3 * DIFF_WIDTH

LANES = 128
BF16_SUBLANES = 16
V7X_VMEM_BYTES = 64 * 2**20
VMEM_TEMP_BYTES = 12 * 2**20

TM_PROJ = 1024
TR_PROJ = 512
TN_PROJ = 1024
TM_MLP = 512
TF_MLP = 1024
NA_ROW_BLOCK = 4
NA_WIN_BLOCKS = 3
NA_BATCH_BLOCK = 4
TQ_DIFF = 256
FAR_PARTS = 3

BF16 = jnp.bfloat16
F32 = jnp.float32


def _nbytes(shape, dtype, buffers=1):
    return buffers * math.prod(shape) * jnp.dtype(dtype).itemsize


def _vmem_limit(*window_bytes):
    total = sum(window_bytes) + VMEM_TEMP_BYTES
    assert total < V7X_VMEM_BYTES
    return int(total)


def _aligned(index, multiple):
    return index if isinstance(index, int) else pl.multiple_of(index, multiple)


def _rmsnorm_f32(x, g):
    return (x * lax.rsqrt(jnp.mean(x * x, axis=-1, keepdims=True) + RMS_EPS)) * g


def _in_proj_kernel(x_ref, g_ref, w_ref, qkv_ref, gate_ref, *, rows, chunk):
    n_qkv = qkv_ref.shape[1]
    n_gate = gate_ref.shape[1]
    for r in range(x_ref.shape[0] // rows):
        rs = slice(r * rows, (r + 1) * rows)
        hb = _rmsnorm_f32(x_ref[rs, :], g_ref[...]).astype(BF16)
        for c in range(n_qkv // chunk):
            cs = slice(c * chunk, (c + 1) * chunk)
            qkv_ref[rs, cs] = jnp.dot(hb, w_ref[:, cs], preferred_element_type=F32).astype(BF16)
        for c in range(n_gate // chunk):
            cs = slice(c * chunk, (c + 1) * chunk)
            ws = slice(n_qkv + c * chunk, n_qkv + (c + 1) * chunk)
            gate_ref[rs, cs] = jnp.dot(hb, w_ref[:, ws], preferred_element_type=F32)


def _in_proj(xt, g, w):
    T, D = xt.shape
    n_all = w.shape[1]
    n_gate = n_all - QKV_WIDTH
    tm = TM_PROJ
    vmem = _vmem_limit(_nbytes((tm, D), F32, 2), _nbytes((D, n_all), BF16),
                       _nbytes((tm, QKV_WIDTH), BF16, 2), _nbytes((tm, n_gate), F32, 2))
    return pl.pallas_call(
        functools.partial(_in_proj_kernel, rows=TR_PROJ, chunk=TN_PROJ),
        out_shape=(jax.ShapeDtypeStruct((T, QKV_WIDTH), BF16),
                   jax.ShapeDtypeStruct((T, n_gate), F32)),
        grid=(T // tm,),
        in_specs=[
            pl.BlockSpec((tm, D), lambda i: (i, 0)),
            pl.BlockSpec((1, D), lambda i: (0, 0)),
            pl.BlockSpec((D, n_all), lambda i: (0, 0), pipeline_mode=pl.Buffered(1)),
        ],
        out_specs=(pl.BlockSpec((tm, QKV_WIDTH), lambda i: (i, 0)),
                   pl.BlockSpec((tm, n_gate), lambda i: (i, 0))),
        compiler_params=pltpu.CompilerParams(
            dimension_semantics=("arbitrary",), vmem_limit_bytes=vmem),
        name="in_proj",
    )(xt, g.reshape(1, D), w)


def _na_geometry(rows):
    rb_rows = NA_ROW_BLOCK
    n_rb = rows // rb_rows
    win_rows = NA_WIN_BLOCKS * rb_rows
    wr = min(NA_WIN_ROWS, rows)
    kb = np.clip(np.arange(n_rb) - 1, 0, n_rb - NA_WIN_BLOCKS)
    dr_idx = np.zeros((n_rb, rb_rows, win_rows), np.int32)
    valid = np.zeros((n_rb, rb_rows, win_rows), bool)
    for rb in range(n_rb):
        w0 = kb[rb] * rb_rows
        for ri in range(rb_rows):
            r = rb * rb_rows + ri
            r0 = min(max(r - wr // 2, 0), rows - wr)
            assert w0 <= r0 and r0 + wr <= w0 + win_rows
            for wj in range(win_rows):
                krow = w0 + wj
                valid[rb, ri, wj] = r0 <= krow < r0 + wr
                dr_idx[rb, ri, wj] = min(max(krow - r + NA_WIN_ROWS - 1, 0), 2 * NA_WIN_ROWS - 2)
    for rb in range(2, n_rb - 1):
        assert (valid[rb] == valid[1]).all() and (dr_idx[rb] == dr_idx[1]).all()
    classes = [0, 1, n_rb - 1]
    return n_rb, kb, dr_idx[classes], valid[classes]


def _na_bias_kernel(rpb_ref, o_ref, *, dr_idx, row_valid):
    assert LANES == 2 * GRID_W
    n_cls, rb_rows, win_rows = dr_idx.shape
    c = lax.broadcasted_iota(jnp.int32, (GRID_W, LANES), 0)
    lane = lax.broadcasted_iota(jnp.int32, (GRID_W, LANES), 1)
    left = lane < GRID_W
    kc = jnp.where(left, lane, lane - GRID_W)
    c0 = jnp.clip(c - NA_WIN_COLS // 2, 0, GRID_W - NA_WIN_COLS)
    col_ok = (kc >= c0) & (kc < c0 + NA_WIN_COLS)
    neg = jnp.full((GRID_W, LANES), NEG_INF, F32)

    @functools.lru_cache(maxsize=None)
    def half_tile(d, right):
        w = jnp.broadcast_to(rpb_ref[d:d + 1, :], (GRID_W, LANES))
        t = pltpu.roll(w, GRID_W if right else 0, 1, stride=1, stride_axis=0)
        return jnp.where(col_ok, t, neg)

    for cls in range(n_cls):
        for ri in range(rb_rows):
            for p in range(win_rows // 2):
                halves = []
                for side in range(2):
                    wj = 2 * p + side
                    ok = bool(row_valid[cls, ri, wj])
                    halves.append(half_tile(int(dr_idx[cls, ri, wj]), side == 1) if ok else neg)
                o_ref[cls, ri * GRID_W:(ri + 1) * GRID_W, p * LANES:(p + 1) * LANES] = (
                    jnp.where(left, halves[0], halves[1]))


def _na_bias_table(na_rpb, rows):
    depth, H, n_dr, n_dc = na_rpb.shape
    assert n_dr == 2 * NA_WIN_ROWS - 1 and n_dc == 2 * NA_WIN_COLS - 1
    _, _, dr_idx, row_valid = _na_geometry(rows)
    n_cls, rb_rows, win_rows = dr_idx.shape
    rq, wk = rb_rows * GRID_W, win_rows * GRID_W
    lanes = jnp.pad(na_rpb, ((0, 0), (0, 0), (0, 0), (0, LANES - n_dc)), constant_values=NEG_INF)
    lanes = jnp.roll(lanes, -(NA_WIN_COLS - 1), axis=-1)
    return pl.pallas_call(
        functools.partial(_na_bias_kernel, dr_idx=dr_idx, row_valid=row_valid),
        out_shape=jax.ShapeDtypeStruct((depth, n_cls, H, rq, wk), F32),
        grid=(depth, H),
        in_specs=[pl.BlockSpec((None, None, n_dr, LANES), lambda l, h: (l, h, 0, 0))],
        out_specs=pl.BlockSpec((None, n_cls, None, rq, wk), lambda l, h: (l, 0, h, 0, 0)),
        compiler_params=pltpu.CompilerParams(dimension_semantics=("arbitrary", "arbitrary")),
        name="na_bias",
    )(lanes)


def _na_kernel(q_ref, k0_ref, k1_ref, k2_ref, v0_ref, v1_ref, v2_ref, bias_ref, *rest):
    n_cast = len(rest) // 2
    o_ref = rest[n_cast]
    for w_ref, wb_ref in zip(rest[:n_cast], rest[n_cast + 1:]):
        wb_ref[...] = w_ref[...].astype(BF16)
    nb, rq = q_ref.shape[0], q_ref.shape[1]
    scale = 1.0 / math.sqrt(NA_HEAD_DIM)
    lo = lax.broadcasted_iota(jnp.int32, (rq, LANES), 1) < NA_HEAD_DIM
    heads_per_vreg = LANES // NA_HEAD_DIM
    for bi, hp in np.ndindex(nb, NA_HEADS // heads_per_vreg):
        cs = (bi, slice(None), slice(hp * LANES, (hp + 1) * LANES))
        q = q_ref[cs] * scale
        k = jnp.concatenate([k0_ref[cs], k1_ref[cs], k2_ref[cs]], axis=0)
        v = jnp.concatenate([v0_ref[cs], v1_ref[cs], v2_ref[cs]], axis=0)
        v_ones = jnp.concatenate([v, jnp.ones_like(v)], axis=1)
        outs = []
        for e in range(heads_per_vreg):
            qm = jnp.where(lo if e == 0 else jnp.logical_not(lo), q, jnp.zeros_like(q))
            s = lax.dot_general(qm, k, (((1,), (1,)), ((), ())), preferred_element_type=F32)
            s = s + bias_ref[heads_per_vreg * hp + e]
            m = jnp.max(s, axis=-1, keepdims=True)
            p = jnp.exp((s - m).astype(BF16))
            ol = jnp.dot(p, v_ones, preferred_element_type=F32)
            outs.append(ol[:, 0:LANES] / ol[:, LANES:2 * LANES])
        o_ref[cs] = jnp.where(lo, outs[0], outs[1]).astype(BF16)


def _na_attention(qkv3, bias_table, layer, casts):
    B, S, _ = qkv3.shape
    rows = S // GRID_W
    n_rb, kb, _, _ = _na_geometry(rows)
    rq = NA_ROW_BLOCK * GRID_W
    assert NA_HEADS * NA_HEAD_DIM == NA_WIDTH and LANES % NA_HEAD_DIM == 0
    k_col, v_col = 1, 2
    n_kb = n_rb - NA_WIN_BLOCKS

    nb = NA_BATCH_BLOCK
    assert B % nb == 0

    def kv_spec(col, t):
        return pl.BlockSpec((nb, rq, NA_WIDTH),
                            lambda rb, b: (b, jnp.clip(rb - 1, 0, n_kb) + t, col))

    def bias_map(rb, b):
        cls = (rb > 0).astype(jnp.int32) + (rb == n_rb - 1).astype(jnp.int32)
        return (layer, cls, 0, 0, 0)

    n_bb = B // nb
    n_steps = n_rb * n_bb
    cast_in, cast_in_specs, cast_out, cast_out_specs = [], [], [], []
    for w, w_layer in casts:
        depth, R, C = w.shape
        slab = R // n_steps
        assert R % n_steps == 0 and slab % BF16_SUBLANES == 0
        cast_in.append(w.reshape(depth * n_steps, slab, C))
        first = w_layer * n_steps
        cast_in_specs.append(pl.BlockSpec(
            (None, slab, C), lambda rb, b, first=first: (first + rb * n_bb + b, 0, 0)))
        cast_out.append(jax.ShapeDtypeStruct((n_steps, slab, C), BF16))
        cast_out_specs.append(pl.BlockSpec((None, slab, C), lambda rb, b: (rb * n_bb + b, 0, 0)))

    wk = NA_WIN_BLOCKS * rq
    n_qkv_windows = 2 + 2 * NA_WIN_BLOCKS
    vmem = _vmem_limit(
        n_qkv_windows * _nbytes((nb, rq, NA_WIDTH), BF16, 2),
        _nbytes((NA_HEADS, rq, wk), F32, 2),
        *[_nbytes(o.shape[1:], F32, 2) + _nbytes(o.shape[1:], BF16, 2) for o in cast_out])
    y, *cast = pl.pallas_call(
        _na_kernel,
        out_shape=[jax.ShapeDtypeStruct((B, S, NA_WIDTH), BF16)] + cast_out,
        grid=(n_rb, n_bb),
        in_specs=[pl.BlockSpec((nb, rq, NA_WIDTH), lambda rb, b: (b, rb, 0))]
        + [kv_spec(k_col, t) for t in range(NA_WIN_BLOCKS)]
        + [kv_spec(v_col, t) for t in range(NA_WIN_BLOCKS)]
        + [pl.BlockSpec((None, None, NA_HEADS, rq, wk), bias_map)]
        + cast_in_specs,
        out_specs=[pl.BlockSpec((nb, rq, NA_WIDTH), lambda rb, b: (b, rb, 0))] + cast_out_specs,
        compiler_params=pltpu.CompilerParams(
            dimension_semantics=("arbitrary", "arbitrary"), vmem_limit_bytes=vmem),
        name="na_attn",
    )(qkv3, qkv3, qkv3, qkv3, qkv3, qkv3, qkv3, bias_table, *cast_in)
    return y, [c.reshape(w.shape[1], w.shape[2]) for c, (w, _) in zip(cast, casts)]


def _t5_bucket_steps(seq):
    half = T5_BUCKETS // 2
    max_exact = half // 2
    rel = np.arange(-(seq - 1), seq)
    n = np.abs(rel)
    nf = np.maximum(n, 1).astype(np.float64)
    large = max_exact + np.floor(
        np.log(nf / max_exact) / math.log(T5_MAX_DIST / max_exact) * (half - max_exact) + 1e-9
    ).astype(np.int64)
    large = np.minimum(large, half - 1)
    bucket = np.where(rel > 0, half, 0) + np.where(n < max_exact, n, large)
    steps = [(int(rel[i]), int(bucket[i])) for i in range(1, len(rel)) if bucket[i] != bucket[i - 1]]
    sat = max(abs(steps[0][0]) + 1, abs(steps[-1][0]))
    return int(bucket[0]), steps, sat


def _diff_kernel(t5_ref, q_ref, k_ref, v_ref, lam_ref, g_ref, o_ref, *scratch, tq, n_buf,
                 lam_init, bucket_steps):
    s_refs, m_refs, ol_refs = (scratch[i * n_buf:(i + 1) * n_buf] for i in range(3))
    corr_ref, kfar_ref = scratch[3 * n_buf:]
    b = pl.program_id(0)
    seq = k_ref.shape[0]
    n_t = seq // tq
    band = 3 * tq
    first_bucket, steps, _ = bucket_steps
    last_bucket = steps[-1][1]
    split = tq + tq // 2
    one_lane = FAR_PARTS * n_t
    assert one_lane + FAR_PARTS <= LANES and n_t >= 4

    @pl.when(b == 0)
    def _init():
        row = lax.broadcasted_iota(jnp.int32, (tq, band), 0)
        col = lax.broadcasted_iota(jnp.int32, (tq, band), 1)
        rel = col - tq - row
        for hh in range(DIFF_HEADS):
            val = jnp.full((tq, band), t5_ref[first_bucket, hh], F32)
            for thr, bkt in steps:
                val = jnp.where(rel >= thr, t5_ref[bkt, hh], val)
            far = jnp.where(col < split, t5_ref[first_bucket, hh], t5_ref[last_bucket, hh])
            corr_ref[hh] = val - far
        corr_ref[DIFF_HEADS] = jnp.zeros((tq, band), F32)
        key = lax.broadcasted_iota(jnp.int32, (seq, LANES), 0)
        lane = lax.broadcasted_iota(jnp.int32, (seq, LANES), 1)
        tile = lax.rem(lane, n_t)
        step = jnp.where(key >= tile * tq + tq // 2, 1.0, 0.0)
        ones = jnp.where(lane < one_lane + FAR_PARTS, 1.0, 0.0)
        kfar_ref[...] = jnp.where(lane < one_lane, step, ones).astype(BF16)

    scale = 1.0 / math.sqrt(DIFF_QK_DIM)
    lane = lax.broadcasted_iota(jnp.int32, (tq, LANES), 1)
    lo = lane < DIFF_QK_DIM
    lp = lam_ref[...]
    lam = (jnp.exp(jnp.sum(lp[0:1] * lp[1:2], axis=-1, keepdims=True))
           - jnp.exp(jnp.sum(lp[2:3] * lp[3:4], axis=-1, keepdims=True)) + lam_init)

    chunk_order = [-1, 0, 1] + list(range(2, n_t - 1))

    def tile_ids(t):
        if isinstance(t, int):
            h, qi = divmod(t, n_t)
        else:
            h, qi = lax.div(t, n_t), lax.rem(t, n_t)
        return h, qi, pl.ds(_aligned(h * LANES, LANES), LANES)

    def key_rows(qi, d):
        chunk = (qi + d) % n_t if isinstance(qi, int) else lax.rem(qi + (d + n_t), n_t)
        return pl.ds(_aligned(chunk * tq, tq), tq)

    def scores(t, s_ref, m_ref):
        h, qi, cols = tile_ids(t)
        c_neg = t5_ref[first_bucket, h]
        c_step = t5_ref[last_bucket, h] - c_neg
        row0 = _aligned(qi * tq, tq)
        q = q_ref[pl.ds(row0, tq), cols] * scale
        zero = jnp.zeros_like(q)
        far = jnp.where(lane < one_lane, jnp.where(lax.rem(lane, n_t) == qi, c_step, 0.0),
                        jnp.where(lane < one_lane + FAR_PARTS, c_neg, 0.0))
        part = jnp.where(lane < one_lane, lax.div(lane, n_t), lane - one_lane)
        q_far = jnp.zeros((tq, LANES), BF16)
        for i in range(FAR_PARTS):
            piece = far.astype(BF16)
            q_far = jnp.where(part == i, piece, q_far)
            far = far - piece.astype(F32)
        lhs = jnp.concatenate(
            [jnp.concatenate([jnp.where(lo, q, zero), q_far], axis=1),
             jnp.concatenate([jnp.where(lo, zero, q), q_far], axis=1)], axis=0)
        m_run = None
        for pos, d in enumerate(chunk_order):
            rows = key_rows(qi, d)
            rhs = jnp.concatenate([k_ref[rows, cols], kfar_ref[rows, :]], axis=1)
            s = lax.dot_general(lhs, rhs, (((1,), (1,)), ((), ())), preferred_element_type=F32)
            if d in (-1, 0, 1):
                inside = (qi + d >= 0) & (qi + d < n_t)
                head = jnp.where(inside, h, DIFF_HEADS)
                corr = corr_ref[head, :, (d + 1) * tq:(d + 2) * tq]
                s = s + jnp.concatenate([corr, corr], axis=0)
            s_ref[:, pos * tq:(pos + 1) * tq] = s
            m_chunk = functools.reduce(
                jnp.maximum, [s[:, i * LANES:(i + 1) * LANES] for i in range(tq // LANES)])
            m_run = m_chunk if m_run is None else jnp.maximum(m_run, m_chunk)
        m_ref[...] = m_run

    def attend(t, s_ref, m_ref, ol_ref):
        _, qi, cols = tile_ids(t)
        m = jnp.broadcast_to(jnp.max(m_ref[...], axis=-1, keepdims=True), (2 * tq, LANES))
        m = jnp.concatenate([m] * (tq // LANES), axis=1)
        ol = None
        for pos, d in enumerate(chunk_order):
            e = jnp.exp((s_ref[:, pos * tq:(pos + 1) * tq] - m).astype(BF16))
            v = v_ref[key_rows(qi, d), cols]
            part = jnp.dot(e, jnp.concatenate([v, jnp.ones_like(v)], axis=1),
                           preferred_element_type=F32)
            ol = part if ol is None else ol + part
        ol_ref[...] = ol

    def finish(t, ol_ref):
        _, qi, cols = tile_ids(t)
        o = (ol_ref[0:tq, 0:LANES] / ol_ref[0:tq, LANES:2 * LANES]
             - lam * (ol_ref[tq:2 * tq, 0:LANES] / ol_ref[tq:2 * tq, LANES:2 * LANES]))
        row0 = _aligned(qi * tq, tq)
        o_ref[pl.ds(row0, tq), cols] = (
            _rmsnorm_f32(o, g_ref[...]) * (1.0 - lam_init)).astype(BF16)

    n_tiles = DIFF_HEADS * n_t

    def step(u, slot):
        live = (lambda t: True) if not isinstance(u, int) else (lambda t: 0 <= t < n_tiles)
        if live(u):
            scores(u, s_refs[slot], m_refs[slot])
        if live(u - 3):
            finish(u - 3, ol_refs[(slot - 3) % n_buf])
        if live(u - 2):
            a = (slot - 2) % n_buf
            attend(u - 2, s_refs[a], m_refs[a], ol_refs[a])

    first, trips = 3, (n_tiles - 3) // n_buf
    assert trips >= 1
    for u in range(first):
        step(u, u % n_buf)

    def rotation(j, carry):
        u = first + n_buf * j
        for i in range(n_buf):
            step(u + i, (first + i) % n_buf)
        return carry

    lax.fori_loop(0, trips, rotation, 0)
    for u in range(first + n_buf * trips, n_tiles + 3):
        step(u, u % n_buf)


def _diff_attention(qkv3, t5_bias, lam_params, subln_g, lam_init):
    B, S, _ = qkv3.shape
    tq = TQ_DIFF
    bucket_steps = _t5_bucket_steps(S)
    assert bucket_steps[2] <= tq and S % tq == 0 and tq % LANES == 0
    assert 2 * DIFF_QK_DIM == LANES and DIFF_V_DIM == LANES
    assert NA_WIDTH == DIFF_WIDTH
    q_col = 3
    n_buf = 3
    scratch = (
        [((2 * tq, S), F32)] * n_buf
        + [((2 * tq, LANES), F32)] * n_buf
        + [((2 * tq, 2 * LANES), F32)] * n_buf
        + [((DIFF_HEADS + 1, tq, 3 * tq), F32),
           ((S, LANES), BF16)])
    vmem = _vmem_limit(4 * _nbytes((S, DIFF_WIDTH), BF16, 2), *[_nbytes(*s) for s in scratch])
    return pl.pallas_call(
        functools.partial(_diff_kernel, tq=tq, n_buf=n_buf, lam_init=lam_init,
                          bucket_steps=bucket_steps),
        out_shape=jax.ShapeDtypeStruct((B, S, DIFF_WIDTH), BF16),
        grid=(B,),
        in_specs=[
            pl.BlockSpec(memory_space=pltpu.SMEM),
            pl.BlockSpec((None, S, DIFF_WIDTH), lambda b: (b, 0, q_col)),
            pl.BlockSpec((None, S, DIFF_WIDTH), lambda b: (b, 0, q_col + 1)),
            pl.BlockSpec((None, S, DIFF_WIDTH), lambda b: (b, 0, q_col + 2)),
            pl.BlockSpec((4, DIFF_QK_DIM), lambda b: (0, 0)),
            pl.BlockSpec((1, DIFF_V_DIM), lambda b: (0, 0)),
        ],
        out_specs=pl.BlockSpec((None, S, DIFF_WIDTH), lambda b: (b, 0, 0)),
        scratch_shapes=[pltpu.VMEM(*s) for s in scratch],
        compiler_params=pltpu.CompilerParams(
            dimension_semantics=("arbitrary",), vmem_limit_bytes=vmem),
        name="diff_attn",
    )(t5_bias, qkv3, qkv3, qkv3, lam_params, subln_g.reshape(1, DIFF_V_DIM))


def _mix_mlp_kernel(x_ref, yna_ref, ydf_ref, gate_ref, wna_ref, wdf_ref, wout_ref, g_ref, w1_ref,
                    w2_ref, fg_ref, o_ref, *, tf, final):
    d = x_ref.shape[1]
    b_na = jnp.dot(yna_ref[...], wna_ref[...], preferred_element_type=F32)
    b_df = jnp.dot(ydf_ref[...], wdf_ref[...], preferred_element_type=F32)
    merged = (jax.nn.sigmoid(gate_ref[:, 0:d]) * b_na
              + jax.nn.sigmoid(gate_ref[:, d:2 * d]) * b_df)
    x = x_ref[...] + jnp.dot(merged.astype(BF16), wout_ref[...], preferred_element_type=F32)
    h = _rmsnorm_f32(x, g_ref[...]).astype(BF16)
    for c in range(w1_ref.shape[1] // tf):
        u = jnp.dot(h, w1_ref[:, c * tf:(c + 1) * tf], preferred_element_type=F32)
        u = jnp.square(jnp.maximum(u, 0.0)).astype(BF16)
        x = x + jnp.dot(u, w2_ref[c * tf:(c + 1) * tf, :], preferred_element_type=F32)
    o_ref[...] = _rmsnorm_f32(x, fg_ref[...]) if final else x


def _mix_mlp(xt, y_na, y_df, gates, w_na, w_df, w_out, g, w1, w2, final_g, final):
    T, D = xt.shape
    F = w1.shape[1]
    tm, tf = TM_MLP, TF_MLP
    once = pl.Buffered(1)
    assert F % tf == 0
    weights = (w_na, w_df, w_out, w1, w2)
    vmem = _vmem_limit(
        2 * _nbytes((tm, D), F32, 2),
        _nbytes((tm, NA_WIDTH), BF16, 2), _nbytes((tm, DIFF_WIDTH), BF16, 2),
        _nbytes((tm, 2 * D), F32, 2), *[_nbytes(w.shape, BF16) for w in weights])
    return pl.pallas_call(
        functools.partial(_mix_mlp_kernel, tf=tf, final=final),
        out_shape=jax.ShapeDtypeStruct((T, D), F32),
        grid=(T // tm,),
        in_specs=[
            pl.BlockSpec((tm, D), lambda i: (i, 0)),
            pl.BlockSpec((tm, NA_WIDTH), lambda i: (i, 0)),
            pl.BlockSpec((tm, DIFF_WIDTH), lambda i: (i, 0)),
            pl.BlockSpec((tm, 2 * D), lambda i: (i, 0)),
            pl.BlockSpec((NA_WIDTH, D), lambda i: (0, 0), pipeline_mode=once),
            pl.BlockSpec((DIFF_WIDTH, D), lambda i: (0, 0), pipeline_mode=once),
            pl.BlockSpec((D, D), lambda i: (0, 0), pipeline_mode=once),
            pl.BlockSpec((1, D), lambda i: (0, 0)),
            pl.BlockSpec((D, F), lambda i: (0, 0), pipeline_mode=once),
            pl.BlockSpec((F, D), lambda i: (0, 0), pipeline_mode=once),
            pl.BlockSpec((1, D), lambda i: (0, 0)),
        ],
        out_specs=pl.BlockSpec((tm, D), lambda i: (i, 0)),
        compiler_params=pltpu.CompilerParams(
            dimension_semantics=("arbitrary",), vmem_limit_bytes=vmem),
        name="mix_mlp",
    )(xt, y_na, y_df, gates, w_na, w_df, w_out, g.reshape(1, D), w1, w2, final_g.reshape(1, D))


def kernel(x, t5_bias, final_norm_g, norm1_g, w_in, na_rpb, diff_lambda, diff_subln_g, w_na_o,
           w_diff_o, w_out, norm2_g, w_ff1, w_ff2):
    B, S, D = x.shape
    depth = w_in.shape[0]
    T = B * S
    assert w_in.shape[2] == QKV_WIDTH + 2 * D and S % GRID_W == 0
    xt = x.reshape(T, D)
    na_bias = _na_bias_table(na_rpb, S // GRID_W)
    w_in_b = w_in[0].astype(BF16)
    for layer in range(depth):
        qkv, gates = _in_proj(xt, norm1_g[layer], w_in_b)
        qkv3 = qkv.reshape(B, S, QKV_WIDTH)
        casts = [(w, layer) for w in (w_na_o, w_diff_o, w_out, w_ff1, w_ff2)]
        if layer + 1 < depth:
            casts.append((w_in, layer + 1))
        y_na, (w_na_b, w_df_b, w_out_b, w1_b, w2_b, *w_in_next) = _na_attention(
            qkv3, na_bias, layer, casts)
        lam_init = 0.8 - 0.6 * math.exp(-0.3 * layer)
        y_df = _diff_attention(qkv3, t5_bias, diff_lambda[layer], diff_subln_g[layer], lam_init)
        xt = _mix_mlp(xt, y_na.reshape(T, NA_WIDTH), y_df.reshape(T, DIFF_WIDTH), gates,
                      w_na_b, w_df_b, w_out_b, norm2_g[layer], w1_b, w2_b, final_norm_g,
                      final=(layer == depth - 1))
        if w_in_next:
            w_in_b = w_in_next[0]
    return xt.reshape(B, S, D)
```

```python
import functools
import math

import numpy as np
import jax
import jax.numpy as jnp
from jax import lax
from jax.experimental import pallas as pl
from jax.experimental.pallas import tpu as pltpu

GRID_W = 64
NA_HEADS = 8
NA_HEAD_DIM = 64
NA_WIN_ROWS = 8
NA_WIN_COLS = 16
DIFF_HEADS = 4
DIFF_QK_DIM = 64
DIFF_V_DIM = 2 * DIFF_QK_DIM
T5_BUCKETS = 32
T5_MAX_DIST = 128
RMS_EPS = 1e-6
NEG_INF = -1e30

NA_WIDTH = NA_HEADS * NA_HEAD_DIM
DIFF_WIDTH = DIFF_HEADS * DIFF_V_DIM
QKV_WIDTH = 3 * NA_WIDTH + 3 * DIFF_WIDTH

LANES = 128
BF16_SUBLANES = 16
V7X_VMEM_BYTES = 64 * 2**20
VMEM_TEMP_BYTES = 12 * 2**20

TM_PROJ = 1024
TR_PROJ = 512
TN_PROJ = 1024
TM_MLP = 512
TF_MLP = 1024
NA_ROW_BLOCK = 4
NA_WIN_BLOCKS = 3
NA_BATCH_BLOCK = 4
TQ_DIFF = 256
FAR_PARTS = 3

BF16 = jnp.bfloat16
F32 = jnp.float32


def _nbytes(shape, dtype, buffers=1):
    return buffers * math.prod(shape) * jnp.dtype(dtype).itemsize


def _vmem_limit(*window_bytes):
    total = sum(window_bytes) + VMEM_TEMP_BYTES
    assert total < V7X_VMEM_BYTES
    return int(total)


def _aligned(index, multiple):
    return index if isinstance(index, int) else pl.multiple_of(index, multiple)


def _rmsnorm_f32(x, g):
    return (x * lax.rsqrt(jnp.mean(x * x, axis=-1, keepdims=True) + RMS_EPS)) * g


def _in_proj_kernel(x_ref, g_ref, w_ref, qkv_ref, gate_ref, *, rows, chunk):
    n_qkv = qkv_ref.shape[1]
    n_gate = gate_ref.shape[1]
    for r in range(x_ref.shape[0] // rows):
        rs = slice(r * rows, (r + 1) * rows)
        hb = _rmsnorm_f32(x_ref[rs, :], g_ref[...]).astype(BF16)
        for c in range(n_qkv // chunk):
            cs = slice(c * chunk, (c + 1) * chunk)
            qkv_ref[rs, cs] = jnp.dot(hb, w_ref[:, cs], preferred_element_type=F32).astype(BF16)
        for c in range(n_gate // chunk):
            cs = slice(c * chunk, (c + 1) * chunk)
            ws = slice(n_qkv + c * chunk, n_qkv + (c + 1) * chunk)
            gate_ref[rs, cs] = jnp.dot(hb, w_ref[:, ws], preferred_element_type=F32)


def _in_proj(xt, g, w):
    T, D = xt.shape
    n_all = w.shape[1]
    n_gate = n_all - QKV_WIDTH
    tm = TM_PROJ
    vmem = _vmem_limit(_nbytes((tm, D), F32, 2), _nbytes((D, n_all), BF16),
                       _nbytes((tm, QKV_WIDTH), BF16, 2), _nbytes((tm, n_gate), F32, 2))
    return pl.pallas_call(
        functools.partial(_in_proj_kernel, rows=TR_PROJ, chunk=TN_PROJ),
        out_shape=(jax.ShapeDtypeStruct((T, QKV_WIDTH), BF16),
                   jax.ShapeDtypeStruct((T, n_gate), F32)),
        grid=(T // tm,),
        in_specs=[
            pl.BlockSpec((tm, D), lambda i: (i, 0)),
            pl.BlockSpec((1, D), lambda i: (0, 0)),
            pl.BlockSpec((D, n_all), lambda i: (0, 0), pipeline_mode=pl.Buffered(1)),
        ],
        out_specs=(pl.BlockSpec((tm, QKV_WIDTH), lambda i: (i, 0)),
                   pl.BlockSpec((tm, n_gate), lambda i: (i, 0))),
        compiler_params=pltpu.CompilerParams(
            dimension_semantics=("arbitrary",), vmem_limit_bytes=vmem),
        name="in_proj",
    )(xt, g.reshape(1, D), w)


def _na_geometry(rows):
    rb_rows = NA_ROW_BLOCK
    n_rb = rows // rb_rows
    win_rows = NA_WIN_BLOCKS * rb_rows
    wr = min(NA_WIN_ROWS, rows)
    kb = np.clip(np.arange(n_rb) - 1, 0, n_rb - NA_WIN_BLOCKS)
    dr_idx = np.zeros((n_rb, rb_rows, win_rows), np.int32)
    valid = np.zeros((n_rb, rb_rows, win_rows), bool)
    for rb in range(n_rb):
        w0 = kb[rb] * rb_rows
        for ri in range(rb_rows):
            r = rb * rb_rows + ri
            r0 = min(max(r - wr // 2, 0), rows - wr)
            assert w0 <= r0 and r0 + wr <= w0 + win_rows
            for wj in range(win_rows):
                krow = w0 + wj
                valid[rb, ri, wj] = r0 <= krow < r0 + wr
                dr_idx[rb, ri, wj] = min(max(krow - r + NA_WIN_ROWS - 1, 0), 2 * NA_WIN_ROWS - 2)
    for rb in range(2, n_rb - 1):
        assert (valid[rb] == valid[1]).all() and (dr_idx[rb] == dr_idx[1]).all()
    classes = [0, 1, n_rb - 1]
    return n_rb, kb, dr_idx[classes], valid[classes]


def _na_bias_kernel(rpb_ref, w_ref, o_ref, wb_ref, *, dr_idx, row_valid):
    assert LANES == 2 * GRID_W
    n_cls, rb_rows, win_rows = dr_idx.shape
    c = lax.broadcasted_iota(jnp.int32, (GRID_W, LANES), 0)
    lane = lax.broadcasted_iota(jnp.int32, (GRID_W, LANES), 1)
    left = lane < GRID_W
    kc = jnp.where(left, lane, lane - GRID_W)
    c0 = jnp.clip(c - NA_WIN_COLS // 2, 0, GRID_W - NA_WIN_COLS)
    col_ok = (kc >= c0) & (kc < c0 + NA_WIN_COLS)
    neg = jnp.full((GRID_W, LANES), NEG_INF, F32)

    @functools.lru_cache(maxsize=None)
    def half_tile(d, right):
        w = jnp.broadcast_to(rpb_ref[d:d + 1, :], (GRID_W, LANES))
        t = pltpu.roll(w, GRID_W if right else 0, 1, stride=1, stride_axis=0)
        return jnp.where(col_ok, t, neg)

    for cls in range(n_cls):
        for ri in range(rb_rows):
            for p in range(win_rows // 2):
                halves = []
                for side in range(2):
                    wj = 2 * p + side
                    ok = bool(row_valid[cls, ri, wj])
                    halves.append(half_tile(int(dr_idx[cls, ri, wj]), side == 1) if ok else neg)
                o_ref[cls, ri * GRID_W:(ri + 1) * GRID_W, p * LANES:(p + 1) * LANES] = (
                    jnp.where(left, halves[0], halves[1]))
    wb_ref[...] = w_ref[...].astype(BF16)


def _na_bias_table(na_rpb, rows, w_in):
    depth, H, n_dr, n_dc = na_rpb.shape
    assert n_dr == 2 * NA_WIN_ROWS - 1 and n_dc == 2 * NA_WIN_COLS - 1
    _, _, dr_idx, row_valid = _na_geometry(rows)
    n_cls, rb_rows, win_rows = dr_idx.shape
    rq, wk = rb_rows * GRID_W, win_rows * GRID_W
    lanes = jnp.pad(na_rpb, ((0, 0), (0, 0), (0, 0), (0, LANES - n_dc)), constant_values=NEG_INF)
    lanes = jnp.roll(lanes, -(NA_WIN_COLS - 1), axis=-1)
    n_steps = depth * H
    _, R, C = w_in.shape
    slab = R // n_steps
    assert R % n_steps == 0 and slab % BF16_SUBLANES == 0
    table, w0 = pl.pallas_call(
        functools.partial(_na_bias_kernel, dr_idx=dr_idx, row_valid=row_valid),
        out_shape=(jax.ShapeDtypeStruct((depth, n_cls, H, rq, wk), F32),
                   jax.ShapeDtypeStruct((n_steps, slab, C), BF16)),
        grid=(depth, H),
        in_specs=[pl.BlockSpec((None, None, n_dr, LANES), lambda l, h: (l, h, 0, 0)),
                  pl.BlockSpec((None, slab, C), lambda l, h: (l * H + h, 0, 0))],
        out_specs=(pl.BlockSpec((None, n_cls, None, rq, wk), lambda l, h: (l, 0, h, 0, 0)),
                   pl.BlockSpec((None, slab, C), lambda l, h: (l * H + h, 0, 0))),
        compiler_params=pltpu.CompilerParams(dimension_semantics=("arbitrary", "arbitrary")),
        name="na_bias",
    )(lanes, w_in.reshape(-1, slab, C))
    return table, w0.reshape(R, C)


def _na_kernel(q_ref, k0_ref, k1_ref, k2_ref, v0_ref, v1_ref, v2_ref, bias_ref, *rest):
    n_cast = len(rest) // 2
    o_ref = rest[n_cast]
    for w_ref, wb_ref in zip(rest[:n_cast], rest[n_cast + 1:]):
        wb_ref[...] = w_ref[...].astype(BF16)
    nb, rq = q_ref.shape[0], q_ref.shape[1]
    scale = 1.0 / math.sqrt(NA_HEAD_DIM)
    lo = lax.broadcasted_iota(jnp.int32, (rq, LANES), 1) < NA_HEAD_DIM
    heads_per_vreg = LANES // NA_HEAD_DIM
    for bi, hp in np.ndindex(nb, NA_HEADS // heads_per_vreg):
        cs = (bi, slice(None), slice(hp * LANES, (hp + 1) * LANES))
        q = q_ref[cs] * scale
        k = jnp.concatenate([k0_ref[cs], k1_ref[cs], k2_ref[cs]], axis=0)
        v = jnp.concatenate([v0_ref[cs], v1_ref[cs], v2_ref[cs]], axis=0)
        v_ones = jnp.concatenate([v, jnp.ones_like(v)], axis=1)
        outs = []
        for e in range(heads_per_vreg):
            qm = jnp.where(lo if e == 0 else jnp.logical_not(lo), q, jnp.zeros_like(q))
            s = lax.dot_general(qm, k, (((1,), (1,)), ((), ())), preferred_element_type=F32)
            s = s + bias_ref[heads_per_vreg * hp + e]
            m = jnp.max(s, axis=-1, keepdims=True)
            p = jnp.exp((s - m).astype(BF16))
            ol = jnp.dot(p, v_ones, preferred_element_type=F32)
            outs.append(ol[:, 0:LANES] / ol[:, LANES:2 * LANES])
        o_ref[cs] = jnp.where(lo, outs[0], outs[1]).astype(BF16)


def _na_attention(qkv3, bias_table, layer, casts):
    B, S, _ = qkv3.shape
    rows = S // GRID_W
    n_rb, kb, _, _ = _na_geometry(rows)
    rq = NA_ROW_BLOCK * GRID_W
    assert NA_HEADS * NA_HEAD_DIM == NA_WIDTH and LANES % NA_HEAD_DIM == 0
    k_col, v_col = 1, 2
    n_kb = n_rb - NA_WIN_BLOCKS

    nb = NA_BATCH_BLOCK
    assert B % nb == 0

    def kv_spec(col, t):
        return pl.BlockSpec((nb, rq, NA_WIDTH),
                            lambda rb, b: (b, jnp.clip(rb - 1, 0, n_kb) + t, col))

    def bias_map(rb, b):
        cls = (rb > 0).astype(jnp.int32) + (rb == n_rb - 1).astype(jnp.int32)
        return (layer, cls, 0, 0, 0)

    n_bb = B // nb
    n_steps = n_rb * n_bb
    cast_in, cast_in_specs, cast_out, cast_out_specs = [], [], [], []
    for w, w_layer in casts:
        depth, R, C = w.shape
        slab = R // n_steps
        assert R % n_steps == 0 and slab % BF16_SUBLANES == 0
        cast_in.append(w.reshape(depth * n_steps, slab, C))
        first = w_layer * n_steps
        cast_in_specs.append(pl.BlockSpec(
            (None, slab, C), lambda rb, b, first=first: (first + rb * n_bb + b, 0, 0)))
        cast_out.append(jax.ShapeDtypeStruct((n_steps, slab, C), BF16))
        cast_out_specs.append(pl.BlockSpec((None, slab, C), lambda rb, b: (rb * n_bb + b, 0, 0)))

    wk = NA_WIN_BLOCKS * rq
    n_qkv_windows = 2 + 2 * NA_WIN_BLOCKS
    vmem = _vmem_limit(
        n_qkv_windows * _nbytes((nb, rq, NA_WIDTH), BF16, 2),
        _nbytes((NA_HEADS, rq, wk), F32, 2),
        *[_nbytes(o.shape[1:], F32, 2) + _nbytes(o.shape[1:], BF16, 2) for o in cast_out])
    y, *cast = pl.pallas_call(
        _na_kernel,
        out_shape=[jax.ShapeDtypeStruct((B, S, NA_WIDTH), BF16)] + cast_out,
        grid=(n_rb, n_bb),
        in_specs=[pl.BlockSpec((nb, rq, NA_WIDTH), lambda rb, b: (b, rb, 0))]
        + [kv_spec(k_col, t) for t in range(NA_WIN_BLOCKS)]
        + [kv_spec(v_col, t) for t in range(NA_WIN_BLOCKS)]
        + [pl.BlockSpec((None, None, NA_HEADS, rq, wk), bias_map)]
        + cast_in_specs,
        out_specs=[pl.BlockSpec((nb, rq, NA_WIDTH), lambda rb, b: (b, rb, 0))] + cast_out_specs,
        compiler_params=pltpu.CompilerParams(
            dimension_semantics=("arbitrary", "arbitrary"), vmem_limit_bytes=vmem),
        name="na_attn",
    )(qkv3, qkv3, qkv3, qkv3, qkv3, qkv3, qkv3, bias_table, *cast_in)
    return y, [c.reshape(w.shape[1], w.shape[2]) for c, (w, _) in zip(cast, casts)]


def _t5_bucket_steps(seq):
    half = T5_BUCKETS // 2
    max_exact = half // 2
    rel = np.arange(-(seq - 1), seq)
    n = np.abs(rel)
    nf = np.maximum(n, 1).astype(np.float64)
    large = max_exact + np.floor(
        np.log(nf / max_exact) / math.log(T5_MAX_DIST / max_exact) * (half - max_exact) + 1e-9
    ).astype(np.int64)
    large = np.minimum(large, half - 1)
    bucket = np.where(rel > 0, half, 0) + np.where(n < max_exact, n, large)
    steps = [(int(rel[i]), int(bucket[i])) for i in range(1, len(rel)) if bucket[i] != bucket[i - 1]]
    sat = max(abs(steps[0][0]) + 1, abs(steps[-1][0]))
    return int(bucket[0]), steps, sat


def _diff_kernel(t5_ref, q_ref, k_ref, v_ref, lam_ref, g_ref, o_ref, *scratch, tq, n_buf,
                 lam_init, bucket_steps):
    s_refs, m_refs, ol_refs = (scratch[i * n_buf:(i + 1) * n_buf] for i in range(3))
    corr_ref, kfar_ref = scratch[3 * n_buf:]
    b = pl.program_id(0)
    seq = k_ref.shape[0]
    n_t = seq // tq
    band = 3 * tq
    first_bucket, steps, _ = bucket_steps
    last_bucket = steps[-1][1]
    split = tq + tq // 2
    one_lane = FAR_PARTS * n_t
    assert one_lane + FAR_PARTS <= LANES and n_t >= 4

    @pl.when(b == 0)
    def _init():
        row = lax.broadcasted_iota(jnp.int32, (tq, band), 0)
        col = lax.broadcasted_iota(jnp.int32, (tq, band), 1)
        rel = col - tq - row
        for hh in range(DIFF_HEADS):
            val = jnp.full((tq, band), t5_ref[first_bucket, hh], F32)
            for thr, bkt in steps:
                val = jnp.where(rel >= thr, t5_ref[bkt, hh], val)
            far = jnp.where(col < split, t5_ref[first_bucket, hh], t5_ref[last_bucket, hh])
            corr_ref[hh] = val - far
        corr_ref[DIFF_HEADS] = jnp.zeros((tq, band), F32)
        key = lax.broadcasted_iota(jnp.int32, (seq, LANES), 0)
        lane = lax.broadcasted_iota(jnp.int32, (seq, LANES), 1)
        tile = lax.rem(lane, n_t)
        step = jnp.where(key >= tile * tq + tq // 2, 1.0, 0.0)
        ones = jnp.where(lane < one_lane + FAR_PARTS, 1.0, 0.0)
        kfar_ref[...] = jnp.where(lane < one_lane, step, ones).astype(BF16)

    scale = 1.0 / math.sqrt(DIFF_QK_DIM)
    lane = lax.broadcasted_iota(jnp.int32, (tq, LANES), 1)
    lo = lane < DIFF_QK_DIM
    lp = lam_ref[...]
    lam = (jnp.exp(jnp.sum(lp[0:1] * lp[1:2], axis=-1, keepdims=True))
           - jnp.exp(jnp.sum(lp[2:3] * lp[3:4], axis=-1, keepdims=True)) + lam_init)

    chunk_order = [-1, 0, 1] + list(range(2, n_t - 1))

    def tile_ids(t):
        if isinstance(t, int):
            h, qi = divmod(t, n_t)
        else:
            h, qi = lax.div(t, n_t), lax.rem(t, n_t)
        return h, qi, pl.ds(_aligned(h * LANES, LANES), LANES)

    def key_rows(qi, d):
        chunk = (qi + d) % n_t if isinstance(qi, int) else lax.rem(qi + (d + n_t), n_t)
        return pl.ds(_aligned(chunk * tq, tq), tq)

    def scores(t, s_ref, m_ref):
        h, qi, cols = tile_ids(t)
        c_neg = t5_ref[first_bucket, h]
        c_step = t5_ref[last_bucket, h] - c_neg
        row0 = _aligned(qi * tq, tq)
        q = q_ref[pl.ds(row0, tq), cols] * scale
        zero = jnp.zeros_like(q)
        far = jnp.where(lane < one_lane, jnp.where(lax.rem(lane, n_t) == qi, c_step, 0.0),
                        jnp.where(lane < one_lane + FAR_PARTS, c_neg, 0.0))
        part = jnp.where(lane < one_lane, lax.div(lane, n_t), lane - one_lane)
        q_far = jnp.zeros((tq, LANES), BF16)
        for i in range(FAR_PARTS):
            piece = far.astype(BF16)
            q_far = jnp.where(part == i, piece, q_far)
            far = far - piece.astype(F32)
        lhs = jnp.concatenate(
            [jnp.concatenate([jnp.where(lo, q, zero), q_far], axis=1),
             jnp.concatenate([jnp.where(lo, zero, q), q_far], axis=1)], axis=0)
        m_run = None
        for pos, d in enumerate(chunk_order):
            rows = key_rows(qi, d)
            rhs = jnp.concatenate([k_ref[rows, cols], kfar_ref[rows, :]], axis=1)
            s = lax.dot_general(lhs, rhs, (((1,), (1,)), ((), ())), preferred_element_type=F32)
            if d in (-1, 0, 1):
                inside = (qi + d >= 0) & (qi + d < n_t)
                head = jnp.where(inside, h, DIFF_HEADS)
                corr = corr_ref[head, :, (d + 1) * tq:(d + 2) * tq]
                s = s + jnp.concatenate([corr, corr], axis=0)
            s_ref[:, pos * tq:(pos + 1) * tq] = s
            m_chunk = functools.reduce(
                jnp.maximum, [s[:, i * LANES:(i + 1) * LANES] for i in range(tq // LANES)])
            m_run = m_chunk if m_run is None else jnp.maximum(m_run, m_chunk)
        m_ref[...] = m_run

    def attend(t, s_ref, m_ref, ol_ref):
        _, qi, cols = tile_ids(t)
        m = jnp.broadcast_to(jnp.max(m_ref[...], axis=-1, keepdims=True), (2 * tq, LANES))
        m = jnp.concatenate([m] * (tq // LANES), axis=1)
        ol = None
        for pos, d in enumerate(chunk_order):
            e = jnp.exp((s_ref[:, pos * tq:(pos + 1) * tq] - m).astype(BF16))
            v = v_ref[key_rows(qi, d), cols]
            part = jnp.dot(e, jnp.concatenate([v, jnp.ones_like(v)], axis=1),
                           preferred_element_type=F32)
            ol = part if ol is None else ol + part
        ol_ref[...] = ol

    def finish(t, ol_ref):
        _, qi, cols = tile_ids(t)
        o = (ol_ref[0:tq, 0:LANES] / ol_ref[0:tq, LANES:2 * LANES]
             - lam * (ol_ref[tq:2 * tq, 0:LANES] / ol_ref[tq:2 * tq, LANES:2 * LANES]))
        row0 = _aligned(qi * tq, tq)
        o_ref[pl.ds(row0, tq), cols] = (
            _rmsnorm_f32(o, g_ref[...]) * (1.0 - lam_init)).astype(BF16)

    n_tiles = DIFF_HEADS * n_t

    def step(u, slot):
        live = (lambda t: True) if not isinstance(u, int) else (lambda t: 0 <= t < n_tiles)
        if live(u):
            scores(u, s_refs[slot], m_refs[slot])
        if live(u - 3):
            finish(u - 3, ol_refs[(slot - 3) % n_buf])
        if live(u - 2):
            a = (slot - 2) % n_buf
            attend(u - 2, s_refs[a], m_refs[a], ol_refs[a])

    first, trips = 3, (n_tiles - 3) // n_buf
    assert trips >= 1
    for u in range(first):
        step(u, u % n_buf)

    def rotation(j, carry):
        u = first + n_buf * j
        for i in range(n_buf):
            step(u + i, (first + i) % n_buf)
        return carry

    lax.fori_loop(0, trips, rotation, 0)
    for u in range(first + n_buf * trips, n_tiles + 3):
        step(u, u % n_buf)


def _diff_attention(qkv3, t5_bias, lam_params, subln_g, lam_init):
    B, S, _ = qkv3.shape
    tq = TQ_DIFF
    bucket_steps = _t5_bucket_steps(S)
    assert bucket_steps[2] <= tq and S % tq == 0 and tq % LANES == 0
    assert 2 * DIFF_QK_DIM == LANES and DIFF_V_DIM == LANES
    assert NA_WIDTH == DIFF_WIDTH
    q_col = 3
    n_buf = 3
    scratch = (
        [((2 * tq, S), F32)] * n_buf
        + [((2 * tq, LANES), F32)] * n_buf
        + [((2 * tq, 2 * LANES), F32)] * n_buf
        + [((DIFF_HEADS + 1, tq, 3 * tq), F32),
           ((S, LANES), BF16)])
    vmem = _vmem_limit(4 * _nbytes((S, DIFF_WIDTH), BF16, 2), *[_nbytes(*s) for s in scratch])
    return pl.pallas_call(
        functools.partial(_diff_kernel, tq=tq, n_buf=n_buf, lam_init=lam_init,
                          bucket_steps=bucket_steps),
        out_shape=jax.ShapeDtypeStruct((B, S, DIFF_WIDTH), BF16),
        grid=(B,),
        in_specs=[
            pl.BlockSpec(memory_space=pltpu.SMEM),
            pl.BlockSpec((None, S, DIFF_WIDTH), lambda b: (b, 0, q_col)),
            pl.BlockSpec((None, S, DIFF_WIDTH), lambda b: (b, 0, q_col + 1)),
            pl.BlockSpec((None, S, DIFF_WIDTH), lambda b: (b, 0, q_col + 2)),
            pl.BlockSpec((4, DIFF_QK_DIM), lambda b: (0, 0)),
            pl.BlockSpec((1, DIFF_V_DIM), lambda b: (0, 0)),
        ],
        out_specs=pl.BlockSpec((None, S, DIFF_WIDTH), lambda b: (b, 0, 0)),
        scratch_shapes=[pltpu.VMEM(*s) for s in scratch],
        compiler_params=pltpu.CompilerParams(
            dimension_semantics=("arbitrary",), vmem_limit_bytes=vmem),
        name="diff_attn",
    )(t5_bias, qkv3, qkv3, qkv3, lam_params, subln_g.reshape(1, DIFF_V_DIM))


def _mix_mlp_kernel(x_ref, yna_ref, ydf_ref, gate_ref, wna_ref, wdf_ref, wout_ref, g_ref, w1_ref,
                    w2_ref, fg_ref, o_ref, *, tf, final):
    d = x_ref.shape[1]
    b_na = jnp.dot(yna_ref[...], wna_ref[...], preferred_element_type=F32)
    b_df = jnp.dot(ydf_ref[...], wdf_ref[...], preferred_element_type=F32)
    merged = (jax.nn.sigmoid(gate_ref[:, 0:d]) * b_na
              + jax.nn.sigmoid(gate_ref[:, d:2 * d]) * b_df)
    x = x_ref[...] + jnp.dot(merged.astype(BF16), wout_ref[...], preferred_element_type=F32)
    h = _rmsnorm_f32(x, g_ref[...]).astype(BF16)
    for c in range(w1_ref.shape[1] // tf):
        u = jnp.dot(h, w1_ref[:, c * tf:(c + 1) * tf], preferred_element_type=F32)
        u = jnp.square(jnp.maximum(u, 0.0)).astype(BF16)
        x = x + jnp.dot(u, w2_ref[c * tf:(c + 1) * tf, :], preferred_element_type=F32)
    o_ref[...] = _rmsnorm_f32(x, fg_ref[...]) if final else x


def _mix_mlp(xt, y_na, y_df, gates, w_na, w_df, w_out, g, w1, w2, final_g, final):
    T, D = xt.shape
    F = w1.shape[1]
    tm, tf = TM_MLP, TF_MLP
    once = pl.Buffered(1)
    assert F % tf == 0
    weights = (w_na, w_df, w_out, w1, w2)
    vmem = _vmem_limit(
        2 * _nbytes((tm, D), F32, 2),
        _nbytes((tm, NA_WIDTH), BF16, 2), _nbytes((tm, DIFF_WIDTH), BF16, 2),
        _nbytes((tm, 2 * D), F32, 2), *[_nbytes(w.shape, BF16) for w in weights])
    return pl.pallas_call(
        functools.partial(_mix_mlp_kernel, tf=tf, final=final),
        out_shape=jax.ShapeDtypeStruct((T, D), F32),
        grid=(T // tm,),
        in_specs=[
            pl.BlockSpec((tm, D), lambda i: (i, 0)),
            pl.BlockSpec((tm, NA_WIDTH), lambda i: (i, 0)),
            pl.BlockSpec((tm, DIFF_WIDTH), lambda i: (i, 0)),
            pl.BlockSpec((tm, 2 * D), lambda i: (i, 0)),
            pl.BlockSpec((NA_WIDTH, D), lambda i: (0, 0), pipeline_mode=once),
            pl.BlockSpec((DIFF_WIDTH, D), lambda i: (0, 0), pipeline_mode=once),
            pl.BlockSpec((D, D), lambda i: (0, 0), pipeline_mode=once),
            pl.BlockSpec((1, D), lambda i: (0, 0)),
            pl.BlockSpec((D, F), lambda i: (0, 0), pipeline_mode=once),
            pl.BlockSpec((F, D), lambda i: (0, 0), pipeline_mode=once),
            pl.BlockSpec((1, D), lambda i: (0, 0)),
        ],
        out_specs=pl.BlockSpec((tm, D), lambda i: (i, 0)),
        compiler_params=pltpu.CompilerParams(
            dimension_semantics=("arbitrary",), vmem_limit_bytes=vmem),
        name="mix_mlp",
    )(xt, y_na, y_df, gates, w_na, w_df, w_out, g.reshape(1, D), w1, w2, final_g.reshape(1, D))


def kernel(x, t5_bias, final_norm_g, norm1_g, w_in, na_rpb, diff_lambda, diff_subln_g, w_na_o,
           w_diff_o, w_out, norm2_g, w_ff1, w_ff2):
    B, S, D = x.shape
    depth = w_in.shape[0]
    T = B * S
    assert w_in.shape[2] == QKV_WIDTH + 2 * D and S % GRID_W == 0
    xt = x.reshape(T, D)
    na_bias, w_in_b = _na_bias_table(na_rpb, S // GRID_W, w_in)
    for layer in range(depth):
        qkv, gates = _in_proj(xt, norm1_g[layer], w_in_b)
        qkv3 = qkv.reshape(B, S, QKV_WIDTH)
        casts = [(w, layer) for w in (w_na_o, w_diff_o, w_out, w_ff1, w_ff2)]
        if layer + 1 < depth:
            casts.append((w_in, layer + 1))
        y_na, (w_na_b, w_df_b, w_out_b, w1_b, w2_b, *w_in_next) = _na_attention(
            qkv3, na_bias, layer, casts)
        lam_init = 0.8 - 0.6 * math.exp(-0.3 * layer)
        y_df = _diff_attention(qkv3, t5_bias, diff_lambda[layer], diff_subln_g[layer], lam_init)
        xt = _mix_mlp(xt, y_na.reshape(T, NA_WIDTH), y_df.reshape(T, DIFF_WIDTH), gates,
                      w_na_b, w_df_b, w_out_b, norm2_g[layer], w1_b, w2_b, final_norm_g,
                      final=(layer == depth - 1))
        if w_in_next:
            w_in_b = w_in_next[0]
    return xt.reshape(B, S, D)
```

```python
import functools
import math

import numpy as np
import jax
import jax.numpy as jnp
from jax import lax
from jax.experimental import pallas as pl
from jax.experimental.pallas import tpu as pltpu

GRID_W = 64
NA_HEADS = 8
NA_HEAD_DIM = 64
NA_WIN_ROWS = 8
NA_WIN_COLS = 16
DIFF_HEADS = 4
DIFF_QK_DIM = 64
DIFF_V_DIM = 2 * DIFF_QK_DIM
T5_BUCKETS = 32
T5_MAX_DIST = 128
RMS_EPS = 1e-6
NEG_INF = -1e30

NA_WIDTH = NA_HEADS * NA_HEAD_DIM
DIFF_WIDTH = DIFF_HEADS * DIFF_V_DIM
QKV_WIDTH = 3 * NA_WIDTH + 3 * DIFF_WIDTH

LANES = 128
BF16_SUBLANES = 16
V7X_VMEM_BYTES = 64 * 2**20
VMEM_TEMP_BYTES = 12 * 2**20

TM_PROJ = 1024
TR_PROJ = 512
TN_PROJ = 1024
TM_MLP = 512
TF_MLP = 1024
NA_ROW_BLOCK = 4
NA_WIN_BLOCKS = 3
NA_BATCH_BLOCK = 4
TQ_DIFF = 256
FAR_PARTS = 3

BF16 = jnp.bfloat16
F32 = jnp.float32


def _nbytes(shape, dtype, buffers=1):
    return buffers * math.prod(shape) * jnp.dtype(dtype).itemsize


def _vmem_limit(*window_bytes):
    total = sum(window_bytes) + VMEM_TEMP_BYTES
    assert total < V7X_VMEM_BYTES
    return int(total)


def _aligned(index, multiple):
    return index if isinstance(index, int) else pl.multiple_of(index, multiple)


def _rmsnorm_f32(x, g):
    return (x * lax.rsqrt(jnp.mean(x * x, axis=-1, keepdims=True) + RMS_EPS)) * g


def _in_proj_kernel(x_ref, g_ref, w_ref, qkv_ref, gate_ref, *, rows, chunk):
    n_qkv = qkv_ref.shape[1]
    n_gate = gate_ref.shape[1]
    for r in range(x_ref.shape[0] // rows):
        rs = slice(r * rows, (r + 1) * rows)
        hb = _rmsnorm_f32(x_ref[rs, :], g_ref[...]).astype(BF16)
        for c in range(n_qkv // chunk):
            cs = slice(c * chunk, (c + 1) * chunk)
            qkv_ref[rs, cs] = jnp.dot(hb, w_ref[:, cs], preferred_element_type=F32).astype(BF16)
        for c in range(n_gate // chunk):
            cs = slice(c * chunk, (c + 1) * chunk)
            ws = slice(n_qkv + c * chunk, n_qkv + (c + 1) * chunk)
            gate_ref[rs, cs] = jnp.dot(hb, w_ref[:, ws], preferred_element_type=F32)


def _in_proj(xt, g, w):
    T, D = xt.shape
    n_all = w.shape[1]
    n_gate = n_all - QKV_WIDTH
    tm = TM_PROJ
    vmem = _vmem_limit(_nbytes((tm, D), F32, 2), _nbytes((D, n_all), BF16),
                       _nbytes((tm, QKV_WIDTH), BF16, 2), _nbytes((tm, n_gate), F32, 2))
    return pl.pallas_call(
        functools.partial(_in_proj_kernel, rows=TR_PROJ, chunk=TN_PROJ),
        out_shape=(jax.ShapeDtypeStruct((T, QKV_WIDTH), BF16),
                   jax.ShapeDtypeStruct((T, n_gate), F32)),
        grid=(T // tm,),
        in_specs=[
            pl.BlockSpec((tm, D), lambda i: (i, 0)),
            pl.BlockSpec((1, D), lambda i: (0, 0)),
            pl.BlockSpec((D, n_all), lambda i: (0, 0), pipeline_mode=pl.Buffered(1)),
        ],
        out_specs=(pl.BlockSpec((tm, QKV_WIDTH), lambda i: (i, 0)),
                   pl.BlockSpec((tm, n_gate), lambda i: (i, 0))),
        compiler_params=pltpu.CompilerParams(
            dimension_semantics=("arbitrary",), vmem_limit_bytes=vmem),
        name="in_proj",
    )(xt, g.reshape(1, D), w)


def _na_geometry(rows):
    rb_rows = NA_ROW_BLOCK
    n_rb = rows // rb_rows
    win_rows = NA_WIN_BLOCKS * rb_rows
    wr = min(NA_WIN_ROWS, rows)
    kb = np.clip(np.arange(n_rb) - 1, 0, n_rb - NA_WIN_BLOCKS)
    dr_idx = np.zeros((n_rb, rb_rows, win_rows), np.int32)
    valid = np.zeros((n_rb, rb_rows, win_rows), bool)
    for rb in range(n_rb):
        w0 = kb[rb] * rb_rows
        for ri in range(rb_rows):
            r = rb * rb_rows + ri
            r0 = min(max(r - wr // 2, 0), rows - wr)
            assert w0 <= r0 and r0 + wr <= w0 + win_rows
            for wj in range(win_rows):
                krow = w0 + wj
                valid[rb, ri, wj] = r0 <= krow < r0 + wr
                dr_idx[rb, ri, wj] = min(max(krow - r + NA_WIN_ROWS - 1, 0), 2 * NA_WIN_ROWS - 2)
    for rb in range(2, n_rb - 1):
        assert (valid[rb] == valid[1]).all() and (dr_idx[rb] == dr_idx[1]).all()
    classes = [0, 1, n_rb - 1]
    return n_rb, kb, dr_idx[classes], valid[classes]


def _na_bias_kernel(rpb_ref, w_ref, o_ref, wb_ref, *, dr_idx, row_valid):
    assert LANES == 2 * GRID_W
    n_cls, rb_rows, win_rows = dr_idx.shape
    c = lax.broadcasted_iota(jnp.int32, (GRID_W, LANES), 0)
    lane = lax.broadcasted_iota(jnp.int32, (GRID_W, LANES), 1)
    left = lane < GRID_W
    kc = jnp.where(left, lane, lane - GRID_W)
    c0 = jnp.clip(c - NA_WIN_COLS // 2, 0, GRID_W - NA_WIN_COLS)
    col_ok = (kc >= c0) & (kc < c0 + NA_WIN_COLS)
    neg = jnp.full((GRID_W, LANES), NEG_INF, F32)

    @functools.lru_cache(maxsize=None)
    def half_tile(d, right):
        w = jnp.broadcast_to(rpb_ref[d:d + 1, :], (GRID_W, LANES))
        t = pltpu.roll(w, GRID_W if right else 0, 1, stride=1, stride_axis=0)
        return jnp.where(col_ok, t, neg)

    for cls in range(n_cls):
        for ri in range(rb_rows):
            for p in range(win_rows // 2):
                halves = []
                for side in range(2):
                    wj = 2 * p + side
                    ok = bool(row_valid[cls, ri, wj])
                    halves.append(half_tile(int(dr_idx[cls, ri, wj]), side == 1) if ok else neg)
                o_ref[cls, ri * GRID_W:(ri + 1) * GRID_W, p * LANES:(p + 1) * LANES] = (
                    jnp.where(left, halves[0], halves[1]))
    wb_ref[...] = w_ref[...].astype(BF16)


def _na_bias_table(na_rpb, rows, w_in):
    depth, H, n_dr, n_dc = na_rpb.shape
    assert n_dr == 2 * NA_WIN_ROWS - 1 and n_dc == 2 * NA_WIN_COLS - 1
    _, _, dr_idx, row_valid = _na_geometry(rows)
    n_cls, rb_rows, win_rows = dr_idx.shape
    rq, wk = rb_rows * GRID_W, win_rows * GRID_W
    lanes = jnp.pad(na_rpb, ((0, 0), (0, 0), (0, 0), (0, LANES - n_dc)), constant_values=NEG_INF)
    lanes = jnp.roll(lanes, -(NA_WIN_COLS - 1), axis=-1)
    n_steps = depth * H
    _, R, C = w_in.shape
    slab = R // n_steps
    assert R % n_steps == 0 and slab % BF16_SUBLANES == 0
    table, w0 = pl.pallas_call(
        functools.partial(_na_bias_kernel, dr_idx=dr_idx, row_valid=row_valid),
        out_shape=(jax.ShapeDtypeStruct((depth, n_cls, H, rq, wk), F32),
                   jax.ShapeDtypeStruct((n_steps, slab, C), BF16)),
        grid=(depth, H),
        in_specs=[pl.BlockSpec((None, None, n_dr, LANES), lambda l, h: (l, h, 0, 0)),
                  pl.BlockSpec((None, slab, C), lambda l, h: (l * H + h, 0, 0))],
        out_specs=(pl.BlockSpec((None, n_cls, None, rq, wk), lambda l, h: (l, 0, h, 0, 0)),
                   pl.BlockSpec((None, slab, C), lambda l, h: (l * H + h, 0, 0))),
        compiler_params=pltpu.CompilerParams(dimension_semantics=("arbitrary", "arbitrary")),
        name="na_bias",
    )(lanes, w_in.reshape(-1, slab, C))
    return table, w0.reshape(R, C)


def _na_kernel(q_ref, k0_ref, k1_ref, k2_ref, v0_ref, v1_ref, v2_ref, bias_ref, *rest):
    n_cast = len(rest) // 2
    o_ref = rest[n_cast]
    for w_ref, wb_ref in zip(rest[:n_cast], rest[n_cast + 1:]):
        wb_ref[...] = w_ref[...].astype(BF16)
    nb, rq = q_ref.shape[0], q_ref.shape[1]
    scale = 1.0 / math.sqrt(NA_HEAD_DIM)
    lo = lax.broadcasted_iota(jnp.int32, (rq, LANES), 1) < NA_HEAD_DIM
    heads_per_vreg = LANES // NA_HEAD_DIM
    for bi, hp in np.ndindex(nb, NA_HEADS // heads_per_vreg):
        cs = (bi, slice(None), slice(hp * LANES, (hp + 1) * LANES))
        q = q_ref[cs] * scale
        k = jnp.concatenate([k0_ref[cs], k1_ref[cs], k2_ref[cs]], axis=0)
        v = jnp.concatenate([v0_ref[cs], v1_ref[cs], v2_ref[cs]], axis=0)
        v_ones = jnp.concatenate([v, jnp.ones_like(v)], axis=1)
        outs = []
        for e in range(heads_per_vreg):
            qm = jnp.where(lo if e == 0 else jnp.logical_not(lo), q, jnp.zeros_like(q))
            s = lax.dot_general(qm, k, (((1,), (1,)), ((), ())), preferred_element_type=F32)
            s = s + bias_ref[heads_per_vreg * hp + e]
            m = jnp.max(s, axis=-1, keepdims=True)
            p = jnp.exp((s - m).astype(BF16))
            ol = jnp.dot(p, v_ones, preferred_element_type=F32)
            outs.append(ol[:, 0:LANES] / ol[:, LANES:2 * LANES])
        o_ref[cs] = jnp.where(lo, outs[0], outs[1]).astype(BF16)


def _na_attention(qkv3, bias_table, layer, casts):
    B, S, _ = qkv3.shape
    rows = S // GRID_W
    n_rb, kb, _, _ = _na_geometry(rows)
    rq = NA_ROW_BLOCK * GRID_W
    assert NA_HEADS * NA_HEAD_DIM == NA_WIDTH and LANES % NA_HEAD_DIM == 0
    k_col, v_col = 1, 2
    n_kb = n_rb - NA_WIN_BLOCKS

    nb = NA_BATCH_BLOCK
    assert B % nb == 0

    def kv_spec(col, t):
        return pl.BlockSpec((nb, rq, NA_WIDTH),
                            lambda rb, b: (b, jnp.clip(rb - 1, 0, n_kb) + t, col))

    def bias_map(rb, b):
        cls = (rb > 0).astype(jnp.int32) + (rb == n_rb - 1).astype(jnp.int32)
        return (layer, cls, 0, 0, 0)

    n_bb = B // nb
    n_steps = n_rb * n_bb
    cast_in, cast_in_specs, cast_out, cast_out_specs = [], [], [], []
    for w, w_layer in casts:
        depth, R, C = w.shape
        slab = R // n_steps
        assert R % n_steps == 0 and slab % BF16_SUBLANES == 0
        cast_in.append(w.reshape(depth * n_steps, slab, C))
        first = w_layer * n_steps
        cast_in_specs.append(pl.BlockSpec(
            (None, slab, C), lambda rb, b, first=first: (first + rb * n_bb + b, 0, 0)))
        cast_out.append(jax.ShapeDtypeStruct((n_steps, slab, C), BF16))
        cast_out_specs.append(pl.BlockSpec((None, slab, C), lambda rb, b: (rb * n_bb + b, 0, 0)))

    wk = NA_WIN_BLOCKS * rq
    n_qkv_windows = 2 + 2 * NA_WIN_BLOCKS
    vmem = _vmem_limit(
        n_qkv_windows * _nbytes((nb, rq, NA_WIDTH), BF16, 2),
        _nbytes((NA_HEADS, rq, wk), F32, 2),
        *[_nbytes(o.shape[1:], F32, 2) + _nbytes(o.shape[1:], BF16, 2) for o in cast_out])
    y, *cast = pl.pallas_call(
        _na_kernel,
        out_shape=[jax.ShapeDtypeStruct((B, S, NA_WIDTH), BF16)] + cast_out,
        grid=(n_rb, n_bb),
        in_specs=[pl.BlockSpec((nb, rq, NA_WIDTH), lambda rb, b: (b, rb, 0))]
        + [kv_spec(k_col, t) for t in range(NA_WIN_BLOCKS)]
        + [kv_spec(v_col, t) for t in range(NA_WIN_BLOCKS)]
        + [pl.BlockSpec((None, None, NA_HEADS, rq, wk), bias_map)]
        + cast_in_specs,
        out_specs=[pl.BlockSpec((nb, rq, NA_WIDTH), lambda rb, b: (b, rb, 0))] + cast_out_specs,
        compiler_params=pltpu.CompilerParams(
            dimension_semantics=("arbitrary", "arbitrary"), vmem_limit_bytes=vmem),
        name="na_attn",
    )(qkv3, qkv3, qkv3, qkv3, qkv3, qkv3, qkv3, bias_table, *cast_in)
    return y, [c.reshape(w.shape[1], w.shape[2]) for c, (w, _) in zip(cast, casts)]


def _t5_bucket_steps(seq):
    half = T5_BUCKETS // 2
    max_exact = half // 2
    rel = np.arange(-(seq - 1), seq)
    n = np.abs(rel)
    nf = np.maximum(n, 1).astype(np.float64)
    large = max_exact + np.floor(
        np.log(nf / max_exact) / math.log(T5_MAX_DIST / max_exact) * (half - max_exact) + 1e-9
    ).astype(np.int64)
    large = np.minimum(large, half - 1)
    bucket = np.where(rel > 0, half, 0) + np.where(n < max_exact, n, large)
    steps = [(int(rel[i]), int(bucket[i])) for i in range(1, len(rel)) if bucket[i] != bucket[i - 1]]
    sat = max(abs(steps[0][0]) + 1, abs(steps[-1][0]))
    return int(bucket[0]), steps, sat


def _diff_kernel(t5_ref, qkv_ref, lam_ref, g_ref, o_ref, *scratch, tq, n_buf, lam_init,
                 bucket_steps):
    s_refs, m_refs, ol_refs = (scratch[i * n_buf:(i + 1) * n_buf] for i in range(3))
    corr_ref, kfar_ref = scratch[3 * n_buf:]
    b = pl.program_id(0)
    seq = qkv_ref.shape[0]
    n_t = seq // tq
    band = 3 * tq
    first_bucket, steps, _ = bucket_steps
    last_bucket = steps[-1][1]
    split = tq + tq // 2
    one_lane = FAR_PARTS * n_t
    assert one_lane + FAR_PARTS <= LANES and n_t >= 4

    @pl.when(b == 0)
    def _init():
        row = lax.broadcasted_iota(jnp.int32, (tq, band), 0)
        col = lax.broadcasted_iota(jnp.int32, (tq, band), 1)
        rel = col - tq - row
        for hh in range(DIFF_HEADS):
            val = jnp.full((tq, band), t5_ref[first_bucket, hh], F32)
            for thr, bkt in steps:
                val = jnp.where(rel >= thr, t5_ref[bkt, hh], val)
            far = jnp.where(col < split, t5_ref[first_bucket, hh], t5_ref[last_bucket, hh])
            corr_ref[hh] = val - far
        corr_ref[DIFF_HEADS] = jnp.zeros((tq, band), F32)
        key = lax.broadcasted_iota(jnp.int32, (seq, LANES), 0)
        lane = lax.broadcasted_iota(jnp.int32, (seq, LANES), 1)
        tile = lax.rem(lane, n_t)
        step = jnp.where(key >= tile * tq + tq // 2, 1.0, 0.0)
        ones = jnp.where(lane < one_lane + FAR_PARTS, 1.0, 0.0)
        kfar_ref[...] = jnp.where(lane < one_lane, step, ones).astype(BF16)

    scale = 1.0 / math.sqrt(DIFF_QK_DIM)
    lane = lax.broadcasted_iota(jnp.int32, (tq, LANES), 1)
    lo = lane < DIFF_QK_DIM
    lp = lam_ref[...]
    lam = (jnp.exp(jnp.sum(lp[0:1] * lp[1:2], axis=-1, keepdims=True))
           - jnp.exp(jnp.sum(lp[2:3] * lp[3:4], axis=-1, keepdims=True)) + lam_init)

    chunk_order = [-1, 0, 1] + list(range(2, n_t - 1))

    def tile_ids(t):
        if isinstance(t, int):
            h, qi = divmod(t, n_t)
        else:
            h, qi = lax.div(t, n_t), lax.rem(t, n_t)
        return h, qi, pl.ds(_aligned(h * LANES, LANES), LANES)

    def qkv_cols(h, part):
        return pl.ds(_aligned(part * DIFF_WIDTH + h * LANES, LANES), LANES)

    def key_rows(qi, d):
        chunk = (qi + d) % n_t if isinstance(qi, int) else lax.rem(qi + (d + n_t), n_t)
        return pl.ds(_aligned(chunk * tq, tq), tq)

    def scores(t, s_ref, m_ref):
        h, qi, _ = tile_ids(t)
        c_neg = t5_ref[first_bucket, h]
        c_step = t5_ref[last_bucket, h] - c_neg
        row0 = _aligned(qi * tq, tq)
        q = qkv_ref[pl.ds(row0, tq), qkv_cols(h, 0)] * scale
        zero = jnp.zeros_like(q)
        far = jnp.where(lane < one_lane, jnp.where(lax.rem(lane, n_t) == qi, c_step, 0.0),
                        jnp.where(lane < one_lane + FAR_PARTS, c_neg, 0.0))
        part = jnp.where(lane < one_lane, lax.div(lane, n_t), lane - one_lane)
        q_far = jnp.zeros((tq, LANES), BF16)
        for i in range(FAR_PARTS):
            piece = far.astype(BF16)
            q_far = jnp.where(part == i, piece, q_far)
            far = far - piece.astype(F32)
        lhs = jnp.concatenate(
            [jnp.concatenate([jnp.where(lo, q, zero), q_far], axis=1),
             jnp.concatenate([jnp.where(lo, zero, q), q_far], axis=1)], axis=0)
        m_run = None
        for pos, d in enumerate(chunk_order):
            rows = key_rows(qi, d)
            rhs = jnp.concatenate([qkv_ref[rows, qkv_cols(h, 1)], kfar_ref[rows, :]], axis=1)
            s = lax.dot_general(lhs, rhs, (((1,), (1,)), ((), ())), preferred_element_type=F32)
            if d in (-1, 0, 1):
                inside = (qi + d >= 0) & (qi + d < n_t)
                head = jnp.where(inside, h, DIFF_HEADS)
                corr = corr_ref[head, :, (d + 1) * tq:(d + 2) * tq]
                s = s + jnp.concatenate([corr, corr], axis=0)
            s_ref[:, pos * tq:(pos + 1) * tq] = s
            m_chunk = functools.reduce(
                jnp.maximum, [s[:, i * LANES:(i + 1) * LANES] for i in range(tq // LANES)])
            m_run = m_chunk if m_run is None else jnp.maximum(m_run, m_chunk)
        m_ref[...] = m_run

    def attend(t, s_ref, m_ref, ol_ref):
        h, qi, _ = tile_ids(t)
        m = jnp.broadcast_to(jnp.max(m_ref[...], axis=-1, keepdims=True), (2 * tq, LANES))
        m = jnp.concatenate([m] * (tq // LANES), axis=1)
        ol = None
        for pos, d in enumerate(chunk_order):
            e = jnp.exp((s_ref[:, pos * tq:(pos + 1) * tq] - m).astype(BF16))
            v = qkv_ref[key_rows(qi, d), qkv_cols(h, 2)]
            part = jnp.dot(e, jnp.concatenate([v, jnp.ones_like(v)], axis=1),
                           preferred_element_type=F32)
            ol = part if ol is None else ol + part
        ol_ref[...] = ol

    def finish(t, ol_ref):
        _, qi, cols = tile_ids(t)
        o = (ol_ref[0:tq, 0:LANES] / ol_ref[0:tq, LANES:2 * LANES]
             - lam * (ol_ref[tq:2 * tq, 0:LANES] / ol_ref[tq:2 * tq, LANES:2 * LANES]))
        row0 = _aligned(qi * tq, tq)
        o_ref[pl.ds(row0, tq), cols] = (
            _rmsnorm_f32(o, g_ref[...]) * (1.0 - lam_init)).astype(BF16)

    n_tiles = DIFF_HEADS * n_t

    def step(u, slot):
        live = (lambda t: True) if not isinstance(u, int) else (lambda t: 0 <= t < n_tiles)
        if live(u):
            scores(u, s_refs[slot], m_refs[slot])
        if live(u - 3):
            finish(u - 3, ol_refs[(slot - 3) % n_buf])
        if live(u - 2):
            a = (slot - 2) % n_buf
            attend(u - 2, s_refs[a], m_refs[a], ol_refs[a])

    first, trips = 3, (n_tiles - 3) // n_buf
    assert trips >= 1
    for u in range(first):
        step(u, u % n_buf)

    def rotation(j, carry):
        u = first + n_buf * j
        for i in range(n_buf):
            step(u + i, (first + i) % n_buf)
        return carry

    lax.fori_loop(0, trips, rotation, 0)
    for u in range(first + n_buf * trips, n_tiles + 3):
        step(u, u % n_buf)


def _diff_attention(qkv3, t5_bias, lam_params, subln_g, lam_init):
    B, S, _ = qkv3.shape
    tq = TQ_DIFF
    bucket_steps = _t5_bucket_steps(S)
    assert bucket_steps[2] <= tq and S % tq == 0 and tq % LANES == 0
    assert 2 * DIFF_QK_DIM == LANES and DIFF_V_DIM == LANES
    assert 3 * NA_WIDTH == 3 * DIFF_WIDTH
    n_buf = 3
    scratch = (
        [((2 * tq, S), F32)] * n_buf
        + [((2 * tq, LANES), F32)] * n_buf
        + [((2 * tq, 2 * LANES), F32)] * n_buf
        + [((DIFF_HEADS + 1, tq, 3 * tq), F32),
           ((S, LANES), BF16)])
    vmem = _vmem_limit(4 * _nbytes((S, DIFF_WIDTH), BF16, 2), *[_nbytes(*s) for s in scratch])
    return pl.pallas_call(
        functools.partial(_diff_kernel, tq=tq, n_buf=n_buf, lam_init=lam_init,
                          bucket_steps=bucket_steps),
        out_shape=jax.ShapeDtypeStruct((B, S, DIFF_WIDTH), BF16),
        grid=(B,),
        in_specs=[
            pl.BlockSpec(memory_space=pltpu.SMEM),
            pl.BlockSpec((None, S, 3 * DIFF_WIDTH), lambda b: (b, 0, 1)),
            pl.BlockSpec((4, DIFF_QK_DIM), lambda b: (0, 0)),
            pl.BlockSpec((1, DIFF_V_DIM), lambda b: (0, 0)),
        ],
        out_specs=pl.BlockSpec((None, S, DIFF_WIDTH), lambda b: (b, 0, 0)),
        scratch_shapes=[pltpu.VMEM(*s) for s in scratch],
        compiler_params=pltpu.CompilerParams(
            dimension_semantics=("arbitrary",), vmem_limit_bytes=vmem),
        name="diff_attn",
    )(t5_bias, qkv3, lam_params, subln_g.reshape(1, DIFF_V_DIM))


def _mix_mlp_kernel(x_ref, yna_ref, ydf_ref, gate_ref, wna_ref, wdf_ref, wout_ref, g_ref, w1_ref,
                    w2_ref, fg_ref, o_ref, *, tf, final):
    d = x_ref.shape[1]
    b_na = jnp.dot(yna_ref[...], wna_ref[...], preferred_element_type=F32)
    b_df = jnp.dot(ydf_ref[...], wdf_ref[...], preferred_element_type=F32)
    merged = (jax.nn.sigmoid(gate_ref[:, 0:d]) * b_na
              + jax.nn.sigmoid(gate_ref[:, d:2 * d]) * b_df)
    x = x_ref[...] + jnp.dot(merged.astype(BF16), wout_ref[...], preferred_element_type=F32)
    h = _rmsnorm_f32(x, g_ref[...]).astype(BF16)
    for c in range(w1_ref.shape[1] // tf):
        u = jnp.dot(h, w1_ref[:, c * tf:(c + 1) * tf], preferred_element_type=F32)
        u = jnp.square(jnp.maximum(u, 0.0)).astype(BF16)
        x = x + jnp.dot(u, w2_ref[c * tf:(c + 1) * tf, :], preferred_element_type=F32)
    o_ref[...] = _rmsnorm_f32(x, fg_ref[...]) if final else x


def _mix_mlp(xt, y_na, y_df, gates, w_na, w_df, w_out, g, w1, w2, final_g, final):
    T, D = xt.shape
    F = w1.shape[1]
    tm, tf = TM_MLP, TF_MLP
    once = pl.Buffered(1)
    assert F % tf == 0
    weights = (w_na, w_df, w_out, w1, w2)
    vmem = _vmem_limit(
        2 * _nbytes((tm, D), F32, 2),
        _nbytes((tm, NA_WIDTH), BF16, 2), _nbytes((tm, DIFF_WIDTH), BF16, 2),
        _nbytes((tm, 2 * D), F32, 2), *[_nbytes(w.shape, BF16) for w in weights])
    return pl.pallas_call(
        functools.partial(_mix_mlp_kernel, tf=tf, final=final),
        out_shape=jax.ShapeDtypeStruct((T, D), F32),
        grid=(T // tm,),
        in_specs=[
            pl.BlockSpec((tm, D), lambda i: (i, 0)),
            pl.BlockSpec((tm, NA_WIDTH), lambda i: (i, 0)),
            pl.BlockSpec((tm, DIFF_WIDTH), lambda i: (i, 0)),
            pl.BlockSpec((tm, 2 * D), lambda i: (i, 0)),
            pl.BlockSpec((NA_WIDTH, D), lambda i: (0, 0), pipeline_mode=once),
            pl.BlockSpec((DIFF_WIDTH, D), lambda i: (0, 0), pipeline_mode=once),
            pl.BlockSpec((D, D), lambda i: (0, 0), pipeline_mode=once),
            pl.BlockSpec((1, D), lambda i: (0, 0)),
            pl.BlockSpec((D, F), lambda i: (0, 0), pipeline_mode=once),
            pl.BlockSpec((F, D), lambda i: (0, 0), pipeline_mode=once),
            pl.BlockSpec((1, D), lambda i: (0, 0)),
        ],
        out_specs=pl.BlockSpec((tm, D), lambda i: (i, 0)),
        compiler_params=pltpu.CompilerParams(
            dimension_semantics=("arbitrary",), vmem_limit_bytes=vmem),
        name="mix_mlp",
    )(xt, y_na, y_df, gates, w_na, w_df, w_out, g.reshape(1, D), w1, w2, final_g.reshape(1, D))


def kernel(x, t5_bias, final_norm_g, norm1_g, w_in, na_rpb, diff_lambda, diff_subln_g, w_na_o,
           w_diff_o, w_out, norm2_g, w_ff1, w_ff2):
    B, S, D = x.shape
    depth = w_in.shape[0]
    T = B * S
    assert w_in.shape[2] == QKV_WIDTH + 2 * D and S % GRID_W == 0
    xt = x.reshape(T, D)
    na_bias, w_in_b = _na_bias_table(na_rpb, S // GRID_W, w_in)
    for layer in range(depth):
        qkv, gates = _in_proj(xt, norm1_g[layer], w_in_b)
        qkv3 = qkv.reshape(B, S, QKV_WIDTH)
        casts = [(w, layer) for w in (w_na_o, w_diff_o, w_out, w_ff1, w_ff2)]
        if layer + 1 < depth:
            casts.append((w_in, layer + 1))
        y_na, (w_na_b, w_df_b, w_out_b, w1_b, w2_b, *w_in_next) = _na_attention(
            qkv3, na_bias, layer, casts)
        lam_init = 0.8 - 0.6 * math.exp(-0.3 * layer)
        y_df = _diff_attention(qkv3, t5_bias, diff_lambda[layer], diff_subln_g[layer], lam_init)
        xt = _mix_mlp(xt, y_na.reshape(T, NA_WIDTH), y_df.reshape(T, DIFF_WIDTH), gates,
                      w_na_b, w_df_b, w_out_b, norm2_g[layer], w1_b, w2_b, final_norm_g,
                      final=(layer == depth - 1))
        if w_in_next:
            w_in_b = w_in_next[0]
    return xt.reshape(B, S, D)
```

```python
import functools
import math

import numpy as np
import jax
import jax.numpy as jnp
from jax import lax
from jax.experimental import pallas as pl
from jax.experimental.pallas import tpu as pltpu

GRID_W = 64
NA_HEADS = 8
NA_HEAD_DIM = 64
NA_WIN_ROWS = 8
NA_WIN_COLS = 16
DIFF_HEADS = 4
DIFF_QK_DIM = 64
DIFF_V_DIM = 2 * DIFF_QK_DIM
T5_BUCKETS = 32
T5_MAX_DIST = 128
RMS_EPS = 1e-6
NEG_INF = -1e30

NA_WIDTH = NA_HEADS * NA_HEAD_DIM
DIFF_WIDTH = DIFF_HEADS * DIFF_V_DIM
QKV_WIDTH = 3 * NA_WIDTH + 3 * DIFF_WIDTH
QKV_GROUP = NA_WIDTH
QKV_ORDER = (1, 2, 0, 3, 4, 5)

LANES = 128
BF16_SUBLANES = 16
V7X_VMEM_BYTES = 64 * 2**20
VMEM_TEMP_BYTES = 12 * 2**20

TM_PROJ = 1024
TR_PROJ = 512
TN_PROJ = 1024
TM_MLP = 512
TF_MLP = 1024
NA_ROW_BLOCK = 4
NA_WIN_BLOCKS = 3
NA_BATCH_BLOCK = 4
TQ_DIFF = 256
FAR_PARTS = 3

BF16 = jnp.bfloat16
F32 = jnp.float32


def _nbytes(shape, dtype, buffers=1):
    return buffers * math.prod(shape) * jnp.dtype(dtype).itemsize


def _vmem_limit(*window_bytes):
    total = sum(window_bytes) + VMEM_TEMP_BYTES
    assert total < V7X_VMEM_BYTES
    return int(total)


def _aligned(index, multiple):
    return index if isinstance(index, int) else pl.multiple_of(index, multiple)


def _rmsnorm_f32(x, g):
    return (x * lax.rsqrt(jnp.mean(x * x, axis=-1, keepdims=True) + RMS_EPS)) * g


def _in_proj_kernel(x_ref, g_ref, w_ref, qkv_ref, gate_ref, *, rows, chunk):
    n_qkv = qkv_ref.shape[1]
    n_gate = gate_ref.shape[1]
    for r in range(x_ref.shape[0] // rows):
        rs = slice(r * rows, (r + 1) * rows)
        hb = _rmsnorm_f32(x_ref[rs, :], g_ref[...]).astype(BF16)
        for c in range(n_qkv // chunk):
            cs = slice(c * chunk, (c + 1) * chunk)
            res = jnp.dot(hb, w_ref[:, cs], preferred_element_type=F32).astype(BF16)
            for i in range(chunk // QKV_GROUP):
                dst = QKV_ORDER.index(c * (chunk // QKV_GROUP) + i) * QKV_GROUP
                qkv_ref[rs, dst:dst + QKV_GROUP] = res[:, i * QKV_GROUP:(i + 1) * QKV_GROUP]
        for c in range(n_gate // chunk):
            cs = slice(c * chunk, (c + 1) * chunk)
            ws = slice(n_qkv + c * chunk, n_qkv + (c + 1) * chunk)
            gate_ref[rs, cs] = jnp.dot(hb, w_ref[:, ws], preferred_element_type=F32)


def _in_proj(xt, g, w):
    T, D = xt.shape
    n_all = w.shape[1]
    n_gate = n_all - QKV_WIDTH
    tm = TM_PROJ
    vmem = _vmem_limit(_nbytes((tm, D), F32, 2), _nbytes((D, n_all), BF16),
                       _nbytes((tm, QKV_WIDTH), BF16, 2), _nbytes((tm, n_gate), F32, 2))
    return pl.pallas_call(
        functools.partial(_in_proj_kernel, rows=TR_PROJ, chunk=TN_PROJ),
        out_shape=(jax.ShapeDtypeStruct((T, QKV_WIDTH), BF16),
                   jax.ShapeDtypeStruct((T, n_gate), F32)),
        grid=(T // tm,),
        in_specs=[
            pl.BlockSpec((tm, D), lambda i: (i, 0)),
            pl.BlockSpec((1, D), lambda i: (0, 0)),
            pl.BlockSpec((D, n_all), lambda i: (0, 0), pipeline_mode=pl.Buffered(1)),
        ],
        out_specs=(pl.BlockSpec((tm, QKV_WIDTH), lambda i: (i, 0)),
                   pl.BlockSpec((tm, n_gate), lambda i: (i, 0))),
        compiler_params=pltpu.CompilerParams(
            dimension_semantics=("arbitrary",), vmem_limit_bytes=vmem),
        name="in_proj",
    )(xt, g.reshape(1, D), w)


def _na_geometry(rows):
    rb_rows = NA_ROW_BLOCK
    n_rb = rows // rb_rows
    win_rows = NA_WIN_BLOCKS * rb_rows
    wr = min(NA_WIN_ROWS, rows)
    kb = np.clip(np.arange(n_rb) - 1, 0, n_rb - NA_WIN_BLOCKS)
    dr_idx = np.zeros((n_rb, rb_rows, win_rows), np.int32)
    valid = np.zeros((n_rb, rb_rows, win_rows), bool)
    for rb in range(n_rb):
        w0 = kb[rb] * rb_rows
        for ri in range(rb_rows):
            r = rb * rb_rows + ri
            r0 = min(max(r - wr // 2, 0), rows - wr)
            assert w0 <= r0 and r0 + wr <= w0 + win_rows
            for wj in range(win_rows):
                krow = w0 + wj
                valid[rb, ri, wj] = r0 <= krow < r0 + wr
                dr_idx[rb, ri, wj] = min(max(krow - r + NA_WIN_ROWS - 1, 0), 2 * NA_WIN_ROWS - 2)
    for rb in range(2, n_rb - 1):
        assert (valid[rb] == valid[1]).all() and (dr_idx[rb] == dr_idx[1]).all()
    classes = [0, 1, n_rb - 1]
    return n_rb, kb, dr_idx[classes], valid[classes]


def _na_bias_kernel(rpb_ref, w_ref, o_ref, wb_ref, *, dr_idx, row_valid):
    assert LANES == 2 * GRID_W
    n_cls, rb_rows, win_rows = dr_idx.shape
    c = lax.broadcasted_iota(jnp.int32, (GRID_W, LANES), 0)
    lane = lax.broadcasted_iota(jnp.int32, (GRID_W, LANES), 1)
    left = lane < GRID_W
    kc = jnp.where(left, lane, lane - GRID_W)
    c0 = jnp.clip(c - NA_WIN_COLS // 2, 0, GRID_W - NA_WIN_COLS)
    col_ok = (kc >= c0) & (kc < c0 + NA_WIN_COLS)
    neg = jnp.full((GRID_W, LANES), NEG_INF, F32)

    @functools.lru_cache(maxsize=None)
    def half_tile(d, right):
        w = jnp.broadcast_to(rpb_ref[d:d + 1, :], (GRID_W, LANES))
        t = pltpu.roll(w, GRID_W if right else 0, 1, stride=1, stride_axis=0)
        return jnp.where(col_ok, t, neg)

    for cls in range(n_cls):
        for ri in range(rb_rows):
            for p in range(win_rows // 2):
                halves = []
                for side in range(2):
                    wj = 2 * p + side
                    ok = bool(row_valid[cls, ri, wj])
                    halves.append(half_tile(int(dr_idx[cls, ri, wj]), side == 1) if ok else neg)
                o_ref[cls, ri * GRID_W:(ri + 1) * GRID_W, p * LANES:(p + 1) * LANES] = (
                    jnp.where(left, halves[0], halves[1]))
    wb_ref[...] = w_ref[...].astype(BF16)


def _na_bias_table(na_rpb, rows, w_in):
    depth, H, n_dr, n_dc = na_rpb.shape
    assert n_dr == 2 * NA_WIN_ROWS - 1 and n_dc == 2 * NA_WIN_COLS - 1
    _, _, dr_idx, row_valid = _na_geometry(rows)
    n_cls, rb_rows, win_rows = dr_idx.shape
    rq, wk = rb_rows * GRID_W, win_rows * GRID_W
    lanes = jnp.pad(na_rpb, ((0, 0), (0, 0), (0, 0), (0, LANES - n_dc)), constant_values=NEG_INF)
    lanes = jnp.roll(lanes, -(NA_WIN_COLS - 1), axis=-1)
    n_steps = depth * H
    _, R, C = w_in.shape
    slab = R // n_steps
    assert R % n_steps == 0 and slab % BF16_SUBLANES == 0
    table, w0 = pl.pallas_call(
        functools.partial(_na_bias_kernel, dr_idx=dr_idx, row_valid=row_valid),
        out_shape=(jax.ShapeDtypeStruct((depth, n_cls, H, rq, wk), F32),
                   jax.ShapeDtypeStruct((n_steps, slab, C), BF16)),
        grid=(depth, H),
        in_specs=[pl.BlockSpec((None, None, n_dr, LANES), lambda l, h: (l, h, 0, 0)),
                  pl.BlockSpec((None, slab, C), lambda l, h: (l * H + h, 0, 0))],
        out_specs=(pl.BlockSpec((None, n_cls, None, rq, wk), lambda l, h: (l, 0, h, 0, 0)),
                   pl.BlockSpec((None, slab, C), lambda l, h: (l * H + h, 0, 0))),
        compiler_params=pltpu.CompilerParams(dimension_semantics=("arbitrary", "arbitrary")),
        name="na_bias",
    )(lanes, w_in.reshape(-1, slab, C))
    return table, w0.reshape(R, C)


def _na_kernel(q_ref, kv0_ref, kv1_ref, kv2_ref, bias_ref, *rest):
    n_cast = len(rest) // 2
    o_ref = rest[n_cast]
    for w_ref, wb_ref in zip(rest[:n_cast], rest[n_cast + 1:]):
        wb_ref[...] = w_ref[...].astype(BF16)
    nb, rq = q_ref.shape[0], q_ref.shape[1]
    scale = 1.0 / math.sqrt(NA_HEAD_DIM)
    lo = lax.broadcasted_iota(jnp.int32, (rq, LANES), 1) < NA_HEAD_DIM
    heads_per_vreg = LANES // NA_HEAD_DIM
    for bi, hp in np.ndindex(nb, NA_HEADS // heads_per_vreg):
        cs = (bi, slice(None), slice(hp * LANES, (hp + 1) * LANES))
        q = q_ref[cs] * scale
        cv = (bi, slice(None), slice(NA_WIDTH + hp * LANES, NA_WIDTH + (hp + 1) * LANES))
        k = jnp.concatenate([r[cs] for r in (kv0_ref, kv1_ref, kv2_ref)], axis=0)
        v = jnp.concatenate([r[cv] for r in (kv0_ref, kv1_ref, kv2_ref)], axis=0)
        v_ones = jnp.concatenate([v, jnp.ones_like(v)], axis=1)
        outs = []
        for e in range(heads_per_vreg):
            qm = jnp.where(lo if e == 0 else jnp.logical_not(lo), q, jnp.zeros_like(q))
            s = lax.dot_general(qm, k, (((1,), (1,)), ((), ())), preferred_element_type=F32)
            s = s + bias_ref[heads_per_vreg * hp + e]
            m = jnp.max(s, axis=-1, keepdims=True)
            p = jnp.exp((s - m).astype(BF16))
            ol = jnp.dot(p, v_ones, preferred_element_type=F32)
            outs.append(ol[:, 0:LANES] / ol[:, LANES:2 * LANES])
        o_ref[cs] = jnp.where(lo, outs[0], outs[1]).astype(BF16)


def _na_attention(qkv3, bias_table, layer, casts):
    B, S, _ = qkv3.shape
    rows = S // GRID_W
    n_rb, kb, _, _ = _na_geometry(rows)
    rq = NA_ROW_BLOCK * GRID_W
    assert NA_HEADS * NA_HEAD_DIM == NA_WIDTH and LANES % NA_HEAD_DIM == 0
    assert QKV_ORDER[:3] == (1, 2, 0)
    q_col = 2
    n_kb = n_rb - NA_WIN_BLOCKS

    nb = NA_BATCH_BLOCK
    assert B % nb == 0

    def kv_spec(t):
        return pl.BlockSpec((nb, rq, 2 * NA_WIDTH),
                            lambda rb, b: (b, jnp.clip(rb - 1, 0, n_kb) + t, 0))

    def bias_map(rb, b):
        cls = (rb > 0).astype(jnp.int32) + (rb == n_rb - 1).astype(jnp.int32)
        return (layer, cls, 0, 0, 0)

    n_bb = B // nb
    n_steps = n_rb * n_bb
    cast_in, cast_in_specs, cast_out, cast_out_specs = [], [], [], []
    for w, w_layer in casts:
        depth, R, C = w.shape
        slab = R // n_steps
        assert R % n_steps == 0 and slab % BF16_SUBLANES == 0
        cast_in.append(w.reshape(depth * n_steps, slab, C))
        first = w_layer * n_steps
        cast_in_specs.append(pl.BlockSpec(
            (None, slab, C), lambda rb, b, first=first: (first + rb * n_bb + b, 0, 0)))
        cast_out.append(jax.ShapeDtypeStruct((n_steps, slab, C), BF16))
        cast_out_specs.append(pl.BlockSpec((None, slab, C), lambda rb, b: (rb * n_bb + b, 0, 0)))

    wk = NA_WIN_BLOCKS * rq
    n_qkv_windows = 2 + 2 * NA_WIN_BLOCKS
    vmem = _vmem_limit(
        n_qkv_windows * _nbytes((nb, rq, NA_WIDTH), BF16, 2),
        _nbytes((NA_HEADS, rq, wk), F32, 2),
        *[_nbytes(o.shape[1:], F32, 2) + _nbytes(o.shape[1:], BF16, 2) for o in cast_out])
    y, *cast = pl.pallas_call(
        _na_kernel,
        out_shape=[jax.ShapeDtypeStruct((B, S, NA_WIDTH), BF16)] + cast_out,
        grid=(n_rb, n_bb),
        in_specs=[pl.BlockSpec((nb, rq, NA_WIDTH), lambda rb, b: (b, rb, q_col))]
        + [kv_spec(t) for t in range(NA_WIN_BLOCKS)]
        + [pl.BlockSpec((None, None, NA_HEADS, rq, wk), bias_map)]
        + cast_in_specs,
        out_specs=[pl.BlockSpec((nb, rq, NA_WIDTH), lambda rb, b: (b, rb, 0))] + cast_out_specs,
        compiler_params=pltpu.CompilerParams(
            dimension_semantics=("arbitrary", "arbitrary"), vmem_limit_bytes=vmem),
        name="na_attn",
    )(qkv3, qkv3, qkv3, qkv3, bias_table, *cast_in)
    return y, [c.reshape(w.shape[1], w.shape[2]) for c, (w, _) in zip(cast, casts)]


def _t5_bucket_steps(seq):
    half = T5_BUCKETS // 2
    max_exact = half // 2
    rel = np.arange(-(seq - 1), seq)
    n = np.abs(rel)
    nf = np.maximum(n, 1).astype(np.float64)
    large = max_exact + np.floor(
        np.log(nf / max_exact) / math.log(T5_MAX_DIST / max_exact) * (half - max_exact) + 1e-9
    ).astype(np.int64)
    large = np.minimum(large, half - 1)
    bucket = np.where(rel > 0, half, 0) + np.where(n < max_exact, n, large)
    steps = [(int(rel[i]), int(bucket[i])) for i in range(1, len(rel)) if bucket[i] != bucket[i - 1]]
    sat = max(abs(steps[0][0]) + 1, abs(steps[-1][0]))
    return int(bucket[0]), steps, sat


def _diff_kernel(t5_ref, qkv_ref, lam_ref, g_ref, o_ref, *scratch, tq, n_buf, lam_init,
                 bucket_steps):
    s_refs, m_refs, ol_refs = (scratch[i * n_buf:(i + 1) * n_buf] for i in range(3))
    corr_ref, kfar_ref = scratch[3 * n_buf:]
    b = pl.program_id(0)
    seq = qkv_ref.shape[0]
    n_t = seq // tq
    band = 3 * tq
    first_bucket, steps, _ = bucket_steps
    last_bucket = steps[-1][1]
    split = tq + tq // 2
    one_lane = FAR_PARTS * n_t
    assert one_lane + FAR_PARTS <= LANES and n_t >= 4

    @pl.when(b == 0)
    def _init():
        row = lax.broadcasted_iota(jnp.int32, (tq, band), 0)
        col = lax.broadcasted_iota(jnp.int32, (tq, band), 1)
        rel = col - tq - row
        for hh in range(DIFF_HEADS):
            val = jnp.full((tq, band), t5_ref[first_bucket, hh], F32)
            for thr, bkt in steps:
                val = jnp.where(rel >= thr, t5_ref[bkt, hh], val)
            far = jnp.where(col < split, t5_ref[first_bucket, hh], t5_ref[last_bucket, hh])
            corr_ref[hh] = val - far
        corr_ref[DIFF_HEADS] = jnp.zeros((tq, band), F32)
        key = lax.broadcasted_iota(jnp.int32, (seq, LANES), 0)
        lane = lax.broadcasted_iota(jnp.int32, (seq, LANES), 1)
        tile = lax.rem(lane, n_t)
        step = jnp.where(key >= tile * tq + tq // 2, 1.0, 0.0)
        ones = jnp.where(lane < one_lane + FAR_PARTS, 1.0, 0.0)
        kfar_ref[...] = jnp.where(lane < one_lane, step, ones).astype(BF16)

    scale = 1.0 / math.sqrt(DIFF_QK_DIM)
    lane = lax.broadcasted_iota(jnp.int32, (tq, LANES), 1)
    lo = lane < DIFF_QK_DIM
    lp = lam_ref[...]
    lam = (jnp.exp(jnp.sum(lp[0:1] * lp[1:2], axis=-1, keepdims=True))
           - jnp.exp(jnp.sum(lp[2:3] * lp[3:4], axis=-1, keepdims=True)) + lam_init)

    chunk_order = [-1, 0, 1] + list(range(2, n_t - 1))

    def tile_ids(t):
        if isinstance(t, int):
            h, qi = divmod(t, n_t)
        else:
            h, qi = lax.div(t, n_t), lax.rem(t, n_t)
        return h, qi, pl.ds(_aligned(h * LANES, LANES), LANES)

    def qkv_cols(h, part):
        return pl.ds(_aligned(part * DIFF_WIDTH + h * LANES, LANES), LANES)

    def key_rows(qi, d):
        chunk = (qi + d) % n_t if isinstance(qi, int) else lax.rem(qi + (d + n_t), n_t)
        return pl.ds(_aligned(chunk * tq, tq), tq)

    def scores(t, s_ref, m_ref):
        h, qi, _ = tile_ids(t)
        c_neg = t5_ref[first_bucket, h]
        c_step = t5_ref[last_bucket, h] - c_neg
        row0 = _aligned(qi * tq, tq)
        q = qkv_ref[pl.ds(row0, tq), qkv_cols(h, 0)] * scale
        zero = jnp.zeros_like(q)
        far = jnp.where(lane < one_lane, jnp.where(lax.rem(lane, n_t) == qi, c_step, 0.0),
                        jnp.where(lane < one_lane + FAR_PARTS, c_neg, 0.0))
        part = jnp.where(lane < one_lane, lax.div(lane, n_t), lane - one_lane)
        q_far = jnp.zeros((tq, LANES), BF16)
        for i in range(FAR_PARTS):
            piece = far.astype(BF16)
            q_far = jnp.where(part == i, piece, q_far)
            far = far - piece.astype(F32)
        lhs = jnp.concatenate(
            [jnp.concatenate([jnp.where(lo, q, zero), q_far], axis=1),
             jnp.concatenate([jnp.where(lo, zero, q), q_far], axis=1)], axis=0)
        m_run = None
        for pos, d in enumerate(chunk_order):
            rows = key_rows(qi, d)
            rhs = jnp.concatenate([qkv_ref[rows, qkv_cols(h, 1)], kfar_ref[rows, :]], axis=1)
            s = lax.dot_general(lhs, rhs, (((1,), (1,)), ((), ())), preferred_element_type=F32)
            if d in (-1, 0, 1):
                inside = (qi + d >= 0) & (qi + d < n_t)
                head = jnp.where(inside, h, DIFF_HEADS)
                corr = corr_ref[head, :, (d + 1) * tq:(d + 2) * tq]
                s = s + jnp.concatenate([corr, corr], axis=0)
            s_ref[:, pos * tq:(pos + 1) * tq] = s
            m_chunk = functools.reduce(
                jnp.maximum, [s[:, i * LANES:(i + 1) * LANES] for i in range(tq // LANES)])
            m_run = m_chunk if m_run is None else jnp.maximum(m_run, m_chunk)
        m_ref[...] = m_run

    def attend(t, s_ref, m_ref, ol_ref):
        h, qi, _ = tile_ids(t)
        m = jnp.broadcast_to(jnp.max(m_ref[...], axis=-1, keepdims=True), (2 * tq, LANES))
        m = jnp.concatenate([m] * (tq // LANES), axis=1)
        ol = None
        for pos, d in enumerate(chunk_order):
            e = jnp.exp((s_ref[:, pos * tq:(pos + 1) * tq] - m).astype(BF16))
            v = qkv_ref[key_rows(qi, d), qkv_cols(h, 2)]
            part = jnp.dot(e, jnp.concatenate([v, jnp.ones_like(v)], axis=1),
                           preferred_element_type=F32)
            ol = part if ol is None else ol + part
        ol_ref[...] = ol

    def finish(t, ol_ref):
        _, qi, cols = tile_ids(t)
        o = (ol_ref[0:tq, 0:LANES] / ol_ref[0:tq, LANES:2 * LANES]
             - lam * (ol_ref[tq:2 * tq, 0:LANES] / ol_ref[tq:2 * tq, LANES:2 * LANES]))
        row0 = _aligned(qi * tq, tq)
        o_ref[pl.ds(row0, tq), cols] = (
            _rmsnorm_f32(o, g_ref[...]) * (1.0 - lam_init)).astype(BF16)

    n_tiles = DIFF_HEADS * n_t

    def step(u, slot):
        live = (lambda t: True) if not isinstance(u, int) else (lambda t: 0 <= t < n_tiles)
        if live(u):
            scores(u, s_refs[slot], m_refs[slot])
        if live(u - 3):
            finish(u - 3, ol_refs[(slot - 3) % n_buf])
        if live(u - 2):
            a = (slot - 2) % n_buf
            attend(u - 2, s_refs[a], m_refs[a], ol_refs[a])

    first, trips = 3, (n_tiles - 3) // n_buf
    assert trips >= 1
    for u in range(first):
        step(u, u % n_buf)

    def rotation(j, carry):
        u = first + n_buf * j
        for i in range(n_buf):
            step(u + i, (first + i) % n_buf)
        return carry

    lax.fori_loop(0, trips, rotation, 0)
    for u in range(first + n_buf * trips, n_tiles + 3):
        step(u, u % n_buf)


def _diff_attention(qkv3, t5_bias, lam_params, subln_g, lam_init):
    B, S, _ = qkv3.shape
    tq = TQ_DIFF
    bucket_steps = _t5_bucket_steps(S)
    assert bucket_steps[2] <= tq and S % tq == 0 and tq % LANES == 0
    assert 2 * DIFF_QK_DIM == LANES and DIFF_V_DIM == LANES
    assert 3 * NA_WIDTH == 3 * DIFF_WIDTH
    n_buf = 3
    scratch = (
        [((2 * tq, S), F32)] * n_buf
        + [((2 * tq, LANES), F32)] * n_buf
        + [((2 * tq, 2 * LANES), F32)] * n_buf
        + [((DIFF_HEADS + 1, tq, 3 * tq), F32),
           ((S, LANES), BF16)])
    vmem = _vmem_limit(4 * _nbytes((S, DIFF_WIDTH), BF16, 2), *[_nbytes(*s) for s in scratch])
    return pl.pallas_call(
        functools.partial(_diff_kernel, tq=tq, n_buf=n_buf, lam_init=lam_init,
                          bucket_steps=bucket_steps),
        out_shape=jax.ShapeDtypeStruct((B, S, DIFF_WIDTH), BF16),
        grid=(B,),
        in_specs=[
            pl.BlockSpec(memory_space=pltpu.SMEM),
            pl.BlockSpec((None, S, 3 * DIFF_WIDTH), lambda b: (b, 0, 1)),
            pl.BlockSpec((4, DIFF_QK_DIM), lambda b: (0, 0)),
            pl.BlockSpec((1, DIFF_V_DIM), lambda b: (0, 0)),
        ],
        out_specs=pl.BlockSpec((None, S, DIFF_WIDTH), lambda b: (b, 0, 0)),
        scratch_shapes=[pltpu.VMEM(*s) for s in scratch],
        compiler_params=pltpu.CompilerParams(
            dimension_semantics=("arbitrary",), vmem_limit_bytes=vmem),
        name="diff_attn",
    )(t5_bias, qkv3, lam_params, subln_g.reshape(1, DIFF_V_DIM))


def _mix_mlp_kernel(x_ref, yna_ref, ydf_ref, gate_ref, wna_ref, wdf_ref, wout_ref, g_ref, w1_ref,
                    w2_ref, fg_ref, o_ref, *, tf, final):
    d = x_ref.shape[1]
    b_na = jnp.dot(yna_ref[...], wna_ref[...], preferred_element_type=F32)
    b_df = jnp.dot(ydf_ref[...], wdf_ref[...], preferred_element_type=F32)
    merged = (jax.nn.sigmoid(gate_ref[:, 0:d]) * b_na
              + jax.nn.sigmoid(gate_ref[:, d:2 * d]) * b_df)
    x = x_ref[...] + jnp.dot(merged.astype(BF16), wout_ref[...], preferred_element_type=F32)
    h = _rmsnorm_f32(x, g_ref[...]).astype(BF16)
    for c in range(w1_ref.shape[1] // tf):
        u = jnp.dot(h, w1_ref[:, c * tf:(c + 1) * tf], preferred_element_type=F32)
        u = jnp.square(jnp.maximum(u, 0.0)).astype(BF16)
        x = x + jnp.dot(u, w2_ref[c * tf:(c + 1) * tf, :], preferred_element_type=F32)
    o_ref[...] = _rmsnorm_f32(x, fg_ref[...]) if final else x


def _mix_mlp(xt, y_na, y_df, gates, w_na, w_df, w_out, g, w1, w2, final_g, final):
    T, D = xt.shape
    F = w1.shape[1]
    tm, tf = TM_MLP, TF_MLP
    once = pl.Buffered(1)
    assert F % tf == 0
    weights = (w_na, w_df, w_out, w1, w2)
    vmem = _vmem_limit(
        2 * _nbytes((tm, D), F32, 2),
        _nbytes((tm, NA_WIDTH), BF16, 2), _nbytes((tm, DIFF_WIDTH), BF16, 2),
        _nbytes((tm, 2 * D), F32, 2), *[_nbytes(w.shape, BF16) for w in weights])
    return pl.pallas_call(
        functools.partial(_mix_mlp_kernel, tf=tf, final=final),
        out_shape=jax.ShapeDtypeStruct((T, D), F32),
        grid=(T // tm,),
        in_specs=[
            pl.BlockSpec((tm, D), lambda i: (i, 0)),
            pl.BlockSpec((tm, NA_WIDTH), lambda i: (i, 0)),
            pl.BlockSpec((tm, DIFF_WIDTH), lambda i: (i, 0)),
            pl.BlockSpec((tm, 2 * D), lambda i: (i, 0)),
            pl.BlockSpec((NA_WIDTH, D), lambda i: (0, 0), pipeline_mode=once),
            pl.BlockSpec((DIFF_WIDTH, D), lambda i: (0, 0), pipeline_mode=once),
            pl.BlockSpec((D, D), lambda i: (0, 0), pipeline_mode=once),
            pl.BlockSpec((1, D), lambda i: (0, 0)),
            pl.BlockSpec((D, F), lambda i: (0, 0), pipeline_mode=once),
            pl.BlockSpec((F, D), lambda i: (0, 0), pipeline_mode=once),
            pl.BlockSpec((1, D), lambda i: (0, 0)),
        ],
        out_specs=pl.BlockSpec((tm, D), lambda i: (i, 0)),
        compiler_params=pltpu.CompilerParams(
            dimension_semantics=("arbitrary",), vmem_limit_bytes=vmem),
        name="mix_mlp",
    )(xt, y_na, y_df, gates, w_na, w_df, w_out, g.reshape(1, D), w1, w2, final_g.reshape(1, D))


def kernel(x, t5_bias, final_norm_g, norm1_g, w_in, na_rpb, diff_lambda, diff_subln_g, w_na_o,
           w_diff_o, w_out, norm2_g, w_ff1, w_ff2):
    B, S, D = x.shape
    depth = w_in.shape[0]
    T = B * S
    assert w_in.shape[2] == QKV_WIDTH + 2 * D and S % GRID_W == 0
    xt = x.reshape(T, D)
    na_bias, w_in_b = _na_bias_table(na_rpb, S // GRID_W, w_in)
    for layer in range(depth):
        qkv, gates = _in_proj(xt, norm1_g[layer], w_in_b)
        qkv3 = qkv.reshape(B, S, QKV_WIDTH)
        casts = [(w, layer) for w in (w_na_o, w_diff_o, w_out, w_ff1, w_ff2)]
        if layer + 1 < depth:
            casts.append((w_in, layer + 1))
        y_na, (w_na_b, w_df_b, w_out_b, w1_b, w2_b, *w_in_next) = _na_attention(
            qkv3, na_bias, layer, casts)
        lam_init = 0.8 - 0.6 * math.exp(-0.3 * layer)
        y_df = _diff_attention(qkv3, t5_bias, diff_lambda[layer], diff_subln_g[layer], lam_init)
        xt = _mix_mlp(xt, y_na.reshape(T, NA_WIDTH), y_df.reshape(T, DIFF_WIDTH), gates,
                      w_na_b, w_df_b, w_out_b, norm2_g[layer], w1_b, w2_b, final_norm_g,
                      final=(layer == depth - 1))
        if w_in_next:
            w_in_b = w_in_next[0]
    return xt.reshape(B, S, D)
```

```python
import functools
import math

import numpy as np
import jax
import jax.numpy as jnp
from jax import lax
from jax.experimental import pallas as pl
from jax.experimental.pallas import tpu as pltpu

GRID_W = 64
NA_HEADS = 8
NA_HEAD_DIM = 64
NA_WIN_ROWS = 8
NA_WIN_COLS = 16
DIFF_HEADS = 4
DIFF_QK_DIM = 64
DIFF_V_DIM = 2 * DIFF_QK_DIM
T5_BUCKETS = 32
T5_MAX_DIST = 128
RMS_EPS = 1e-6
NEG_INF = -1e30

NA_WIDTH = NA_HEADS * NA_HEAD_DIM
DIFF_WIDTH = DIFF_HEADS * DIFF_V_DIM
QKV_WIDTH = 3 * NA_WIDTH + 3 * DIFF_WIDTH

LANES = 128
BF16_SUBLANES = 16
V7X_VMEM_BYTES = 64 * 2**20
VMEM_TEMP_BYTES = 12 * 2**20

TM_PROJ = 1024
TR_PROJ = 512
TN_PROJ = 1024
TM_MLP = 512
TF_MLP = 1024
NA_ROW_BLOCK = 4
NA_WIN_BLOCKS = 3
NA_BATCH_BLOCK = 4
TQ_DIFF = 256
FAR_PARTS = 3
DIFF_ROTATIONS = 2

BF16 = jnp.bfloat16
F32 = jnp.float32


def _nbytes(shape, dtype, buffers=1):
    return buffers * math.prod(shape) * jnp.dtype(dtype).itemsize


def _vmem_limit(*window_bytes):
    total = sum(window_bytes) + VMEM_TEMP_BYTES
    assert total < V7X_VMEM_BYTES
    return int(total)


def _aligned(index, multiple):
    return index if isinstance(index, int) else pl.multiple_of(index, multiple)


def _rmsnorm_f32(x, g):
    return (x * lax.rsqrt(jnp.mean(x * x, axis=-1, keepdims=True) + RMS_EPS)) * g


def _in_proj_kernel(x_ref, g_ref, w_ref, qkv_ref, gate_ref, *, rows, chunk):
    n_qkv = qkv_ref.shape[1]
    n_gate = gate_ref.shape[1]
    for r in range(x_ref.shape[0] // rows):
        rs = slice(r * rows, (r + 1) * rows)
        hb = _rmsnorm_f32(x_ref[rs, :], g_ref[...]).astype(BF16)
        for c in range(n_qkv // chunk):
            cs = slice(c * chunk, (c + 1) * chunk)
            qkv_ref[rs, cs] = jnp.dot(hb, w_ref[:, cs], preferred_element_type=F32).astype(BF16)
        for c in range(n_gate // chunk):
            cs = slice(c * chunk, (c + 1) * chunk)
            ws = slice(n_qkv + c * chunk, n_qkv + (c + 1) * chunk)
            gate_ref[rs, cs] = jnp.dot(hb, w_ref[:, ws], preferred_element_type=F32)


def _in_proj(xt, g, w):
    T, D = xt.shape
    n_all = w.shape[1]
    n_gate = n_all - QKV_WIDTH
    tm = TM_PROJ
    vmem = _vmem_limit(_nbytes((tm, D), F32, 2), _nbytes((D, n_all), BF16),
                       _nbytes((tm, QKV_WIDTH), BF16, 2), _nbytes((tm, n_gate), F32, 2))
    return pl.pallas_call(
        functools.partial(_in_proj_kernel, rows=TR_PROJ, chunk=TN_PROJ),
        out_shape=(jax.ShapeDtypeStruct((T, QKV_WIDTH), BF16),
                   jax.ShapeDtypeStruct((T, n_gate), F32)),
        grid=(T // tm,),
        in_specs=[
            pl.BlockSpec((tm, D), lambda i: (i, 0)),
            pl.BlockSpec((1, D), lambda i: (0, 0)),
            pl.BlockSpec((D, n_all), lambda i: (0, 0), pipeline_mode=pl.Buffered(1)),
        ],
        out_specs=(pl.BlockSpec((tm, QKV_WIDTH), lambda i: (i, 0)),
                   pl.BlockSpec((tm, n_gate), lambda i: (i, 0))),
        compiler_params=pltpu.CompilerParams(
            dimension_semantics=("arbitrary",), vmem_limit_bytes=vmem),
        name="in_proj",
    )(xt, g.reshape(1, D), w)


def _na_geometry(rows):
    rb_rows = NA_ROW_BLOCK
    n_rb = rows // rb_rows
    win_rows = NA_WIN_BLOCKS * rb_rows
    wr = min(NA_WIN_ROWS, rows)
    kb = np.clip(np.arange(n_rb) - 1, 0, n_rb - NA_WIN_BLOCKS)
    dr_idx = np.zeros((n_rb, rb_rows, win_rows), np.int32)
    valid = np.zeros((n_rb, rb_rows, win_rows), bool)
    for rb in range(n_rb):
        w0 = kb[rb] * rb_rows
        for ri in range(rb_rows):
            r = rb * rb_rows + ri
            r0 = min(max(r - wr // 2, 0), rows - wr)
            assert w0 <= r0 and r0 + wr <= w0 + win_rows
            for wj in range(win_rows):
                krow = w0 + wj
                valid[rb, ri, wj] = r0 <= krow < r0 + wr
                dr_idx[rb, ri, wj] = min(max(krow - r + NA_WIN_ROWS - 1, 0), 2 * NA_WIN_ROWS - 2)
    for rb in range(2, n_rb - 1):
        assert (valid[rb] == valid[1]).all() and (dr_idx[rb] == dr_idx[1]).all()
    classes = [0, 1, n_rb - 1]
    return n_rb, kb, dr_idx[classes], valid[classes]


def _na_bias_kernel(rpb_ref, w_ref, o_ref, wb_ref, *, dr_idx, row_valid):
    assert LANES == 2 * GRID_W
    n_cls, rb_rows, win_rows = dr_idx.shape
    c = lax.broadcasted_iota(jnp.int32, (GRID_W, LANES), 0)
    lane = lax.broadcasted_iota(jnp.int32, (GRID_W, LANES), 1)
    left = lane < GRID_W
    kc = jnp.where(left, lane, lane - GRID_W)
    c0 = jnp.clip(c - NA_WIN_COLS // 2, 0, GRID_W - NA_WIN_COLS)
    col_ok = (kc >= c0) & (kc < c0 + NA_WIN_COLS)
    neg = jnp.full((GRID_W, LANES), NEG_INF, F32)

    @functools.lru_cache(maxsize=None)
    def half_tile(d, right):
        w = jnp.broadcast_to(rpb_ref[d:d + 1, :], (GRID_W, LANES))
        t = pltpu.roll(w, GRID_W if right else 0, 1, stride=1, stride_axis=0)
        return jnp.where(col_ok, t, neg)

    for cls in range(n_cls):
        for ri in range(rb_rows):
            for p in range(win_rows // 2):
                halves = []
                for side in range(2):
                    wj = 2 * p + side
                    ok = bool(row_valid[cls, ri, wj])
                    halves.append(half_tile(int(dr_idx[cls, ri, wj]), side == 1) if ok else neg)
                o_ref[cls, ri * GRID_W:(ri + 1) * GRID_W, p * LANES:(p + 1) * LANES] = (
                    jnp.where(left, halves[0], halves[1]))
    wb_ref[...] = w_ref[...].astype(BF16)


def _na_bias_table(na_rpb, rows, w_in):
    depth, H, n_dr, n_dc = na_rpb.shape
    assert n_dr == 2 * NA_WIN_ROWS - 1 and n_dc == 2 * NA_WIN_COLS - 1
    _, _, dr_idx, row_valid = _na_geometry(rows)
    n_cls, rb_rows, win_rows = dr_idx.shape
    rq, wk = rb_rows * GRID_W, win_rows * GRID_W
    lanes = jnp.pad(na_rpb, ((0, 0), (0, 0), (0, 0), (0, LANES - n_dc)), constant_values=NEG_INF)
    lanes = jnp.roll(lanes, -(NA_WIN_COLS - 1), axis=-1)
    n_steps = depth * H
    _, R, C = w_in.shape
    slab = R // n_steps
    assert R % n_steps == 0 and slab % BF16_SUBLANES == 0
    table, w0 = pl.pallas_call(
        functools.partial(_na_bias_kernel, dr_idx=dr_idx, row_valid=row_valid),
        out_shape=(jax.ShapeDtypeStruct((depth, n_cls, H, rq, wk), F32),
                   jax.ShapeDtypeStruct((n_steps, slab, C), BF16)),
        grid=(depth, H),
        in_specs=[pl.BlockSpec((None, None, n_dr, LANES), lambda l, h: (l, h, 0, 0)),
                  pl.BlockSpec((None, slab, C), lambda l, h: (l * H + h, 0, 0))],
        out_specs=(pl.BlockSpec((None, n_cls, None, rq, wk), lambda l, h: (l, 0, h, 0, 0)),
                   pl.BlockSpec((None, slab, C), lambda l, h: (l * H + h, 0, 0))),
        compiler_params=pltpu.CompilerParams(dimension_semantics=("arbitrary", "arbitrary")),
        name="na_bias",
    )(lanes, w_in.reshape(-1, slab, C))
    return table, w0.reshape(R, C)


def _na_kernel(q_ref, k0_ref, k1_ref, k2_ref, v0_ref, v1_ref, v2_ref, bias_ref, *rest):
    n_cast = len(rest) // 2
    o_ref = rest[n_cast]
    for w_ref, wb_ref in zip(rest[:n_cast], rest[n_cast + 1:]):
        wb_ref[...] = w_ref[...].astype(BF16)
    nb, rq = q_ref.shape[0], q_ref.shape[1]
    scale = 1.0 / math.sqrt(NA_HEAD_DIM)
    lo = lax.broadcasted_iota(jnp.int32, (rq, LANES), 1) < NA_HEAD_DIM
    heads_per_vreg = LANES // NA_HEAD_DIM
    for bi, hp in np.ndindex(nb, NA_HEADS // heads_per_vreg):
        cs = (bi, slice(None), slice(hp * LANES, (hp + 1) * LANES))
        q = q_ref[cs] * scale
        k = jnp.concatenate([k0_ref[cs], k1_ref[cs], k2_ref[cs]], axis=0)
        v = jnp.concatenate([v0_ref[cs], v1_ref[cs], v2_ref[cs]], axis=0)
        v_ones = jnp.concatenate([v, jnp.ones_like(v)], axis=1)
        outs = []
        for e in range(heads_per_vreg):
            qm = jnp.where(lo if e == 0 else jnp.logical_not(lo), q, jnp.zeros_like(q))
            s = lax.dot_general(qm, k, (((1,), (1,)), ((), ())), preferred_element_type=F32)
            s = s + bias_ref[heads_per_vreg * hp + e]
            m = jnp.max(s, axis=-1, keepdims=True)
            p = jnp.exp((s - m).astype(BF16))
            ol = jnp.dot(p, v_ones, preferred_element_type=F32)
            outs.append(ol[:, 0:LANES] / ol[:, LANES:2 * LANES])
        o_ref[cs] = jnp.where(lo, outs[0], outs[1]).astype(BF16)


def _na_attention(qkv3, bias_table, layer, casts):
    B, S, _ = qkv3.shape
    rows = S // GRID_W
    n_rb, kb, _, _ = _na_geometry(rows)
    rq = NA_ROW_BLOCK * GRID_W
    assert NA_HEADS * NA_HEAD_DIM == NA_WIDTH and LANES % NA_HEAD_DIM == 0
    k_col, v_col = 1, 2
    n_kb = n_rb - NA_WIN_BLOCKS

    nb = NA_BATCH_BLOCK
    assert B % nb == 0

    def kv_spec(col, t):
        return pl.BlockSpec((nb, rq, NA_WIDTH),
                            lambda rb, b: (b, jnp.clip(rb - 1, 0, n_kb) + t, col))

    def bias_map(rb, b):
        cls = (rb > 0).astype(jnp.int32) + (rb == n_rb - 1).astype(jnp.int32)
        return (layer, cls, 0, 0, 0)

    n_bb = B // nb
    n_steps = n_rb * n_bb
    cast_in, cast_in_specs, cast_out, cast_out_specs = [], [], [], []
    for w, w_layer in casts:
        depth, R, C = w.shape
        slab = R // n_steps
        assert R % n_steps == 0 and slab % BF16_SUBLANES == 0
        cast_in.append(w.reshape(depth * n_steps, slab, C))
        first = w_layer * n_steps
        cast_in_specs.append(pl.BlockSpec(
            (None, slab, C), lambda rb, b, first=first: (first + rb * n_bb + b, 0, 0)))
        cast_out.append(jax.ShapeDtypeStruct((n_steps, slab, C), BF16))
        cast_out_specs.append(pl.BlockSpec((None, slab, C), lambda rb, b: (rb * n_bb + b, 0, 0)))

    wk = NA_WIN_BLOCKS * rq
    n_qkv_windows = 2 + 2 * NA_WIN_BLOCKS
    vmem = _vmem_limit(
        n_qkv_windows * _nbytes((nb, rq, NA_WIDTH), BF16, 2),
        _nbytes((NA_HEADS, rq, wk), F32, 2),
        *[_nbytes(o.shape[1:], F32, 2) + _nbytes(o.shape[1:], BF16, 2) for o in cast_out])
    y, *cast = pl.pallas_call(
        _na_kernel,
        out_shape=[jax.ShapeDtypeStruct((B, S, NA_WIDTH), BF16)] + cast_out,
        grid=(n_rb, n_bb),
        in_specs=[pl.BlockSpec((nb, rq, NA_WIDTH), lambda rb, b: (b, rb, 0))]
        + [kv_spec(k_col, t) for t in range(NA_WIN_BLOCKS)]
        + [kv_spec(v_col, t) for t in range(NA_WIN_BLOCKS)]
        + [pl.BlockSpec((None, None, NA_HEADS, rq, wk), bias_map)]
        + cast_in_specs,
        out_specs=[pl.BlockSpec((nb, rq, NA_WIDTH), lambda rb, b: (b, rb, 0))] + cast_out_specs,
        compiler_params=pltpu.CompilerParams(
            dimension_semantics=("arbitrary", "arbitrary"), vmem_limit_bytes=vmem),
        name="na_attn",
    )(qkv3, qkv3, qkv3, qkv3, qkv3, qkv3, qkv3, bias_table, *cast_in)
    return y, [c.reshape(w.shape[1], w.shape[2]) for c, (w, _) in zip(cast, casts)]


def _t5_bucket_steps(seq):
    half = T5_BUCKETS // 2
    max_exact = half // 2
    rel = np.arange(-(seq - 1), seq)
    n = np.abs(rel)
    nf = np.maximum(n, 1).astype(np.float64)
    large = max_exact + np.floor(
        np.log(nf / max_exact) / math.log(T5_MAX_DIST / max_exact) * (half - max_exact) + 1e-9
    ).astype(np.int64)
    large = np.minimum(large, half - 1)
    bucket = np.where(rel > 0, half, 0) + np.where(n < max_exact, n, large)
    steps = [(int(rel[i]), int(bucket[i])) for i in range(1, len(rel)) if bucket[i] != bucket[i - 1]]
    sat = max(abs(steps[0][0]) + 1, abs(steps[-1][0]))
    return int(bucket[0]), steps, sat


def _diff_kernel(t5_ref, q_ref, k_ref, v_ref, lam_ref, g_ref, o_ref, *scratch, tq, n_buf,
                 lam_init, bucket_steps):
    s_refs, m_refs, ol_refs = (scratch[i * n_buf:(i + 1) * n_buf] for i in range(3))
    corr_ref, kfar_ref = scratch[3 * n_buf:]
    b = pl.program_id(0)
    seq = k_ref.shape[0]
    n_t = seq // tq
    band = 3 * tq
    first_bucket, steps, _ = bucket_steps
    last_bucket = steps[-1][1]
    split = tq + tq // 2
    one_lane = FAR_PARTS * n_t
    assert one_lane + FAR_PARTS <= LANES and n_t >= 4

    @pl.when(b == 0)
    def _init():
        row = lax.broadcasted_iota(jnp.int32, (tq, band), 0)
        col = lax.broadcasted_iota(jnp.int32, (tq, band), 1)
        rel = col - tq - row
        for hh in range(DIFF_HEADS):
            val = jnp.full((tq, band), t5_ref[first_bucket, hh], F32)
            for thr, bkt in steps:
                val = jnp.where(rel >= thr, t5_ref[bkt, hh], val)
            far = jnp.where(col < split, t5_ref[first_bucket, hh], t5_ref[last_bucket, hh])
            corr_ref[hh] = val - far
        corr_ref[DIFF_HEADS] = jnp.zeros((tq, band), F32)
        key = lax.broadcasted_iota(jnp.int32, (seq, LANES), 0)
        lane = lax.broadcasted_iota(jnp.int32, (seq, LANES), 1)
        tile = lax.rem(lane, n_t)
        step = jnp.where(key >= tile * tq + tq // 2, 1.0, 0.0)
        ones = jnp.where(lane < one_lane + FAR_PARTS, 1.0, 0.0)
        kfar_ref[...] = jnp.where(lane < one_lane, step, ones).astype(BF16)

    scale = 1.0 / math.sqrt(DIFF_QK_DIM)
    lane = lax.broadcasted_iota(jnp.int32, (tq, LANES), 1)
    lo = lane < DIFF_QK_DIM
    lp = lam_ref[...]
    lam = (jnp.exp(jnp.sum(lp[0:1] * lp[1:2], axis=-1, keepdims=True))
           - jnp.exp(jnp.sum(lp[2:3] * lp[3:4], axis=-1, keepdims=True)) + lam_init)

    chunk_order = [-1, 0, 1] + list(range(2, n_t - 1))

    def tile_ids(t):
        if isinstance(t, int):
            h, qi = divmod(t, n_t)
        else:
            h, qi = lax.div(t, n_t), lax.rem(t, n_t)
        return h, qi, pl.ds(_aligned(h * LANES, LANES), LANES)

    def key_rows(qi, d):
        chunk = (qi + d) % n_t if isinstance(qi, int) else lax.rem(qi + (d + n_t), n_t)
        return pl.ds(_aligned(chunk * tq, tq), tq)

    def scores(t, s_ref, m_ref):
        h, qi, cols = tile_ids(t)
        c_neg = t5_ref[first_bucket, h]
        c_step = t5_ref[last_bucket, h] - c_neg
        row0 = _aligned(qi * tq, tq)
        q = q_ref[pl.ds(row0, tq), cols] * scale
        zero = jnp.zeros_like(q)
        far = jnp.where(lane < one_lane, jnp.where(lax.rem(lane, n_t) == qi, c_step, 0.0),
                        jnp.where(lane < one_lane + FAR_PARTS, c_neg, 0.0))
        part = jnp.where(lane < one_lane, lax.div(lane, n_t), lane - one_lane)
        q_far = jnp.zeros((tq, LANES), BF16)
        for i in range(FAR_PARTS):
            piece = far.astype(BF16)
            q_far = jnp.where(part == i, piece, q_far)
            far = far - piece.astype(F32)
        lhs = jnp.concatenate(
            [jnp.concatenate([jnp.where(lo, q, zero), q_far], axis=1),
             jnp.concatenate([jnp.where(lo, zero, q), q_far], axis=1)], axis=0)
        m_run = None
        for pos, d in enumerate(chunk_order):
            rows = key_rows(qi, d)
            rhs = jnp.concatenate([k_ref[rows, cols], kfar_ref[rows, :]], axis=1)
            s = lax.dot_general(lhs, rhs, (((1,), (1,)), ((), ())), preferred_element_type=F32)
            if d in (-1, 0, 1):
                inside = (qi + d >= 0) & (qi + d < n_t)
                head = jnp.where(inside, h, DIFF_HEADS)
                corr = corr_ref[head, :, (d + 1) * tq:(d + 2) * tq]
                s = s + jnp.concatenate([corr, corr], axis=0)
            s_ref[:, pos * tq:(pos + 1) * tq] = s
            m_chunk = functools.reduce(
                jnp.maximum, [s[:, i * LANES:(i + 1) * LANES] for i in range(tq // LANES)])
            m_run = m_chunk if m_run is None else jnp.maximum(m_run, m_chunk)
        m_ref[...] = m_run

    def attend(t, s_ref, m_ref, ol_ref):
        _, qi, cols = tile_ids(t)
        m = jnp.broadcast_to(jnp.max(m_ref[...], axis=-1, keepdims=True), (2 * tq, LANES))
        m = jnp.concatenate([m] * (tq // LANES), axis=1)
        ol = None
        for pos, d in enumerate(chunk_order):
            e = jnp.exp((s_ref[:, pos * tq:(pos + 1) * tq] - m).astype(BF16))
            v = v_ref[key_rows(qi, d), cols]
            part = jnp.dot(e, jnp.concatenate([v, jnp.ones_like(v)], axis=1),
                           preferred_element_type=F32)
            ol = part if ol is None else ol + part
        ol_ref[...] = ol

    def finish(t, ol_ref):
        _, qi, cols = tile_ids(t)
        o = (ol_ref[0:tq, 0:LANES] / ol_ref[0:tq, LANES:2 * LANES]
             - lam * (ol_ref[tq:2 * tq, 0:LANES] / ol_ref[tq:2 * tq, LANES:2 * LANES]))
        row0 = _aligned(qi * tq, tq)
        o_ref[pl.ds(row0, tq), cols] = (
            _rmsnorm_f32(o, g_ref[...]) * (1.0 - lam_init)).astype(BF16)

    n_tiles = DIFF_HEADS * n_t

    def step(u, slot):
        live = (lambda t: True) if not isinstance(u, int) else (lambda t: 0 <= t < n_tiles)
        if live(u):
            scores(u, s_refs[slot], m_refs[slot])
        if live(u - 3):
            finish(u - 3, ol_refs[(slot - 3) % n_buf])
        if live(u - 2):
            a = (slot - 2) % n_buf
            attend(u - 2, s_refs[a], m_refs[a], ol_refs[a])

    per_trip = DIFF_ROTATIONS * n_buf
    first, trips = 3, (n_tiles - 3) // per_trip
    assert trips >= 1
    for u in range(first):
        step(u, u % n_buf)

    def rotations(j, carry):
        u = first + per_trip * j
        for i in range(per_trip):
            step(u + i, (first + i) % n_buf)
        return carry

    lax.fori_loop(0, trips, rotations, 0)
    for u in range(first + per_trip * trips, n_tiles + 3):
        step(u, u % n_buf)


def _diff_attention(qkv3, t5_bias, lam_params, subln_g, lam_init):
    B, S, _ = qkv3.shape
    tq = TQ_DIFF
    bucket_steps = _t5_bucket_steps(S)
    assert bucket_steps[2] <= tq and S % tq == 0 and tq % LANES == 0
    assert 2 * DIFF_QK_DIM == LANES and DIFF_V_DIM == LANES
    assert NA_WIDTH == DIFF_WIDTH
    q_col = 3
    n_buf = 3
    scratch = (
        [((2 * tq, S), F32)] * n_buf
        + [((2 * tq, LANES), F32)] * n_buf
        + [((2 * tq, 2 * LANES), F32)] * n_buf
        + [((DIFF_HEADS + 1, tq, 3 * tq), F32),
           ((S, LANES), BF16)])
    vmem = _vmem_limit(4 * _nbytes((S, DIFF_WIDTH), BF16, 2), *[_nbytes(*s) for s in scratch])
    return pl.pallas_call(
        functools.partial(_diff_kernel, tq=tq, n_buf=n_buf, lam_init=lam_init,
                          bucket_steps=bucket_steps),
        out_shape=jax.ShapeDtypeStruct((B, S, DIFF_WIDTH), BF16),
        grid=(B,),
        in_specs=[
            pl.BlockSpec(memory_space=pltpu.SMEM),
            pl.BlockSpec((None, S, DIFF_WIDTH), lambda b: (b, 0, q_col)),
            pl.BlockSpec((None, S, DIFF_WIDTH), lambda b: (b, 0, q_col + 1)),
            pl.BlockSpec((None, S, DIFF_WIDTH), lambda b: (b, 0, q_col + 2)),
            pl.BlockSpec((4, DIFF_QK_DIM), lambda b: (0, 0)),
            pl.BlockSpec((1, DIFF_V_DIM), lambda b: (0, 0)),
        ],
        out_specs=pl.BlockSpec((None, S, DIFF_WIDTH), lambda b: (b, 0, 0)),
        scratch_shapes=[pltpu.VMEM(*s) for s in scratch],
        compiler_params=pltpu.CompilerParams(
            dimension_semantics=("arbitrary",), vmem_limit_bytes=vmem),
        name="diff_attn",
    )(t5_bias, qkv3, qkv3, qkv3, lam_params, subln_g.reshape(1, DIFF_V_DIM))


def _mix_mlp_kernel(x_ref, yna_ref, ydf_ref, gate_ref, wna_ref, wdf_ref, wout_ref, g_ref, w1_ref,
                    w2_ref, fg_ref, o_ref, *, tf, final):
    d = x_ref.shape[1]
    b_na = jnp.dot(yna_ref[...], wna_ref[...], preferred_element_type=F32)
    b_df = jnp.dot(ydf_ref[...], wdf_ref[...], preferred_element_type=F32)
    merged = (jax.nn.sigmoid(gate_ref[:, 0:d]) * b_na
              + jax.nn.sigmoid(gate_ref[:, d:2 * d]) * b_df)
    x = x_ref[...] + jnp.dot(merged.astype(BF16), wout_ref[...], preferred_element_type=F32)
    h = _rmsnorm_f32(x, g_ref[...]).astype(BF16)
    for c in range(w1_ref.shape[1] // tf):
        u = jnp.dot(h, w1_ref[:, c * tf:(c + 1) * tf], preferred_element_type=F32)
        u = jnp.square(jnp.maximum(u, 0.0)).astype(BF16)
        x = x + jnp.dot(u, w2_ref[c * tf:(c + 1) * tf, :], preferred_element_type=F32)
    o_ref[...] = _rmsnorm_f32(x, fg_ref[...]) if final else x


def _mix_mlp(xt, y_na, y_df, gates, w_na, w_df, w_out, g, w1, w2, final_g, final):
    T, D = xt.shape
    F = w1.shape[1]
    tm, tf = TM_MLP, TF_MLP
    once = pl.Buffered(1)
    assert F % tf == 0
    weights = (w_na, w_df, w_out, w1, w2)
    vmem = _vmem_limit(
        2 * _nbytes((tm, D), F32, 2),
        _nbytes((tm, NA_WIDTH), BF16, 2), _nbytes((tm, DIFF_WIDTH), BF16, 2),
        _nbytes((tm, 2 * D), F32, 2), *[_nbytes(w.shape, BF16) for w in weights])
    return pl.pallas_call(
        functools.partial(_mix_mlp_kernel, tf=tf, final=final),
        out_shape=jax.ShapeDtypeStruct((T, D), F32),
        grid=(T // tm,),
        in_specs=[
            pl.BlockSpec((tm, D), lambda i: (i, 0)),
            pl.BlockSpec((tm, NA_WIDTH), lambda i: (i, 0)),
            pl.BlockSpec((tm, DIFF_WIDTH), lambda i: (i, 0)),
            pl.BlockSpec((tm, 2 * D), lambda i: (i, 0)),
            pl.BlockSpec((NA_WIDTH, D), lambda i: (0, 0), pipeline_mode=once),
            pl.BlockSpec((DIFF_WIDTH, D), lambda i: (0, 0), pipeline_mode=once),
            pl.BlockSpec((D, D), lambda i: (0, 0), pipeline_mode=once),
            pl.BlockSpec((1, D), lambda i: (0, 0)),
            pl.BlockSpec((D, F), lambda i: (0, 0), pipeline_mode=once),
            pl.BlockSpec((F, D), lambda i: (0, 0), pipeline_mode=once),
            pl.BlockSpec((1, D), lambda i: (0, 0)),
        ],
        out_specs=pl.BlockSpec((tm, D), lambda i: (i, 0)),
        compiler_params=pltpu.CompilerParams(
            dimension_semantics=("arbitrary",), vmem_limit_bytes=vmem),
        name="mix_mlp",
    )(xt, y_na, y_df, gates, w_na, w_df, w_out, g.reshape(1, D), w1, w2, final_g.reshape(1, D))


def kernel(x, t5_bias, final_norm_g, norm1_g, w_in, na_rpb, diff_lambda, diff_subln_g, w_na_o,
           w_diff_o, w_out, norm2_g, w_ff1, w_ff2):
    B, S, D = x.shape
    depth = w_in.shape[0]
    T = B * S
    assert w_in.shape[2] == QKV_WIDTH + 2 * D and S % GRID_W == 0
    xt = x.reshape(T, D)
    na_bias, w_in_b = _na_bias_table(na_rpb, S // GRID_W, w_in)
    for layer in range(depth):
        qkv, gates = _in_proj(xt, norm1_g[layer], w_in_b)
        qkv3 = qkv.reshape(B, S, QKV_WIDTH)
        casts = [(w, layer) for w in (w_na_o, w_diff_o, w_out, w_ff1, w_ff2)]
        if layer + 1 < depth:
            casts.append((w_in, layer + 1))
        y_na, (w_na_b, w_df_b, w_out_b, w1_b, w2_b, *w_in_next) = _na_attention(
            qkv3, na_bias, layer, casts)
        lam_init = 0.8 - 0.6 * math.exp(-0.3 * layer)
        y_df = _diff_attention(qkv3, t5_bias, diff_lambda[layer], diff_subln_g[layer], lam_init)
        xt = _mix_mlp(xt, y_na.reshape(T, NA_WIDTH), y_df.reshape(T, DIFF_WIDTH), gates,
                      w_na_b, w_df_b, w_out_b, norm2_g[layer], w1_b, w2_b, final_norm_g,
                      final=(layer == depth - 1))
        if w_in_next:
            w_in_b = w_in_next[0]
    return xt.reshape(B, S, D)
```

```python
import functools
import math

import numpy as np
import jax
import jax.numpy as jnp
from jax import lax
from jax.experimental import pallas as pl
from jax.experimental.pallas import tpu as pltpu

GRID_W = 64
NA_HEADS = 8
NA_HEAD_DIM = 64
NA_WIN_ROWS = 8
NA_WIN_COLS = 16
DIFF_HEADS = 4
DIFF_QK_DIM = 64
DIFF_V_DIM = 2 * DIFF_QK_DIM
T5_BUCKETS = 32
T5_MAX_DIST = 128
RMS_EPS = 1e-6
NEG_INF = -1e30

NA_WIDTH = NA_HEADS * NA_HEAD_DIM
DIFF_WIDTH = DIFF_HEADS * DIFF_V_DIM
QKV_WIDTH = 3 * NA_WIDTH + 3 * DIFF_WIDTH

LANES = 128
BF16_SUBLANES = 16
V7X_VMEM_BYTES = 64 * 2**20
VMEM_TEMP_BYTES = 12 * 2**20

TM_PROJ = 1024
TR_PROJ = 512
TN_PROJ = 1024
TM_MLP = 512
TF_MLP = 1024
NA_ROW_BLOCK = 4
NA_WIN_BLOCKS = 3
NA_BATCH_BLOCK = 4
TQ_DIFF = 256
FAR_PARTS = 3
DIFF_ROTATIONS = 2

BF16 = jnp.bfloat16
F32 = jnp.float32


def _nbytes(shape, dtype, buffers=1):
    return buffers * math.prod(shape) * jnp.dtype(dtype).itemsize


def _vmem_limit(*window_bytes):
    total = sum(window_bytes) + VMEM_TEMP_BYTES
    assert total < V7X_VMEM_BYTES
    return int(total)


def _aligned(index, multiple):
    return index if isinstance(index, int) else pl.multiple_of(index, multiple)


def _rmsnorm_f32(x, g):
    return (x * lax.rsqrt(jnp.mean(x * x, axis=-1, keepdims=True) + RMS_EPS)) * g


def _in_proj_kernel(x_ref, g_ref, w_ref, qkv_ref, gate_ref, *, rows, chunk):
    n_qkv = qkv_ref.shape[1]
    n_gate = gate_ref.shape[1]
    for r in range(x_ref.shape[0] // rows):
        rs = slice(r * rows, (r + 1) * rows)
        hb = _rmsnorm_f32(x_ref[rs, :], g_ref[...]).astype(BF16)
        for c in range(n_qkv // chunk):
            cs = slice(c * chunk, (c + 1) * chunk)
            qkv_ref[rs, cs] = jnp.dot(hb, w_ref[:, cs], preferred_element_type=F32).astype(BF16)
        for c in range(n_gate // chunk):
            cs = slice(c * chunk, (c + 1) * chunk)
            ws = slice(n_qkv + c * chunk, n_qkv + (c + 1) * chunk)
            gate_ref[rs, cs] = jnp.dot(hb, w_ref[:, ws], preferred_element_type=F32)


def _in_proj(xt, g, w):
    T, D = xt.shape
    n_all = w.shape[1]
    n_gate = n_all - QKV_WIDTH
    tm = TM_PROJ
    vmem = _vmem_limit(_nbytes((tm, D), F32, 2), _nbytes((D, n_all), BF16),
                       _nbytes((tm, QKV_WIDTH), BF16, 2), _nbytes((tm, n_gate), F32, 2))
    return pl.pallas_call(
        functools.partial(_in_proj_kernel, rows=TR_PROJ, chunk=TN_PROJ),
        out_shape=(jax.ShapeDtypeStruct((T, QKV_WIDTH), BF16),
                   jax.ShapeDtypeStruct((T, n_gate), F32)),
        grid=(T // tm,),
        in_specs=[
            pl.BlockSpec((tm, D), lambda i: (i, 0)),
            pl.BlockSpec((1, D), lambda i: (0, 0)),
            pl.BlockSpec((D, n_all), lambda i: (0, 0), pipeline_mode=pl.Buffered(1)),
        ],
        out_specs=(pl.BlockSpec((tm, QKV_WIDTH), lambda i: (i, 0)),
                   pl.BlockSpec((tm, n_gate), lambda i: (i, 0))),
        compiler_params=pltpu.CompilerParams(
            dimension_semantics=("arbitrary",), vmem_limit_bytes=vmem),
        name="in_proj",
    )(xt, g.reshape(1, D), w)


def _na_geometry(rows):
    rb_rows = NA_ROW_BLOCK
    n_rb = rows // rb_rows
    win_rows = NA_WIN_BLOCKS * rb_rows
    wr = min(NA_WIN_ROWS, rows)
    kb = np.clip(np.arange(n_rb) - 1, 0, n_rb - NA_WIN_BLOCKS)
    dr_idx = np.zeros((n_rb, rb_rows, win_rows), np.int32)
    valid = np.zeros((n_rb, rb_rows, win_rows), bool)
    for rb in range(n_rb):
        w0 = kb[rb] * rb_rows
        for ri in range(rb_rows):
            r = rb * rb_rows + ri
            r0 = min(max(r - wr // 2, 0), rows - wr)
            assert w0 <= r0 and r0 + wr <= w0 + win_rows
            for wj in range(win_rows):
                krow = w0 + wj
                valid[rb, ri, wj] = r0 <= krow < r0 + wr
                dr_idx[rb, ri, wj] = min(max(krow - r + NA_WIN_ROWS - 1, 0), 2 * NA_WIN_ROWS - 2)
    for rb in range(2, n_rb - 1):
        assert (valid[rb] == valid[1]).all() and (dr_idx[rb] == dr_idx[1]).all()
    classes = [0, 1, n_rb - 1]
    return n_rb, kb, dr_idx[classes], valid[classes]


def _na_bias_kernel(rpb_ref, w_ref, o_ref, wb_ref, *, dr_idx, row_valid):
    assert LANES == 2 * GRID_W
    n_cls, rb_rows, win_rows = dr_idx.shape
    c = lax.broadcasted_iota(jnp.int32, (GRID_W, LANES), 0)
    lane = lax.broadcasted_iota(jnp.int32, (GRID_W, LANES), 1)
    left = lane < GRID_W
    kc = jnp.where(left, lane, lane - GRID_W)
    c0 = jnp.clip(c - NA_WIN_COLS // 2, 0, GRID_W - NA_WIN_COLS)
    col_ok = (kc >= c0) & (kc < c0 + NA_WIN_COLS)
    neg = jnp.full((GRID_W, LANES), NEG_INF, F32)

    @functools.lru_cache(maxsize=None)
    def half_tile(d, right):
        w = jnp.broadcast_to(rpb_ref[d:d + 1, :], (GRID_W, LANES))
        t = pltpu.roll(w, GRID_W if right else 0, 1, stride=1, stride_axis=0)
        return jnp.where(col_ok, t, neg)

    for cls in range(n_cls):
        for ri in range(rb_rows):
            for p in range(win_rows // 2):
                halves = []
                for side in range(2):
                    wj = 2 * p + side
                    ok = bool(row_valid[cls, ri, wj])
                    halves.append(half_tile(int(dr_idx[cls, ri, wj]), side == 1) if ok else neg)
                o_ref[cls, ri * GRID_W:(ri + 1) * GRID_W, p * LANES:(p + 1) * LANES] = (
                    jnp.where(left, halves[0], halves[1]))
    wb_ref[...] = w_ref[...].astype(BF16)


def _na_bias_table(na_rpb, rows, w_in):
    depth, H, n_dr, n_dc = na_rpb.shape
    assert n_dr == 2 * NA_WIN_ROWS - 1 and n_dc == 2 * NA_WIN_COLS - 1
    _, _, dr_idx, row_valid = _na_geometry(rows)
    n_cls, rb_rows, win_rows = dr_idx.shape
    rq, wk = rb_rows * GRID_W, win_rows * GRID_W
    lanes = jnp.pad(na_rpb, ((0, 0), (0, 0), (0, 0), (0, LANES - n_dc)), constant_values=NEG_INF)
    lanes = jnp.roll(lanes, -(NA_WIN_COLS - 1), axis=-1)
    n_steps = depth * H
    _, R, C = w_in.shape
    slab = R // n_steps
    assert R % n_steps == 0 and slab % BF16_SUBLANES == 0
    table, w0 = pl.pallas_call(
        functools.partial(_na_bias_kernel, dr_idx=dr_idx, row_valid=row_valid),
        out_shape=(jax.ShapeDtypeStruct((depth, n_cls, H, rq, wk), F32),
                   jax.ShapeDtypeStruct((n_steps, slab, C), BF16)),
        grid=(depth, H),
        in_specs=[pl.BlockSpec((None, None, n_dr, LANES), lambda l, h: (l, h, 0, 0)),
                  pl.BlockSpec((None, slab, C), lambda l, h: (l * H + h, 0, 0))],
        out_specs=(pl.BlockSpec((None, n_cls, None, rq, wk), lambda l, h: (l, 0, h, 0, 0)),
                   pl.BlockSpec((None, slab, C), lambda l, h: (l * H + h, 0, 0))),
        compiler_params=pltpu.CompilerParams(dimension_semantics=("arbitrary", "arbitrary")),
        name="na_bias",
    )(lanes, w_in.reshape(-1, slab, C))
    return table, w0.reshape(R, C)


def _na_kernel(q_ref, k0_ref, k1_ref, k2_ref, v0_ref, v1_ref, v2_ref, bias_ref, *rest):
    n_cast = len(rest) // 2
    o_ref = rest[n_cast]
    for w_ref, wb_ref in zip(rest[:n_cast], rest[n_cast + 1:]):
        wb_ref[...] = w_ref[...].astype(BF16)
    nb, rq = q_ref.shape[0], q_ref.shape[1]
    scale = 1.0 / math.sqrt(NA_HEAD_DIM)
    lo = lax.broadcasted_iota(jnp.int32, (rq, LANES), 1) < NA_HEAD_DIM
    heads_per_vreg = LANES // NA_HEAD_DIM
    for bi, hp in np.ndindex(nb, NA_HEADS // heads_per_vreg):
        cs = (bi, slice(None), slice(hp * LANES, (hp + 1) * LANES))
        q = q_ref[cs] * scale
        k = jnp.concatenate([k0_ref[cs], k1_ref[cs], k2_ref[cs]], axis=0)
        v = jnp.concatenate([v0_ref[cs], v1_ref[cs], v2_ref[cs]], axis=0)
        v_ones = jnp.concatenate([v, jnp.ones_like(v)], axis=1)
        outs = []
        for e in range(heads_per_vreg):
            qm = jnp.where(lo if e == 0 else jnp.logical_not(lo), q, jnp.zeros_like(q))
            s = lax.dot_general(qm, k, (((1,), (1,)), ((), ())), preferred_element_type=F32)
            s = s + bias_ref[heads_per_vreg * hp + e]
            m = jnp.max(s, axis=-1, keepdims=True)
            p = jnp.exp((s - m).astype(BF16))
            ol = jnp.dot(p, v_ones, preferred_element_type=F32)
            outs.append(ol[:, 0:LANES] / ol[:, LANES:2 * LANES])
        o_ref[cs] = jnp.where(lo, outs[0], outs[1]).astype(BF16)


def _na_attention(qkv3, bias_table, layer, casts):
    B, S, _ = qkv3.shape
    rows = S // GRID_W
    n_rb, kb, _, _ = _na_geometry(rows)
    rq = NA_ROW_BLOCK * GRID_W
    assert NA_HEADS * NA_HEAD_DIM == NA_WIDTH and LANES % NA_HEAD_DIM == 0
    k_col, v_col = 1, 2
    n_kb = n_rb - NA_WIN_BLOCKS

    nb = NA_BATCH_BLOCK
    assert B % nb == 0

    def kv_spec(col, t):
        return pl.BlockSpec((nb, rq, NA_WIDTH),
                            lambda rb, b: (b, jnp.clip(rb - 1, 0, n_kb) + t, col))

    def bias_map(rb, b):
        cls = (rb > 0).astype(jnp.int32) + (rb == n_rb - 1).astype(jnp.int32)
        return (layer, cls, 0, 0, 0)

    n_bb = B // nb
    n_steps = n_rb * n_bb
    cast_in, cast_in_specs, cast_out, cast_out_specs = [], [], [], []
    for w, w_layer in casts:
        depth, R, C = w.shape
        slab = R // n_steps
        assert R % n_steps == 0 and slab % BF16_SUBLANES == 0
        cast_in.append(w.reshape(depth * n_steps, slab, C))
        first = w_layer * n_steps
        cast_in_specs.append(pl.BlockSpec(
            (None, slab, C), lambda rb, b, first=first: (first + rb * n_bb + b, 0, 0)))
        cast_out.append(jax.ShapeDtypeStruct((n_steps, slab, C), BF16))
        cast_out_specs.append(pl.BlockSpec((None, slab, C), lambda rb, b: (rb * n_bb + b, 0, 0)))

    wk = NA_WIN_BLOCKS * rq
    n_qkv_windows = 2 + 2 * NA_WIN_BLOCKS
    vmem = _vmem_limit(
        n_qkv_windows * _nbytes((nb, rq, NA_WIDTH), BF16, 2),
        _nbytes((NA_HEADS, rq, wk), F32, 2),
        *[_nbytes(o.shape[1:], F32, 2) + _nbytes(o.shape[1:], BF16, 2) for o in cast_out])
    y, *cast = pl.pallas_call(
        _na_kernel,
        out_shape=[jax.ShapeDtypeStruct((B, S, NA_WIDTH), BF16)] + cast_out,
        grid=(n_rb, n_bb),
        in_specs=[pl.BlockSpec((nb, rq, NA_WIDTH), lambda rb, b: (b, rb, 0))]
        + [kv_spec(k_col, t) for t in range(NA_WIN_BLOCKS)]
        + [kv_spec(v_col, t) for t in range(NA_WIN_BLOCKS)]
        + [pl.BlockSpec((None, None, NA_HEADS, rq, wk), bias_map)]
        + cast_in_specs,
        out_specs=[pl.BlockSpec((nb, rq, NA_WIDTH), lambda rb, b: (b, rb, 0))] + cast_out_specs,
        compiler_params=pltpu.CompilerParams(
            dimension_semantics=("arbitrary", "arbitrary"), vmem_limit_bytes=vmem),
        name="na_attn",
    )(qkv3, qkv3, qkv3, qkv3, qkv3, qkv3, qkv3, bias_table, *cast_in)
    return y, [c.reshape(w.shape[1], w.shape[2]) for c, (w, _) in zip(cast, casts)]


def _t5_bucket_steps(seq):
    half = T5_BUCKETS // 2
    max_exact = half // 2
    rel = np.arange(-(seq - 1), seq)
    n = np.abs(rel)
    nf = np.maximum(n, 1).astype(np.float64)
    large = max_exact + np.floor(
        np.log(nf / max_exact) / math.log(T5_MAX_DIST / max_exact) * (half - max_exact) + 1e-9
    ).astype(np.int64)
    large = np.minimum(large, half - 1)
    bucket = np.where(rel > 0, half, 0) + np.where(n < max_exact, n, large)
    steps = [(int(rel[i]), int(bucket[i])) for i in range(1, len(rel)) if bucket[i] != bucket[i - 1]]
    sat = max(abs(steps[0][0]) + 1, abs(steps[-1][0]))
    return int(bucket[0]), steps, sat


def _diff_kernel(t5_ref, qkv_ref, lam_ref, g_ref, o_ref, *scratch, tq, n_buf, lam_init,
                 bucket_steps):
    s_refs, m_refs, ol_refs = (scratch[i * n_buf:(i + 1) * n_buf] for i in range(3))
    corr_ref, kfar_ref = scratch[3 * n_buf:]
    b = pl.program_id(0)
    seq = qkv_ref.shape[0]
    n_t = seq // tq
    band = 3 * tq
    first_bucket, steps, _ = bucket_steps
    last_bucket = steps[-1][1]
    split = tq + tq // 2
    one_lane = FAR_PARTS * n_t
    assert one_lane + FAR_PARTS <= LANES and n_t >= 4

    @pl.when(b == 0)
    def _init():
        row = lax.broadcasted_iota(jnp.int32, (tq, band), 0)
        col = lax.broadcasted_iota(jnp.int32, (tq, band), 1)
        rel = col - tq - row
        for hh in range(DIFF_HEADS):
            val = jnp.full((tq, band), t5_ref[first_bucket, hh], F32)
            for thr, bkt in steps:
                val = jnp.where(rel >= thr, t5_ref[bkt, hh], val)
            far = jnp.where(col < split, t5_ref[first_bucket, hh], t5_ref[last_bucket, hh])
            corr_ref[hh] = val - far
        corr_ref[DIFF_HEADS] = jnp.zeros((tq, band), F32)
        key = lax.broadcasted_iota(jnp.int32, (seq, LANES), 0)
        lane = lax.broadcasted_iota(jnp.int32, (seq, LANES), 1)
        tile = lax.rem(lane, n_t)
        step = jnp.where(key >= tile * tq + tq // 2, 1.0, 0.0)
        ones = jnp.where(lane < one_lane + FAR_PARTS, 1.0, 0.0)
        kfar_ref[...] = jnp.where(lane < one_lane, step, ones).astype(BF16)

    scale = 1.0 / math.sqrt(DIFF_QK_DIM)
    lane = lax.broadcasted_iota(jnp.int32, (tq, LANES), 1)
    lo = lane < DIFF_QK_DIM
    lp = lam_ref[...]
    lam = (jnp.exp(jnp.sum(lp[0:1] * lp[1:2], axis=-1, keepdims=True))
           - jnp.exp(jnp.sum(lp[2:3] * lp[3:4], axis=-1, keepdims=True)) + lam_init)

    chunk_order = [-1, 0, 1] + list(range(2, n_t - 1))

    def tile_ids(t):
        if isinstance(t, int):
            h, qi = divmod(t, n_t)
        else:
            h, qi = lax.div(t, n_t), lax.rem(t, n_t)
        return h, qi, pl.ds(_aligned(h * LANES, LANES), LANES)

    def qkv_cols(h, part):
        return pl.ds(_aligned(part * DIFF_WIDTH + h * LANES, LANES), LANES)

    def key_rows(qi, d):
        chunk = (qi + d) % n_t if isinstance(qi, int) else lax.rem(qi + (d + n_t), n_t)
        return pl.ds(_aligned(chunk * tq, tq), tq)

    def scores(t, s_ref, m_ref):
        h, qi, _ = tile_ids(t)
        c_neg = t5_ref[first_bucket, h]
        c_step = t5_ref[last_bucket, h] - c_neg
        row0 = _aligned(qi * tq, tq)
        q = qkv_ref[pl.ds(row0, tq), qkv_cols(h, 0)] * scale
        zero = jnp.zeros_like(q)
        far = jnp.where(lane < one_lane, jnp.where(lax.rem(lane, n_t) == qi, c_step, 0.0),
                        jnp.where(lane < one_lane + FAR_PARTS, c_neg, 0.0))
        part = jnp.where(lane < one_lane, lax.div(lane, n_t), lane - one_lane)
        q_far = jnp.zeros((tq, LANES), BF16)
        for i in range(FAR_PARTS):
            piece = far.astype(BF16)
            q_far = jnp.where(part == i, piece, q_far)
            far = far - piece.astype(F32)
        lhs = jnp.concatenate(
            [jnp.concatenate([jnp.where(lo, q, zero), q_far], axis=1),
             jnp.concatenate([jnp.where(lo, zero, q), q_far], axis=1)], axis=0)
        m_run = None
        for pos, d in enumerate(chunk_order):
            rows = key_rows(qi, d)
            rhs = jnp.concatenate([qkv_ref[rows, qkv_cols(h, 1)], kfar_ref[rows, :]], axis=1)
            s = lax.dot_general(lhs, rhs, (((1,), (1,)), ((), ())), preferred_element_type=F32)
            if d in (-1, 0, 1):
                inside = (qi + d >= 0) & (qi + d < n_t)
                head = jnp.where(inside, h, DIFF_HEADS)
                corr = corr_ref[head, :, (d + 1) * tq:(d + 2) * tq]
                s = s + jnp.concatenate([corr, corr], axis=0)
            s_ref[:, pos * tq:(pos + 1) * tq] = s
            m_chunk = functools.reduce(
                jnp.maximum, [s[:, i * LANES:(i + 1) * LANES] for i in range(tq // LANES)])
            m_run = m_chunk if m_run is None else jnp.maximum(m_run, m_chunk)
        m_ref[...] = m_run

    def attend(t, s_ref, m_ref, ol_ref):
        h, qi, _ = tile_ids(t)
        m = jnp.broadcast_to(jnp.max(m_ref[...], axis=-1, keepdims=True), (2 * tq, LANES))
        m = jnp.concatenate([m] * (tq // LANES), axis=1)
        ol = None
        for pos, d in enumerate(chunk_order):
            e = jnp.exp((s_ref[:, pos * tq:(pos + 1) * tq] - m).astype(BF16))
            v = qkv_ref[key_rows(qi, d), qkv_cols(h, 2)]
            part = jnp.dot(e, jnp.concatenate([v, jnp.ones_like(v)], axis=1),
                           preferred_element_type=F32)
            ol = part if ol is None else ol + part
        ol_ref[...] = ol

    def finish(t, ol_ref):
        _, qi, cols = tile_ids(t)
        o = (ol_ref[0:tq, 0:LANES] / ol_ref[0:tq, LANES:2 * LANES]
             - lam * (ol_ref[tq:2 * tq, 0:LANES] / ol_ref[tq:2 * tq, LANES:2 * LANES]))
        row0 = _aligned(qi * tq, tq)
        o_ref[pl.ds(row0, tq), cols] = (
            _rmsnorm_f32(o, g_ref[...]) * (1.0 - lam_init)).astype(BF16)

    n_tiles = DIFF_HEADS * n_t

    def step(u, slot):
        live = (lambda t: True) if not isinstance(u, int) else (lambda t: 0 <= t < n_tiles)
        if live(u):
            scores(u, s_refs[slot], m_refs[slot])
        if live(u - 3):
            finish(u - 3, ol_refs[(slot - 3) % n_buf])
        if live(u - 2):
            a = (slot - 2) % n_buf
            attend(u - 2, s_refs[a], m_refs[a], ol_refs[a])

    per_trip = DIFF_ROTATIONS * n_buf
    first, trips = 3, (n_tiles - 3) // per_trip
    assert trips >= 1
    for u in range(first):
        step(u, u % n_buf)

    def rotations(j, carry):
        u = first + per_trip * j
        for i in range(per_trip):
            step(u + i, (first + i) % n_buf)
        return carry

    lax.fori_loop(0, trips, rotations, 0)
    for u in range(first + per_trip * trips, n_tiles + 3):
        step(u, u % n_buf)


def _diff_attention(qkv3, t5_bias, lam_params, subln_g, lam_init):
    B, S, _ = qkv3.shape
    tq = TQ_DIFF
    bucket_steps = _t5_bucket_steps(S)
    assert bucket_steps[2] <= tq and S % tq == 0 and tq % LANES == 0
    assert 2 * DIFF_QK_DIM == LANES and DIFF_V_DIM == LANES
    assert 3 * NA_WIDTH == 3 * DIFF_WIDTH
    n_buf = 3
    scratch = (
        [((2 * tq, S), F32)] * n_buf
        + [((2 * tq, LANES), F32)] * n_buf
        + [((2 * tq, 2 * LANES), F32)] * n_buf
        + [((DIFF_HEADS + 1, tq, 3 * tq), F32),
           ((S, LANES), BF16)])
    vmem = _vmem_limit(4 * _nbytes((S, DIFF_WIDTH), BF16, 2), *[_nbytes(*s) for s in scratch])
    return pl.pallas_call(
        functools.partial(_diff_kernel, tq=tq, n_buf=n_buf, lam_init=lam_init,
                          bucket_steps=bucket_steps),
        out_shape=jax.ShapeDtypeStruct((B, S, DIFF_WIDTH), BF16),
        grid=(B,),
        in_specs=[
            pl.BlockSpec(memory_space=pltpu.SMEM),
            pl.BlockSpec((None, S, 3 * DIFF_WIDTH), lambda b: (b, 0, 1)),
            pl.BlockSpec((4, DIFF_QK_DIM), lambda b: (0, 0)),
            pl.BlockSpec((1, DIFF_V_DIM), lambda b: (0, 0)),
        ],
        out_specs=pl.BlockSpec((None, S, DIFF_WIDTH), lambda b: (b, 0, 0)),
        scratch_shapes=[pltpu.VMEM(*s) for s in scratch],
        compiler_params=pltpu.CompilerParams(
            dimension_semantics=("arbitrary",), vmem_limit_bytes=vmem),
        name="diff_attn",
    )(t5_bias, qkv3, lam_params, subln_g.reshape(1, DIFF_V_DIM))


def _mix_mlp_kernel(x_ref, yna_ref, ydf_ref, gate_ref, wna_ref, wdf_ref, wout_ref, g_ref, w1_ref,
                    w2_ref, fg_ref, o_ref, *, tf, final):
    d = x_ref.shape[1]
    b_na = jnp.dot(yna_ref[...], wna_ref[...], preferred_element_type=F32)
    b_df = jnp.dot(ydf_ref[...], wdf_ref[...], preferred_element_type=F32)
    merged = (jax.nn.sigmoid(gate_ref[:, 0:d]) * b_na
              + jax.nn.sigmoid(gate_ref[:, d:2 * d]) * b_df)
    x = x_ref[...] + jnp.dot(merged.astype(BF16), wout_ref[...], preferred_element_type=F32)
    h = _rmsnorm_f32(x, g_ref[...]).astype(BF16)
    for c in range(w1_ref.shape[1] // tf):
        u = jnp.dot(h, w1_ref[:, c * tf:(c + 1) * tf], preferred_element_type=F32)
        u = jnp.square(jnp.maximum(u, 0.0)).astype(BF16)
        x = x + jnp.dot(u, w2_ref[c * tf:(c + 1) * tf, :], preferred_element_type=F32)
    o_ref[...] = _rmsnorm_f32(x, fg_ref[...]) if final else x


def _mix_mlp(xt, y_na, y_df, gates, w_na, w_df, w_out, g, w1, w2, final_g, final):
    T, D = xt.shape
    F = w1.shape[1]
    tm, tf = TM_MLP, TF_MLP
    once = pl.Buffered(1)
    assert F % tf == 0
    weights = (w_na, w_df, w_out, w1, w2)
    vmem = _vmem_limit(
        2 * _nbytes((tm, D), F32, 2),
        _nbytes((tm, NA_WIDTH), BF16, 2), _nbytes((tm, DIFF_WIDTH), BF16, 2),
        _nbytes((tm, 2 * D), F32, 2), *[_nbytes(w.shape, BF16) for w in weights])
    return pl.pallas_call(
        functools.partial(_mix_mlp_kernel, tf=tf, final=final),
        out_shape=jax.ShapeDtypeStruct((T, D), F32),
        grid=(T // tm,),
        in_specs=[
            pl.BlockSpec((tm, D), lambda i: (i, 0)),
            pl.BlockSpec((tm, NA_WIDTH), lambda i: (i, 0)),
            pl.BlockSpec((tm, DIFF_WIDTH), lambda i: (i, 0)),
            pl.BlockSpec((tm, 2 * D), lambda i: (i, 0)),
            pl.BlockSpec((NA_WIDTH, D), lambda i: (0, 0), pipeline_mode=once),
            pl.BlockSpec((DIFF_WIDTH, D), lambda i: (0, 0), pipeline_mode=once),
            pl.BlockSpec((D, D), lambda i: (0, 0), pipeline_mode=once),
            pl.BlockSpec((1, D), lambda i: (0, 0)),
            pl.BlockSpec((D, F), lambda i: (0, 0), pipeline_mode=once),
            pl.BlockSpec((F, D), lambda i: (0, 0), pipeline_mode=once),
            pl.BlockSpec((1, D), lambda i: (0, 0)),
        ],
        out_specs=pl.BlockSpec((tm, D), lambda i: (i, 0)),
        compiler_params=pltpu.CompilerParams(
            dimension_semantics=("arbitrary",), vmem_limit_bytes=vmem),
        name="mix_mlp",
    )(xt, y_na, y_df, gates, w_na, w_df, w_out, g.reshape(1, D), w1, w2, final_g.reshape(1, D))


def kernel(x, t5_bias, final_norm_g, norm1_g, w_in, na_rpb, diff_lambda, diff_subln_g, w_na_o,
           w_diff_o, w_out, norm2_g, w_ff1, w_ff2):
    B, S, D = x.shape
    depth = w_in.shape[0]
    T = B * S
    assert w_in.shape[2] == QKV_WIDTH + 2 * D and S % GRID_W == 0
    xt = x.reshape(T, D)
    na_bias, w_in_b = _na_bias_table(na_rpb, S // GRID_W, w_in)
    for layer in range(depth):
        qkv, gates = _in_proj(xt, norm1_g[layer], w_in_b)
        qkv3 = qkv.reshape(B, S, QKV_WIDTH)
        casts = [(w, layer) for w in (w_na_o, w_diff_o, w_out, w_ff1, w_ff2)]
        if layer + 1 < depth:
            casts.append((w_in, layer + 1))
        y_na, (w_na_b, w_df_b, w_out_b, w1_b, w2_b, *w_in_next) = _na_attention(
            qkv3, na_bias, layer, casts)
        lam_init = 0.8 - 0.6 * math.exp(-0.3 * layer)
        y_df = _diff_attention(qkv3, t5_bias, diff_lambda[layer], diff_subln_g[layer], lam_init)
        xt = _mix_mlp(xt, y_na.reshape(T, NA_WIDTH), y_df.reshape(T, DIFF_WIDTH), gates,
                      w_na_b, w_df_b, w_out_b, norm2_g[layer], w1_b, w2_b, final_norm_g,
                      final=(layer == depth - 1))
        if w_in_next:
            w_in_b = w_in_next[0]
    return xt.reshape(B, S, D)
```
